```python
import math
import jax
import jax.numpy as jnp
from jax import lax
import numpy as np

D_MODEL = 1024
BATCH = 8
SEQ = 4096
DEPTH = 1

CTX_LEN = 256
GRID_W = 64
N_BRANCHES = 2
EPS = 1e-6
F32 = jnp.float32

DN_HEADS = 8
DN_DK = 128
DN_DV = 128
DN_CONV = 5
DN_CHUNK = 64
MLA_HEADS = 8
MLA_Q_RANK = 384
MLA_KV_RANK = 256
MLA_NOPE = 128
MLA_ROPE = 64
MLA_QK_DIM = MLA_NOPE + MLA_ROPE
MLA_V = 128
Q_BLOCK = 128
ROPE_BASE = 10000.0
N_EXPERTS = 16
EC_CAPACITY = 2
EXPERT_FF = 1408

DN_QKV_DIM = 2 * DN_HEADS * DN_DK + DN_HEADS * DN_DV
OFF_Z = DN_QKV_DIM
OFF_ALPHA = OFF_Z + DN_HEADS * DN_DV
OFF_BETA = OFF_ALPHA + 2 * DN_HEADS
OFF_CQ = OFF_BETA + 2 * DN_HEADS
OFF_CKV = OFF_CQ + MLA_Q_RANK
OFF_KROPE = OFF_CKV + MLA_KV_RANK
OFF_GATE = OFF_KROPE + MLA_ROPE
D_IN = OFF_GATE + N_BRANCHES * D_MODEL
IN_OFFSETS = (OFF_Z, OFF_ALPHA, OFF_BETA, OFF_CQ, OFF_CKV, OFF_KROPE, OFF_GATE)

kernel_name = "hybrid_deltanet_mla_ec_moe_dit_block"


def rms_norm(x, g):
    xf = x.astype(F32)
    y = xf * lax.rsqrt(jnp.mean(xf * xf, axis=-1, keepdims=True) + EPS)
    return y.astype(x.dtype) * g


def l2_norm(x):
    return x * lax.rsqrt(jnp.sum(x * x, axis=-1, keepdims=True) + EPS)


def adaln(cond, ada_w, ada_b):
    mod = jax.nn.silu(cond) @ ada_w + ada_b
    return jnp.split(mod[:, None, :], 6, axis=-1)


def modulate(x, g, shift, scale):
    return rms_norm(x, g) * (1.0 + scale) + shift


def short_conv(x, w):
    pad = DN_CONV // 2
    y = lax.conv_general_dilated(x, w[:, None, :], window_strides=(1,), padding=[(pad, pad)],
                                 dimension_numbers=('NWC', 'WIO', 'NWC'), feature_group_count=x.shape[-1])
    return jax.nn.silu(y)


def gated_delta_chunked(q, k, v, g, beta, s0):
    b_, n_tok, n_h, dk = q.shape
    dv = v.shape[-1]
    n_chunk = n_tok // DN_CHUNK

    def to_chunks(t):
        return jnp.moveaxis(t.reshape(b_, n_chunk, DN_CHUNK, n_h, *t.shape[3:]), 3, 1)

    q, k, v, g, beta = (to_chunks(t) for t in (q, k, v, g, beta))
    q = q * dk ** -0.5
    g = jnp.cumsum(g, axis=-1)
    incl = jnp.tril(jnp.ones((DN_CHUNK, DN_CHUNK), dtype=bool))
    strict = jnp.tril(jnp.ones((DN_CHUNK, DN_CHUNK), dtype=bool), -1)
    decay = jnp.exp(jnp.where(incl, g[..., :, None] - g[..., None, :], -jnp.inf))
    k_beta = k * beta[..., None]
    m = jnp.where(strict, jnp.einsum('bhnid,bhnjd->bhnij', k_beta, k) * decay, 0.0)
    eye = jnp.eye(DN_CHUNK, dtype=q.dtype)
    t_inv = lax.linalg.triangular_solve(m + eye, jnp.broadcast_to(eye, m.shape), left_side=True,
                                        lower=True, unit_diagonal=True)
    u = jnp.einsum('bhnij,bhnjd->bhnid', t_inv, v * beta[..., None])
    w = jnp.einsum('bhnij,bhnjd->bhnid', t_inv, k_beta * jnp.exp(g)[..., None])
    qk = jnp.where(incl, jnp.einsum('bhnid,bhnjd->bhnij', q, k) * decay, 0.0)
    q_dec = q * jnp.exp(g)[..., None]
    k_dec = k * jnp.exp(g[..., -1:] - g)[..., None]
    chunk_decay = jnp.exp(g[..., -1])

    def chunk_step(s, xs):
        qk_c, q_c, k_c, u_c, w_c, a_c = xs
        v_new = u_c - jnp.einsum('bhik,bhkv->bhiv', w_c, s)
        o_c = jnp.einsum('bhik,bhkv->bhiv', q_c, s) + jnp.einsum('bhij,bhjv->bhiv', qk_c, v_new)
        s = s * a_c[..., None, None] + jnp.einsum('bhik,bhiv->bhkv', k_c, v_new)
        return s, o_c

    xs = tuple(jnp.moveaxis(t, 2, 0) for t in (qk, q_dec, k_dec, u, w, chunk_decay))
    s_final, o = lax.scan(chunk_step, s0, xs)
    o = jnp.transpose(o, (1, 0, 3, 2, 4)).reshape(b_, n_tok, n_h, dv)
    return o, s_final


def deltanet_inputs(qkv, alpha, beta_logit, conv_w, a_log, dt_bias):
    b_, n_tok, _ = qkv.shape
    qkv = short_conv(qkv, conv_w)
    q, k, v = jnp.split(qkv, [DN_HEADS * DN_DK, 2 * DN_HEADS * DN_DK], axis=-1)
    q = l2_norm(q.astype(F32).reshape(b_, n_tok, DN_HEADS, DN_DK))
    k = l2_norm(k.astype(F32).reshape(b_, n_tok, DN_HEADS, DN_DK))
    v = v.astype(F32).reshape(b_, n_tok, DN_HEADS, DN_DV)
    alpha = alpha.astype(F32).reshape(b_, n_tok, 2, DN_HEADS)
    g = -jnp.exp(a_log.astype(F32)) * jax.nn.softplus(alpha + dt_bias.astype(F32))
    beta = jax.nn.sigmoid(beta_logit.astype(F32).reshape(b_, n_tok, 2, DN_HEADS))
    return q, k, v, g, beta


def deltanet_bidir(q, k, v, g, beta, s0_fwd, s0_bwd):
    o_f, s_f = gated_delta_chunked(q, k, v, g[:, :, 0], beta[:, :, 0], s0_fwd)
    flip = lambda t: jnp.flip(t, axis=1)
    o_b, s_b = gated_delta_chunked(flip(q), flip(k), flip(v), flip(g[:, :, 1]), flip(beta[:, :, 1]), s0_bwd)
    return o_f + flip(o_b), s_f, s_b


def deltanet_out(o, z, norm_g):
    b_, n_tok = o.shape[:2]
    zf = z.astype(F32).reshape(b_, n_tok, DN_HEADS, DN_DV)
    y = rms_norm(o, norm_g.astype(F32)) * jax.nn.silu(zf)
    return y.reshape(b_, n_tok, DN_HEADS * DN_DV).astype(z.dtype)


def axial_rope_tables(rows, dtype):
    row = jnp.repeat(jnp.arange(rows), GRID_W).astype(F32)
    col = jnp.broadcast_to(jnp.arange(GRID_W), (rows, GRID_W)).reshape(-1).astype(F32)
    n_freq = MLA_ROPE // 4
    inv_freq = ROPE_BASE ** (-jnp.arange(n_freq, dtype=F32) / n_freq)
    ang_r = row[:, None] * inv_freq
    ang_c = col[:, None] * inv_freq
    ang = jnp.concatenate([ang_r, ang_r, ang_c, ang_c], axis=-1)[:, None, :]
    return jnp.cos(ang).astype(dtype), jnp.sin(ang).astype(dtype)


def apply_rope_tail(t, rope):
    cos, sin = rope
    nope, pe = jnp.split(t, [MLA_NOPE], axis=-1)
    x1, x2, x3, x4 = jnp.split(pe, 4, axis=-1)
    rot = jnp.concatenate([-x2, x1, -x4, x3], axis=-1)
    return jnp.concatenate([nope, pe * cos + rot * sin], axis=-1)


def mla_queries(c_q, q_a_norm_g, w_uq, q_norm_g, rope):
    b_, n_tok, _ = c_q.shape
    q = (rms_norm(c_q, q_a_norm_g) @ w_uq).reshape(b_, n_tok, MLA_HEADS, MLA_QK_DIM)
    q = rms_norm(q, q_norm_g)
    if rope is not None:
        q = apply_rope_tail(q, rope)
    return jnp.transpose(q, (0, 2, 1, 3))


def mla_keys_values(c_kv, k_rope, kv_a_norm_g, w_ukv, k_norm_g, rope):
    b_, n_tok, _ = c_kv.shape
    kv = (rms_norm(c_kv, kv_a_norm_g) @ w_ukv).reshape(b_, n_tok, MLA_HEADS, MLA_NOPE + MLA_V)
    k_nope, v = jnp.split(kv, [MLA_NOPE], axis=-1)
    k_pe = jnp.broadcast_to(k_rope[:, :, None, :], (b_, n_tok, MLA_HEADS, MLA_ROPE))
    k = rms_norm(jnp.concatenate([k_nope, k_pe], axis=-1), k_norm_g)
    if rope is not None:
        k = apply_rope_tail(k, rope)
    return jnp.transpose(k, (0, 2, 1, 3)), jnp.transpose(v, (0, 2, 1, 3))


def softmax_attention(q, k, v):
    b_, n_h, n_tok, dh = q.shape
    n_blk = n_tok // Q_BLOCK
    q_blocks = jnp.moveaxis(q.reshape(b_, n_h, n_blk, Q_BLOCK, dh), 2, 0)
    scale = dh ** -0.5

    def attend_block(qb):
        s = jnp.einsum('bhqd,bhkd->bhqk', qb, k, preferred_element_type=F32) * scale
        p = jax.nn.softmax(s, axis=-1).astype(v.dtype)
        return jnp.einsum('bhqk,bhkd->bhqd', p, v)

    o = lax.map(attend_block, q_blocks)
    o = jnp.moveaxis(o, 0, 2).reshape(b_, n_h, n_tok, -1)
    return jnp.transpose(o, (0, 2, 1, 3)).reshape(b_, n_tok, -1)


def merge_branches(y_a, y_b, gate_logits, w_out_a, w_out_b, w_o):
    g_a, g_b = jnp.split(jax.nn.sigmoid(gate_logits), N_BRANCHES, axis=-1)
    return (g_a * (y_a @ w_out_a) + g_b * (y_b @ w_out_b)) @ w_o


def expert_choice_moe(h, router_w, w_gate, w_up, w_down):
    b_, n_tok, d = h.shape
    cap = EC_CAPACITY * n_tok // N_EXPERTS
    affinity = jax.nn.softmax((h @ router_w).astype(F32), axis=-1)
    gate, idx = lax.top_k(jnp.swapaxes(affinity, 1, 2), cap)
    xe = jax.vmap(lambda hb, ib: hb[ib])(h, idx)
    hid = jax.nn.silu(jnp.einsum('becd,edf->becf', xe, w_gate)) * jnp.einsum('becd,edf->becf', xe, w_up)
    ye = jnp.einsum('becf,efd->becd', hid, w_down) * gate[..., None].astype(h.dtype)
    return jax.vmap(lambda ib, yb: jnp.zeros((n_tok, d), yb.dtype).at[ib.reshape(-1)].add(yb.reshape(-1, d)))(idx, ye)


def setup_inputs(seed: int = 0) -> dict:
    key = jax.random.key(seed)
    ks = jax.random.split(key, 26)
    d = D_MODEL
    nrm = lambda k, shape, scale: jax.random.normal(k, shape, F32) * scale
    gain = lambda k, n: 1.0 + 0.02 * jax.random.normal(k, (DEPTH, n), F32)
    dt = jnp.exp(jax.random.uniform(ks[10], (DEPTH, 2, DN_HEADS), F32, math.log(1e-3), math.log(1e-1)))
    return {
        'x': nrm(ks[0], (BATCH, SEQ, d), 1.0),
        'c': nrm(ks[1], (BATCH, d), 1.0),
        'ctx': nrm(ks[2], (BATCH, CTX_LEN, d), 1.0),
        'c_ctx': nrm(ks[3], (d,), 1.0),
        'ada_w': nrm(ks[4], (DEPTH, d, 6 * d), d ** -0.5),
        'ada_b': nrm(ks[5], (DEPTH, 6 * d), 0.02),
        'norm1_g': gain(ks[6], d),
        'norm2_g': gain(ks[7], d),
        'w_in': nrm(ks[8], (DEPTH, d, D_IN), d ** -0.5),
        'conv_w': nrm(ks[9], (DEPTH, DN_CONV, DN_QKV_DIM), DN_CONV ** -0.5),
        'a_log': jnp.log(jax.random.uniform(ks[11], (DEPTH, 2, DN_HEADS), F32, 1.0, 16.0)),
        'dt_bias': dt + jnp.log(-jnp.expm1(-dt)),
        'dn_norm_g': gain(ks[12], DN_DV),
        'w_out_a': nrm(ks[13], (DEPTH, DN_HEADS * DN_DV, d), (DN_HEADS * DN_DV) ** -0.5),
        'q_a_norm_g': gain(ks[14], MLA_Q_RANK),
        'w_uq': nrm(ks[15], (DEPTH, MLA_Q_RANK, MLA_HEADS * MLA_QK_DIM), MLA_Q_RANK ** -0.5),
        'kv_a_norm_g': gain(ks[16], MLA_KV_RANK),
        'w_ukv': nrm(ks[17], (DEPTH, MLA_KV_RANK, MLA_HEADS * (MLA_NOPE + MLA_V)), MLA_KV_RANK ** -0.5),
        'q_norm_g': gain(ks[18], MLA_QK_DIM),
        'k_norm_g': gain(ks[19], MLA_QK_DIM),
        'w_out_b': nrm(ks[20], (DEPTH, MLA_HEADS * MLA_V, d), (MLA_HEADS * MLA_V) ** -0.5),
        'w_o': nrm(ks[21], (DEPTH, d, d), d ** -0.5),
        'router_w': nrm(ks[22], (DEPTH, d, N_EXPERTS), d ** -0.5),
        'w_gate': nrm(ks[23], (DEPTH, N_EXPERTS, d, EXPERT_FF), d ** -0.5),
        'w_up': nrm(ks[24], (DEPTH, N_EXPERTS, d, EXPERT_FF), d ** -0.5),
        'w_down': nrm(ks[25], (DEPTH, N_EXPERTS, EXPERT_FF, d), EXPERT_FF ** -0.5),
    }


def reference(x, c, ctx, c_ctx, ada_w, ada_b, norm1_g, norm2_g, w_in, conv_w, a_log, dt_bias, dn_norm_g,
              w_out_a, q_a_norm_g, w_uq, kv_a_norm_g, w_ukv, q_norm_g, k_norm_g, w_out_b, w_o, router_w,
              w_gate, w_up, w_down):
    b_, n_lat, _ = x.shape
    rows = n_lat // GRID_W
    rope = axial_rope_tables(rows, x.dtype)
    zero_state = jnp.zeros((b_, DN_HEADS, DN_DK, DN_DV), F32)
    for i in range(DEPTH):
        last = i == DEPTH - 1
        mod_x = adaln(c, ada_w[i], ada_b[i])
        mod_c = adaln(c_ctx[None], ada_w[i], ada_b[i])

        h_c = modulate(ctx, norm1_g[i], mod_c[0], mod_c[1])
        h_x = modulate(x, norm1_g[i], mod_x[0], mod_x[1])
        qkv_c, z_c, al_c, be_c, cq_c, ckv_c, kr_c, gt_c = jnp.split(h_c @ w_in[i], IN_OFFSETS, axis=-1)
        qkv_x, z_x, al_x, be_x, cq_x, ckv_x, kr_x, gt_x = jnp.split(h_x @ w_in[i], IN_OFFSETS, axis=-1)

        o_dn_c, s_fwd, s_bwd = deltanet_bidir(*deltanet_inputs(qkv_c, al_c, be_c, conv_w[i], a_log[i], dt_bias[i]),
                                              zero_state, zero_state)
        o_dn_x, _, _ = deltanet_bidir(*deltanet_inputs(qkv_x, al_x, be_x, conv_w[i], a_log[i], dt_bias[i]),
                                      s_fwd, s_bwd)

        k_c, v_c = mla_keys_values(ckv_c, kr_c, kv_a_norm_g[i], w_ukv[i], k_norm_g[i], None)
        k_x, v_x = mla_keys_values(ckv_x, kr_x, kv_a_norm_g[i], w_ukv[i], k_norm_g[i], rope)
        q_x = mla_queries(cq_x, q_a_norm_g[i], w_uq[i], q_norm_g[i], rope)
        o_mla_x = softmax_attention(q_x, jnp.concatenate([k_c, k_x], axis=2), jnp.concatenate([v_c, v_x], axis=2))

        mix_x = merge_branches(deltanet_out(o_dn_x, z_x, dn_norm_g[i]), o_mla_x, gt_x, w_out_a[i], w_out_b[i], w_o[i])
        x_mid = x + mod_x[2] * mix_x

        if not last:
            q_c = mla_queries(cq_c, q_a_norm_g[i], w_uq[i], q_norm_g[i], None)
            o_mla_c = softmax_attention(q_c, k_c, v_c)
            mix_c = merge_branches(deltanet_out(o_dn_c, z_c, dn_norm_g[i]), o_mla_c, gt_c,
                                   w_out_a[i], w_out_b[i], w_o[i])
            ctx = ctx + mod_c[2] * mix_c
            ctx = ctx + mod_c[5] * expert_choice_moe(modulate(ctx, norm2_g[i], mod_c[3], mod_c[4]),
                                                     router_w[i], w_gate[i], w_up[i], w_down[i])

        h2 = modulate(x_mid, norm2_g[i], mod_x[3], mod_x[4])
        x = x_mid + mod_x[5] * expert_choice_moe(h2, router_w[i], w_gate[i], w_up[i], w_down[i])
    return x
```

```python
import functools

import jax
import jax.numpy as jnp
from jax import lax
from jax.experimental import pallas as pl
from jax.experimental.pallas import tpu as pltpu

F32 = jnp.float32
BF16 = jnp.bfloat16
EPS = 1e-6

N_HEADS = 8
HEAD_DIM = 128
DN_CONV = 5
DN_CHUNK = 64
MLA_Q_RANK = 384
MLA_KV_RANK = 256
MLA_ROPE = 64
MLA_QK_DIM = HEAD_DIM + MLA_ROPE
GRID_W = 64
ROPE_BASE = 10000.0
N_EXPERTS = 16
EC_CAPACITY = 2
N_BRANCHES = 2

LANES = 128
SUBLANES = 8
VMEM_LIMIT_BYTES = 56 * 1024 * 1024

HI = lax.Precision.HIGHEST


def _cparams(sem):
    return pltpu.CompilerParams(dimension_semantics=sem, vmem_limit_bytes=VMEM_LIMIT_BYTES)


def _dot(a, b, precision=None):
    return jnp.dot(a, b, preferred_element_type=F32, precision=precision)


def _dot_nt(a, b, precision=None):
    return lax.dot_general(a, b, (((1,), (1,)), ((), ())), preferred_element_type=F32, precision=precision)


def _dot_tn(a, b):
    return lax.dot_general(a, b, (((0,), (0,)), ((), ())), preferred_element_type=F32)


def _sigmoid(x):
    return 1.0 / (1.0 + jnp.exp(-x))


def _silu(x):
    return x * _sigmoid(x)


def _const_spec(shape):
    nd = len(shape)
    return pl.BlockSpec(shape, lambda *_: (0,) * nd, pipeline_mode=pl.Buffered(1))


def _mod_kernel(c_ref, w_ref, b_ref, o_ref):
    c = c_ref[...]
    o_ref[...] = _dot(_silu(c), w_ref[...], precision=HI) + b_ref[...]


def _modulation(cond, ada_w, ada_b):
    n, d = cond.shape
    n_out = ada_w.shape[1]
    tn = d
    return pl.pallas_call(
        _mod_kernel,
        grid=(n_out // tn,),
        in_specs=[pl.BlockSpec((n, d), lambda j: (0, 0)),
                  pl.BlockSpec((d, tn), lambda j: (0, j)),
                  pl.BlockSpec((1, tn), lambda j: (0, j))],
        out_specs=pl.BlockSpec((n, tn), lambda j: (0, j)),
        out_shape=jax.ShapeDtypeStruct((n, n_out), F32),
        compiler_params=_cparams(("parallel",)),
        name="modulation",
    )(cond, ada_w, ada_b.reshape(1, n_out))


def _modulated_norm(x, g, shift, scale):
    ms = jnp.mean(x * x, axis=-1, keepdims=True)
    return (x * lax.rsqrt(ms + EPS)) * g * (1.0 + scale) + shift


def _inproj_kernel(widths, x_ref, shift_ref, scale_ref, g_ref, wb_ref, ws_ref, *out_refs):
    big_refs, small_ref = out_refs[:-1], out_refs[-1]
    h = _modulated_norm(x_ref[0], g_ref[...], shift_ref[0], scale_ref[0]).astype(BF16)
    off = 0
    for o_ref, width in zip(big_refs, widths):
        for c0 in range(0, width, 1024):
            c1 = min(c0 + 1024, width)
            o_ref[0, :, c0:c1] = _dot(h, wb_ref[:, off + c0:off + c1]).astype(o_ref.dtype)
        off += width
    small_ref[0] = _dot(h, ws_ref[...])


def _in_projection(x, shift, scale, g, w_big, w_small, widths, tl):
    b, n, d = x.shape
    row = lambda w: pl.BlockSpec((1, tl, w), lambda i, j: (i, j, 0))
    vec = pl.BlockSpec((1, 1, d), lambda i, j: (i, 0, 0))
    outs = [jax.ShapeDtypeStruct((b, n, w), BF16) for w in widths]
    outs.append(jax.ShapeDtypeStruct((b, n, w_small.shape[1]), F32))
    return pl.pallas_call(
        functools.partial(_inproj_kernel, widths),
        grid=(b, n // tl),
        in_specs=[row(d), vec, vec, _const_spec((1, d)), _const_spec(w_big.shape), _const_spec(w_small.shape)],
        out_specs=[row(w) for w in widths] + [row(w_small.shape[1])],
        out_shape=outs,
        compiler_params=_cparams(("parallel", "parallel")),
        name="in_projection",
    )(x, shift, scale, g, w_big, w_small)


def _gates_kernel(ab_ref, alog_ref, dtb_ref, o_ref):
    x = ab_ref[...]
    lane = lax.broadcasted_iota(jnp.int32, x.shape, 1)
    is_alpha = (lane % (4 * N_HEADS)) < 2 * N_HEADS
    xs = x + dtb_ref[...]
    softplus = jnp.maximum(xs, 0.0) + jnp.log1p(jnp.exp(-jnp.abs(xs)))
    g = -jnp.exp(alog_ref[...]) * softplus
    o_ref[...] = jnp.where(is_alpha, g, _sigmoid(x))


def _gates(ab, a_log, dt_bias):
    b, n, ch = ab.shape
    rows = b * n * ch // LANES
    rep = LANES // ch
    pad = jnp.zeros((2 * N_HEADS,), F32)
    alog = jnp.tile(jnp.concatenate([a_log.reshape(-1), pad]), rep).reshape(1, LANES)
    dtb = jnp.tile(jnp.concatenate([dt_bias.reshape(-1), pad]), rep).reshape(1, LANES)
    tr = min(rows, 2048)
    out = pl.pallas_call(
        _gates_kernel,
        grid=(rows // tr,),
        in_specs=[pl.BlockSpec((tr, LANES), lambda i: (i, 0)), _const_spec((1, LANES)), _const_spec((1, LANES))],
        out_specs=pl.BlockSpec((tr, LANES), lambda i: (i, 0)),
        out_shape=jax.ShapeDtypeStruct((rows, LANES), F32),
        compiler_params=_cparams(("parallel",)),
        name="dn_gates",
    )(ab.reshape(rows, LANES), alog, dtb)
    return out.reshape(b, n, ch)


def _gate_rows(gb, n_chunk):
    b = gb.shape[0]
    t = gb.reshape(b, n_chunk, DN_CHUNK, 2, 2, N_HEADS)
    t = jnp.transpose(t, (0, 5, 1, 3, 4, 2))
    return t.reshape(b, N_HEADS, n_chunk, 2, 2 * DN_CHUNK)


CONV_ROWS = 512
CONV_PAD = 8


def _conv_silu(raw_ref, w_ref, pad_ref, out_ref, n_tok, normalize):
    half = DN_CONV // 2
    zeros = jnp.zeros((CONV_PAD, LANES), F32)
    pad_ref[0:CONV_PAD, :] = zeros
    pad_ref[CONV_PAD + n_tok:2 * CONV_PAD + n_tok, :] = zeros
    pad_ref[CONV_PAD:CONV_PAD + n_tok, :] = raw_ref[0].astype(F32)
    w = w_ref[...]
    rb = min(CONV_ROWS, n_tok)
    for r0 in range(0, n_tok, rb):
        acc = None
        for j in range(DN_CONV):
            s = CONV_PAD + r0 + j - half
            term = pad_ref[s:s + rb, :] * w[j:j + 1, :]
            acc = term if acc is None else acc + term
        y = _silu(acc)
        if normalize:
            y = y * lax.rsqrt(jnp.sum(y * y, axis=-1, keepdims=True) + EPS)
        out_ref[r0:r0 + rb, :] = y


def _dn_masks():
    n = 2 * DN_CHUNK
    ri = lax.broadcasted_iota(jnp.int32, (n, n), 0)
    ci = lax.broadcasted_iota(jnp.int32, (n, n), 1)
    top = ri < DN_CHUNK
    same = (ri // DN_CHUNK) == (ci // DN_CHUNK)
    sgn = jnp.where(top, 1, -1)
    delta = (ri - ci) * sgn
    incl = same & (delta >= 0)
    strict = same & (delta > 0)
    incl_t = same & (delta <= 0)
    eye = ri == ci
    return top, incl, strict, incl_t, eye


DN_INV_BLOCK = 8


def _hi_lo(a_f32):
    hi = a_f32.astype(BF16)
    return hi, (a_f32 - hi.astype(F32)).astype(BF16)


def _split_dot(a_f32, b_bf16):
    hi, lo = _hi_lo(a_f32)
    return _dot(hi, b_bf16) + _dot(lo, b_bf16)


def _dot3(a_f32, b_f32, nt=False):
    dot = _dot_nt if nt else _dot
    a_hi, a_lo = _hi_lo(a_f32)
    b_hi, b_lo = _hi_lo(b_f32)
    return dot(a_hi, b_hi) + (dot(a_hi, b_lo) + dot(a_lo, b_hi))


def _dn_prep_chunk(c, q_ref, k_ref, v_ref, gb_ref, u_ref, w_ref, qd_ref, kd_ref, qk_ref, ar_ref):
    n = 2 * DN_CHUNK
    top, incl, strict, incl_t, eye = _dn_masks()
    r0 = pl.multiple_of(c * DN_CHUNK, DN_CHUNK)
    q = q_ref[pl.ds(r0, DN_CHUNK), :]
    k = k_ref[pl.ds(r0, DN_CHUNK), :]
    v = v_ref[pl.ds(r0, DN_CHUNK), :]
    gb = gb_ref[0, 0, c]
    g_row = jnp.broadcast_to(gb[0:1, :], (SUBLANES, n))
    beta_rows = jnp.broadcast_to(gb[1:2, :], (n, n))

    gc_row = _split_dot(g_row, jnp.where(incl_t, 1.0, 0.0).astype(BF16))[0:1, :]
    c2 = jnp.broadcast_to(gc_row, (n, n))
    c1 = c2.T
    beta_c = beta_rows.T
    tot = jnp.where(top, c1[DN_CHUNK - 1:DN_CHUNK, :], c1[DN_CHUNK:DN_CHUNK + 1, :])

    decay = jnp.exp(jnp.where(incl, c1 - c2, -jnp.inf))
    e_gc = jnp.exp(c1)
    e_rest = jnp.exp(tot - c1)

    kst = jnp.concatenate([k, k], axis=0)
    qst = jnp.concatenate([q, q], axis=0)
    vst = jnp.concatenate([v, v], axis=0)
    kk = _dot3(kst, kst, nt=True)
    qk = _dot_nt(qst.astype(BF16), kst.astype(BF16))
    scale = HEAD_DIM ** -0.5

    m = jnp.where(strict, kk * beta_c * decay, 0.0)
    ri = lax.broadcasted_iota(jnp.int32, (n, n), 0)
    ci = lax.broadcasted_iota(jnp.int32, (n, n), 1)
    blk = DN_INV_BLOCK
    dg = jnp.where((ri // blk) == (ci // blk), m, 0.0)
    t = jnp.where(eye, 1.0, 0.0) - dg
    dp = _dot3(dg, dg)
    for it in range(blk.bit_length() - 2):
        t = t + _dot3(t, dp)
        if it + 1 < blk.bit_length() - 2:
            dp = _dot3(dp, dp)
    while blk < DN_CHUNK:
        off = jnp.where(((ri // (2 * blk)) == (ci // (2 * blk))) & ((ri // blk) != (ci // blk)), m, 0.0)
        t = t - _dot3(_dot3(t, off), t)
        blk *= 2

    kb = kst * beta_c
    rhs = jnp.concatenate([vst * beta_c, kb * e_gc], axis=1)
    uw = _dot3(t, rhs)
    u_ref[c] = uw[:, 0:HEAD_DIM]
    w_ref[c] = uw[:, HEAD_DIM:2 * HEAD_DIM].astype(BF16)
    qd_ref[c] = (qst * e_gc * scale).astype(BF16)
    kd_ref[c] = (kst * e_rest).astype(BF16)
    qk_ref[c] = (jnp.where(incl, qk * decay, 0.0) * scale).astype(BF16)
    ar_ref[c] = jnp.exp(tot)[DN_CHUNK - 4:DN_CHUNK + 4, :]


def _dn_scan_step(cf, cb, u_ref, w_ref, qd_ref, kd_ref, qk_ref, ar_ref, s_ref, of_ref, ob_ref):
    h = DN_CHUNK
    d = HEAD_DIM

    def pick(ref):
        return jnp.concatenate([ref[cf, 0:h, :], ref[cb, h:2 * h, :]], axis=0)

    s = s_ref[...]
    lhs = jnp.concatenate([pick(w_ref), pick(qd_ref)], axis=0)
    r = _dot(lhs, s.astype(BF16))
    ws = jnp.concatenate([r[0:h, 0:d], r[h:2 * h, d:2 * d]], axis=0)
    qs = jnp.concatenate([r[2 * h:3 * h, 0:d], r[3 * h:4 * h, d:2 * d]], axis=0)
    vn = (pick(u_ref) - ws).astype(BF16)
    o = qs + _dot(pick(qk_ref), vn)
    if of_ref is not None:
        of_ref[pl.ds(pl.multiple_of(cf * h, h), h), :] = o[0:h]
        ob_ref[pl.ds(pl.multiple_of(cb * h, h), h), :] = o[h:2 * h]
    ri = lax.broadcasted_iota(jnp.int32, (2 * h, 2 * d), 0)
    ci = lax.broadcasted_iota(jnp.int32, (2 * h, 2 * d), 1)
    vn_bd = jnp.where((ri // h) == (ci // d), jnp.concatenate([vn, vn], axis=1), jnp.zeros((), BF16))
    upd = _dot_tn(pick(kd_ref), vn_bd)
    a = jnp.concatenate([jnp.broadcast_to(ar_ref[cf, 0:1, :], (d, d)),
                         jnp.broadcast_to(ar_ref[cb, 4:5, :], (d, d))], axis=1)
    s_ref[...] = s * a + upd


def _dn_kernel(n_lat, n_ctx,
               qx_ref, kx_ref, vx_ref, qc_ref, kc_ref, vc_ref, wq_ref, wk_ref, wv_ref,
               gbx_ref, gbc_ref, z_ref, ng_ref, y_ref,
               pad_ref, q_s, k_s, v_s, qc_s, kc_s, vc_s,
               u_x, w_x, qd_x, kd_x, qk_x, ar_x, u_c, w_c, qd_c, kd_c, qk_c, ar_c,
               s_ref, of_ref, ob_ref):
    nc_x = n_lat // DN_CHUNK
    nc_c = n_ctx // DN_CHUNK

    _conv_silu(qc_ref, wq_ref, pad_ref, qc_s, n_ctx, True)
    _conv_silu(kc_ref, wk_ref, pad_ref, kc_s, n_ctx, True)
    _conv_silu(vc_ref, wv_ref, pad_ref, vc_s, n_ctx, False)
    _conv_silu(qx_ref, wq_ref, pad_ref, q_s, n_lat, True)
    _conv_silu(kx_ref, wk_ref, pad_ref, k_s, n_lat, True)
    _conv_silu(vx_ref, wv_ref, pad_ref, v_s, n_lat, False)

    def prep_c(c, carry):
        _dn_prep_chunk(c, qc_s, kc_s, vc_s, gbc_ref, u_c, w_c, qd_c, kd_c, qk_c, ar_c)
        return carry

    def prep_x(c, carry):
        _dn_prep_chunk(c, q_s, k_s, v_s, gbx_ref, u_x, w_x, qd_x, kd_x, qk_x, ar_x)
        return carry

    lax.fori_loop(0, nc_c, prep_c, 0)
    lax.fori_loop(0, nc_x, prep_x, 0)

    s_ref[...] = jnp.zeros(s_ref.shape, F32)

    def scan_c(i, carry):
        _dn_scan_step(i, nc_c - 1 - i, u_c, w_c, qd_c, kd_c, qk_c, ar_c, s_ref, None, None)
        return carry

    def scan_x(i, carry):
        _dn_scan_step(i, nc_x - 1 - i, u_x, w_x, qd_x, kd_x, qk_x, ar_x, s_ref, of_ref, ob_ref)
        return carry

    lax.fori_loop(0, nc_c, scan_c, 0)
    lax.fori_loop(0, nc_x, scan_x, 0)

    rb = min(CONV_ROWS, n_lat)
    for r0 in range(0, n_lat, rb):
        o = of_ref[r0:r0 + rb, :] + ob_ref[r0:r0 + rb, :]
        ms = jnp.mean(o * o, axis=-1, keepdims=True)
        y = (o * lax.rsqrt(ms + EPS)) * ng_ref[...] * _silu(z_ref[0, r0:r0 + rb, :].astype(F32))
        y_ref[0, r0:r0 + rb, :] = y.astype(y_ref.dtype)


def _deltanet(qkv_x, qkv_c, conv_w, gbx, gbc, z, norm_g):
    b, n_lat, _ = qkv_x.shape
    n_ctx = qkv_c.shape[1]
    hh = N_HEADS
    d = HEAD_DIM
    nc_x, nc_c = n_lat // DN_CHUNK, n_ctx // DN_CHUNK
    tok = lambda n, off: pl.BlockSpec((1, n, d), lambda i, j: (i, 0, off + j))
    cw = lambda off: pl.BlockSpec((DN_CONV, d), lambda i, j: (0, off + j))
    gspec = lambda nc: pl.BlockSpec((1, 1, nc, 2, 2 * DN_CHUNK), lambda i, j: (i, j, 0, 0, 0))
    chunk_scratch = lambda nc: [
        pltpu.VMEM((nc, 2 * DN_CHUNK, d), F32),
        pltpu.VMEM((nc, 2 * DN_CHUNK, d), BF16),
        pltpu.VMEM((nc, 2 * DN_CHUNK, d), BF16),
        pltpu.VMEM((nc, 2 * DN_CHUNK, d), BF16),
        pltpu.VMEM((nc, 2 * DN_CHUNK, 2 * DN_CHUNK), BF16),
        pltpu.VMEM((nc, SUBLANES, 2 * DN_CHUNK), F32),
    ]
    return pl.pallas_call(
        functools.partial(_dn_kernel, n_lat, n_ctx),
        grid=(b, hh),
        in_specs=[tok(n_lat, 0), tok(n_lat, hh), tok(n_lat, 2 * hh),
                  tok(n_ctx, 0), tok(n_ctx, hh), tok(n_ctx, 2 * hh),
                  cw(0), cw(hh), cw(2 * hh),
                  gspec(nc_x), gspec(nc_c),
                  tok(n_lat, 0), _const_spec((1, d))],
        out_specs=tok(n_lat, 0),
        out_shape=jax.ShapeDtypeStruct((b, n_lat, hh * d), BF16),
        scratch_shapes=[pltpu.VMEM((n_lat + 2 * CONV_PAD, d), F32)]
        + [pltpu.VMEM((n_lat, d), F32)] * 3 + [pltpu.VMEM((n_ctx, d), F32)] * 3
        + chunk_scratch(nc_x) + chunk_scratch(nc_c)
        + [pltpu.VMEM((d, 2 * d), F32), pltpu.VMEM((n_lat, d), F32), pltpu.VMEM((n_lat, d), F32)],
        compiler_params=_cparams(("parallel", "parallel")),
        name="deltanet",
    )(qkv_x, qkv_x, qkv_x, qkv_c, qkv_c, qkv_c, conv_w, conv_w, conv_w, gbx, gbc, z, norm_g)


def _rope(x, cos, sin):
    lane = lax.broadcasted_iota(jnp.int32, x.shape, 1)
    quarter = MLA_ROPE // 4
    rot = jnp.where((lane // quarter) % 2 == 0,
                    -pltpu.roll(x, LANES - quarter, axis=1), pltpu.roll(x, quarter, axis=1))
    return x * cos + rot * sin


def _rms(x, n):
    return lax.rsqrt(jnp.sum(x * x, axis=-1, keepdims=True) / n + EPS)


def _mla_proj_kernel(with_q, cq_ref, small_ref, gqa_ref, gkva_ref, wq_ref, wkv_ref, gq_ref, gk_ref,
                     cos_ref, sin_ref, *out_refs):
    hd = HEAD_DIM
    hh = N_HEADS
    cos, sin = cos_ref[...], sin_ref[...]
    small = small_ref[0]
    ckv = small[:, 0:MLA_KV_RANK]
    kpe = small[:, MLA_KV_RANK:MLA_KV_RANK + LANES]
    lane = lax.broadcasted_iota(jnp.int32, kpe.shape, 1)
    kpe = jnp.where(lane < MLA_ROPE, kpe, 0.0)
    kpe_ssq = jnp.sum(kpe * kpe, axis=-1, keepdims=True)
    ckvn = (ckv * _rms(ckv, MLA_KV_RANK) * gkva_ref[...]).astype(BF16)
    kvf = _dot(ckvn, wkv_ref[...])
    if with_q:
        q_ref, k_ref, v_ref = out_refs
        cq = cq_ref[0].astype(F32)
        cqn = (cq * _rms(cq, MLA_Q_RANK) * gqa_ref[...]).astype(BF16)
        qf = _dot(cqn, wq_ref[...])
        q_scale = MLA_QK_DIM ** -0.5
    else:
        k_ref, v_ref = out_refs
    gq, gk = gq_ref[...], gk_ref[...]
    for h in range(hh):
        if with_q:
            qn = qf[:, h * hd:(h + 1) * hd]
            qp = qf[:, (hh + h) * hd:(hh + h + 1) * hd]
            ssq = jnp.sum(qn * qn, axis=-1, keepdims=True) + jnp.sum(qp * qp, axis=-1, keepdims=True)
            r = lax.rsqrt(ssq / MLA_QK_DIM + EPS)
            q_ref[0, h, :, 0:hd] = (qn * r * gq[:, 0:hd] * q_scale).astype(q_ref.dtype)
            q_ref[0, h, :, hd:2 * hd] = (_rope(qp * r * gq[:, hd:2 * hd], cos, sin) * q_scale).astype(q_ref.dtype)
        kn = kvf[:, h * hd:(h + 1) * hd]
        r = lax.rsqrt((jnp.sum(kn * kn, axis=-1, keepdims=True) + kpe_ssq) / MLA_QK_DIM + EPS)
        k_ref[0, h, :, 0:hd] = (kn * r * gk[:, 0:hd]).astype(k_ref.dtype)
        k_ref[0, h, :, hd:2 * hd] = _rope(kpe * r * gk[:, hd:2 * hd], cos, sin).astype(k_ref.dtype)
        v_ref[0, h] = kvf[:, (hh + h) * hd:(hh + h + 1) * hd].astype(v_ref.dtype)


def _mla_projections(cq, small, gqa, gkva, wq, wkv, gq, gk, cos, sin, with_q, tl):
    b, n, _ = small.shape
    hh, hd = N_HEADS, HEAD_DIM
    row = lambda w: pl.BlockSpec((1, tl, w), lambda i, j: (i, j, 0))
    head = lambda w: pl.BlockSpec((1, hh, tl, w), lambda i, j: (i, 0, j, 0))
    tab = pl.BlockSpec((tl, LANES), lambda i, j: (j, 0))
    out_specs = [head(2 * hd), head(hd)]
    out_shape = [jax.ShapeDtypeStruct((b, hh, n, 2 * hd), BF16), jax.ShapeDtypeStruct((b, hh, n, hd), BF16)]
    if with_q:
        out_specs = [head(2 * hd)] + out_specs
        out_shape = [jax.ShapeDtypeStruct((b, hh, n, 2 * hd), BF16)] + out_shape
    return pl.pallas_call(
        functools.partial(_mla_proj_kernel, with_q),
        grid=(b, n // tl),
        in_specs=[row(cq.shape[-1]), row(small.shape[-1]), _const_spec(gqa.shape), _const_spec(gkva.shape),
                  _const_spec(wq.shape), _const_spec(wkv.shape), _const_spec(gq.shape), _const_spec(gk.shape),
                  tab, tab],
        out_specs=out_specs,
        out_shape=out_shape,
        compiler_params=_cparams(("parallel", "parallel")),
        name="mla_projections_q" if with_q else "mla_projections_kv",
    )(cq, small, gqa, gkva, wq, wkv, gq, gk, cos, sin)


def _attn_kernel(q_ref, kc_ref, vc_ref, kx_ref, vx_ref, o_ref):
    q = q_ref[0, 0]
    s_c = _dot_nt(q, kc_ref[0, 0])
    s_x = _dot_nt(q, kx_ref[0, 0])
    m = jnp.maximum(jnp.max(s_c, axis=-1, keepdims=True), jnp.max(s_x, axis=-1, keepdims=True))
    p_c = jnp.exp(s_c - m)
    p_x = jnp.exp(s_x - m)
    l = jnp.sum(p_c, axis=-1, keepdims=True) + jnp.sum(p_x, axis=-1, keepdims=True)
    o = _dot(p_c.astype(BF16), vc_ref[0, 0]) + _dot(p_x.astype(BF16), vx_ref[0, 0])
    o_ref[0] = (o / l).astype(o_ref.dtype)


def _attention(q, k_c, v_c, k_x, v_x, tq):
    b, hh, n, dq = q.shape
    n_ctx = k_c.shape[2]
    hd = v_x.shape[-1]
    kv = lambda n_, w: pl.BlockSpec((1, 1, n_, w), lambda i, j, t: (i, j, 0, 0))
    return pl.pallas_call(
        _attn_kernel,
        grid=(b, hh, n // tq),
        in_specs=[pl.BlockSpec((1, 1, tq, dq), lambda i, j, t: (i, j, t, 0)),
                  kv(n_ctx, dq), kv(n_ctx, hd), kv(n, dq), kv(n, hd)],
        out_specs=pl.BlockSpec((1, tq, hd), lambda i, j, t: (i, t, j)),
        out_shape=jax.ShapeDtypeStruct((b, n, hh * hd), BF16),
        compiler_params=_cparams(("parallel", "parallel", "parallel")),
        name="attention",
    )(q, k_c, v_c, k_x, v_x)


def _merge_kernel(ya_ref, yb_ref, gate_ref, x_ref, g1_ref, sh2_ref, sc2_ref, n2g_ref,
                  wa_ref, wb_ref, wo_ref, rwt_ref, xmid_ref, h2_ref, aff_ref):
    d = x_ref.shape[-1]
    ga = _sigmoid(gate_ref[0, :, 0:d].astype(F32))
    gb = _sigmoid(gate_ref[0, :, d:2 * d].astype(F32))
    mix = ga * _dot(ya_ref[0], wa_ref[...]) + gb * _dot(yb_ref[0], wb_ref[...])
    mix = _dot(mix.astype(BF16), wo_ref[...])
    xm = x_ref[0] + g1_ref[0] * mix
    xmid_ref[0] = xm
    h2 = _modulated_norm(xm, n2g_ref[...], sh2_ref[0], sc2_ref[0])
    h2_ref[0] = h2
    logits = _dot_nt(rwt_ref[...], h2, precision=HI)
    mx = jnp.max(logits, axis=0, keepdims=True)
    ex = jnp.exp(logits - mx)
    aff_ref[0] = ex / jnp.sum(ex, axis=0, keepdims=True)


def _merge(ya, yb, gate, x, g1, sh2, sc2, n2g, wa, wb, wo, rwt, tl):
    b, n, d = x.shape
    ne = rwt.shape[0]
    row = lambda w: pl.BlockSpec((1, tl, w), lambda i, j: (i, j, 0))
    vec = pl.BlockSpec((1, 1, d), lambda i, j: (i, 0, 0))
    return pl.pallas_call(
        _merge_kernel,
        grid=(b, n // tl),
        in_specs=[row(d), row(d), row(2 * d), row(d), vec, vec, vec, _const_spec((1, d)),
                  _const_spec(wa.shape), _const_spec(wb.shape), _const_spec(wo.shape), _const_spec(rwt.shape)],
        out_specs=[row(d), row(d), pl.BlockSpec((1, ne, tl), lambda i, j: (i, 0, j))],
        out_shape=[jax.ShapeDtypeStruct((b, n, d), F32), jax.ShapeDtypeStruct((b, n, d), F32),
                   jax.ShapeDtypeStruct((b, ne, n), F32)],
        compiler_params=_cparams(("parallel", "parallel")),
        name="merge_router",
    )(ya, yb, gate, x, g1, sh2, sc2, n2g, wa, wb, wo, rwt)


TOPK_RANK_ROWS = 64


def _lane_cumsum(x_bf16, tri_bf16, out_ref):
    ne, n = x_bf16.shape
    carry = jnp.zeros((ne, 1), F32)
    for j in range(n // LANES):
        blk = _dot(x_bf16[:, j * LANES:(j + 1) * LANES], tri_bf16) + carry
        out_ref[:, j * LANES:(j + 1) * LANES] = blk
        carry = blk[:, LANES - 1:LANES]


def _topk_kernel(cap, aff_ref, idx_ref, gate_ref, cum_ref, affsel_ref, cumrep_ref, affrep_ref):
    aff = aff_ref[0]
    ne, n = aff.shape
    bits = pltpu.bitcast(aff, jnp.int32)

    def search(i, t):
        cand = t | (jnp.int32(1) << (30 - i))
        cnt = jnp.sum(jnp.where(bits >= cand, 1.0, 0.0), axis=-1, keepdims=True)
        return jnp.where(cnt >= cap, cand, t)

    thr = lax.fori_loop(0, 31, search, jnp.zeros((ne, 1), jnp.int32))
    gt = bits > thr
    eq = bits == thr
    need = cap - jnp.sum(jnp.where(gt, 1.0, 0.0), axis=-1, keepdims=True)
    ri = lax.broadcasted_iota(jnp.int32, (LANES, LANES), 0)
    ci = lax.broadcasted_iota(jnp.int32, (LANES, LANES), 1)
    tri = jnp.where(ri <= ci, 1.0, 0.0).astype(BF16)
    _lane_cumsum(jnp.where(eq, 1.0, 0.0).astype(BF16), tri, cum_ref)
    sel = gt | (eq & (cum_ref[...] <= need))
    affsel_ref[...] = jnp.where(sel, aff, 0.0)
    _lane_cumsum(jnp.where(sel, 1.0, 0.0).astype(BF16), tri, cum_ref)
    for ex in range(ne):
        cumrep_ref[ex] = jnp.broadcast_to(cum_ref[ex:ex + 1, :], (SUBLANES, n))
        affrep_ref[ex] = jnp.broadcast_to(affsel_ref[ex:ex + 1, :], (SUBLANES, n))

    rows = TOPK_RANK_ROWS
    lane = lax.broadcasted_iota(jnp.int32, (rows, LANES), 1)
    idx_ref[0] = jnp.zeros((cap, LANES), jnp.int32)
    gate_ref[0] = jnp.zeros((cap, LANES), F32)

    def per_expert(e, carry):
        for r0 in range(0, cap, rows):
            rank = (lax.broadcasted_iota(jnp.int32, (rows, LANES), 0) + r0).astype(F32)
            cnt = jnp.zeros((rows, LANES), F32)
            gat = jnp.zeros((rows, LANES), F32)
            for j in range(n // LANES):
                c_blk = cumrep_ref[e, 0:1, j * LANES:(j + 1) * LANES]
                a_blk = affrep_ref[e, 0:1, j * LANES:(j + 1) * LANES]
                cnt = cnt + jnp.where(c_blk <= rank, 1.0, 0.0)
                gat = gat + jnp.where(c_blk == rank + 1.0, a_blk, 0.0)
            pos = jnp.sum(cnt, axis=-1, keepdims=True).astype(jnp.int32)
            g = jnp.sum(gat, axis=-1, keepdims=True)
            idx_ref[0, r0:r0 + rows, :] = jnp.where(lane == e, pos, idx_ref[0, r0:r0 + rows, :])
            gate_ref[0, r0:r0 + rows, :] = jnp.where(lane == e, g, gate_ref[0, r0:r0 + rows, :])
        return carry

    lax.fori_loop(0, ne, per_expert, 0)


def _expert_topk(aff_t, cap):
    b, ne, n = aff_t.shape
    return pl.pallas_call(
        functools.partial(_topk_kernel, cap),
        grid=(b,),
        in_specs=[pl.BlockSpec((1, ne, n), lambda i: (i, 0, 0))],
        out_specs=[pl.BlockSpec((1, cap, LANES), lambda i: (i, 0, 0))] * 2,
        out_shape=[jax.ShapeDtypeStruct((b, cap, LANES), jnp.int32), jax.ShapeDtypeStruct((b, cap, LANES), F32)],
        scratch_shapes=[pltpu.VMEM((ne, n), F32), pltpu.VMEM((ne, n), F32),
                        pltpu.VMEM((ne, SUBLANES, n), F32), pltpu.VMEM((ne, SUBLANES, n), F32)],
        compiler_params=_cparams(("parallel",)),
        name="expert_topk",
    )(aff_t)


MOE_FF_TILE = 512
MOE_UNROLL = 8


def _moe_kernel(cap, idx_ref, gate_ref, h_ref, wg_ref, wu_ref, wd_ref, o_ref, xe_ref, ye_ref):
    e = pl.program_id(1)
    f = pl.program_id(2)
    nf = pl.num_programs(2)

    @pl.when((e == 0) & (f == 0))
    def _():
        o_ref[...] = jnp.zeros(o_ref.shape, o_ref.dtype)

    @pl.when(f == 0)
    def _():
        def gather(i, carry):
            for u in range(MOE_UNROLL):
                r = i * MOE_UNROLL + u
                xe_ref[pl.ds(r, 1), :] = h_ref[0, pl.ds(idx_ref[0, 0, 0, r], 1), :]
            return carry
        lax.fori_loop(0, cap // MOE_UNROLL, gather, 0)

    x = xe_ref[...].astype(BF16)
    hid = (_silu(_dot(x, wg_ref[0])) * _dot(x, wu_ref[0])).astype(BF16)
    part = _dot(hid, wd_ref[0])

    @pl.when(f == 0)
    def _():
        ye_ref[...] = part

    @pl.when(f > 0)
    def _():
        ye_ref[...] += part

    @pl.when(f == nf - 1)
    def _():
        def scatter(i, carry):
            for u in range(MOE_UNROLL):
                r = i * MOE_UNROLL + u
                t = idx_ref[0, 0, 0, r]
                o_ref[0, pl.ds(t, 1), :] = o_ref[0, pl.ds(t, 1), :] + gate_ref[0, 0, 0, r] * ye_ref[pl.ds(r, 1), :]
            return carry
        lax.fori_loop(0, cap // MOE_UNROLL, scatter, 0)


def _moe(idx, gate, h2, wg, wu, wd):
    b, n, d = h2.shape
    ne, _, ffp = wg.shape
    cap = idx.shape[-1]
    tf = MOE_FF_TILE
    smem = lambda: pl.BlockSpec((1, 1, 1, cap), lambda i, j, k: (i, j, 0, 0), memory_space=pltpu.SMEM)
    return pl.pallas_call(
        functools.partial(_moe_kernel, cap),
        grid=(b, ne, ffp // tf),
        in_specs=[smem(), smem(),
                  pl.BlockSpec((1, n, d), lambda i, j, k: (i, 0, 0), pipeline_mode=pl.Buffered(1)),
                  pl.BlockSpec((1, d, tf), lambda i, j, k: (j, 0, k)),
                  pl.BlockSpec((1, d, tf), lambda i, j, k: (j, 0, k)),
                  pl.BlockSpec((1, tf, d), lambda i, j, k: (j, k, 0))],
        out_specs=pl.BlockSpec((1, n, d), lambda i, j, k: (i, 0, 0), pipeline_mode=pl.Buffered(1)),
        out_shape=jax.ShapeDtypeStruct((b, n, d), F32),
        scratch_shapes=[pltpu.VMEM((cap, d), F32), pltpu.VMEM((cap, d), F32)],
        compiler_params=_cparams(("parallel", "arbitrary", "arbitrary")),
        name="expert_ffn",
    )(idx[:, :, None, :], gate[:, :, None, :], h2, wg, wu, wd)


def _final_kernel(xm_ref, moe_ref, g_ref, o_ref):
    o_ref[0] = xm_ref[0] + g_ref[0] * moe_ref[0]


def _final(xmid, moe, g2, tl):
    b, n, d = xmid.shape
    row = pl.BlockSpec((1, tl, d), lambda i, j: (i, j, 0))
    return pl.pallas_call(
        _final_kernel,
        grid=(b, n // tl),
        in_specs=[row, row, pl.BlockSpec((1, 1, d), lambda i, j: (i, 0, 0))],
        out_specs=row,
        out_shape=jax.ShapeDtypeStruct((b, n, d), F32),
        compiler_params=_cparams(("parallel", "parallel")),
        name="final_residual",
    )(xmid, moe, g2)


def _rope_tables(n_lat):
    rows = n_lat // GRID_W
    row = jnp.repeat(jnp.arange(rows), GRID_W).astype(F32)
    col = jnp.broadcast_to(jnp.arange(GRID_W), (rows, GRID_W)).reshape(-1).astype(F32)
    n_freq = MLA_ROPE // 4
    inv_freq = ROPE_BASE ** (-jnp.arange(n_freq, dtype=F32) / n_freq)
    ang_r = row[:, None] * inv_freq
    ang_c = col[:, None] * inv_freq
    ang = jnp.concatenate([ang_r, ang_r, ang_c, ang_c], axis=-1)
    zeros = jnp.zeros((n_lat, LANES - MLA_ROPE), F32)
    return jnp.concatenate([jnp.cos(ang), zeros], axis=-1), jnp.concatenate([jnp.sin(ang), zeros], axis=-1)


def _pad_lanes(v, width):
    return jnp.concatenate([v, jnp.zeros(v.shape[:-1] + (width - v.shape[-1],), v.dtype)], axis=-1)


def kernel(x, c, ctx, c_ctx, ada_w, ada_b, norm1_g, norm2_g, w_in, conv_w, a_log, dt_bias, dn_norm_g, w_out_a, q_a_norm_g, w_uq, kv_a_norm_g, w_ukv, q_norm_g, k_norm_g, w_out_b, w_o, router_w, w_gate, w_up, w_down):
    assert ada_w.shape[0] == 1, "single-layer block"
    b, n_lat, d = x.shape
    n_ctx = ctx.shape[1]
    hh, hd = N_HEADS, HEAD_DIM
    qkv_dim = 3 * hh * hd
    nb = 2 * hh

    n_cond = -(-(b + 1) // SUBLANES) * SUBLANES
    cond = jnp.concatenate([c, c_ctx[None], jnp.zeros((n_cond - b - 1, d), F32)], axis=0)
    mod = _modulation(cond, ada_w[0], ada_b[0])
    mods = [mod[:, i * d:(i + 1) * d] for i in range(6)]
    lat = lambda m: m[:b, None, :]
    cvec = lambda m: jnp.broadcast_to(m[b][None, None, :], (b, 1, d))

    o_z = qkv_dim
    o_alpha = o_z + hh * hd
    o_beta = o_alpha + nb
    o_cq = o_beta + nb
    o_ckv = o_cq + MLA_Q_RANK
    o_kr = o_ckv + MLA_KV_RANK
    o_gate = o_kr + MLA_ROPE
    w = w_in[0]
    w_big = jnp.concatenate([w[:, :o_alpha], w[:, o_gate:], w[:, o_cq:o_ckv]], axis=1).astype(BF16)
    small_w = jnp.concatenate([w[:, o_ckv:o_gate], w[:, o_alpha:o_cq]], axis=1)
    small_w = _pad_lanes(small_w, MLA_KV_RANK + LANES).astype(BF16)
    widths = (qkv_dim, hh * hd, N_BRANCHES * d, MLA_Q_RANK)
    g1 = norm1_g[0][None]
    qkv_x, z_x, gt_x, cq_x, sm_x = _in_projection(x, lat(mods[0]), lat(mods[1]), g1, w_big, small_w, widths, 512)
    qkv_c, _, _, cq_c, sm_c = _in_projection(ctx, cvec(mods[0]), cvec(mods[1]), g1, w_big, small_w, widths,
                                            min(n_ctx, 512))

    ab0 = MLA_KV_RANK + MLA_ROPE
    gb_x = _gates(sm_x[:, :, ab0:ab0 + 2 * nb], a_log[0], dt_bias[0])
    gb_c = _gates(sm_c[:, :, ab0:ab0 + 2 * nb], a_log[0], dt_bias[0])
    y_a = _deltanet(qkv_x, qkv_c, conv_w[0], _gate_rows(gb_x, n_lat // DN_CHUNK),
                    _gate_rows(gb_c, n_ctx // DN_CHUNK), z_x, dn_norm_g[0][None])

    wq = w_uq[0].reshape(MLA_Q_RANK, hh, MLA_QK_DIM)
    wq = jnp.concatenate([wq[:, :, :hd].reshape(MLA_Q_RANK, hh * hd),
                          _pad_lanes(wq[:, :, hd:], hd).reshape(MLA_Q_RANK, hh * hd)], axis=1).astype(BF16)
    wkv = w_ukv[0].reshape(MLA_KV_RANK, hh, 2 * hd)
    wkv = jnp.concatenate([wkv[:, :, :hd].reshape(MLA_KV_RANK, hh * hd),
                           wkv[:, :, hd:].reshape(MLA_KV_RANK, hh * hd)], axis=1).astype(BF16)
    gq = _pad_lanes(q_norm_g[0], 2 * hd)[None]
    gk = _pad_lanes(k_norm_g[0], 2 * hd)[None]
    gqa, gkva = q_a_norm_g[0][None], kv_a_norm_g[0][None]
    cos, sin = _rope_tables(n_lat)
    cos_c = _pad_lanes(jnp.ones((n_ctx, MLA_ROPE), F32), LANES)
    sin_c = jnp.zeros((n_ctx, LANES), F32)
    q_x, k_x, v_x = _mla_projections(cq_x, sm_x, gqa, gkva, wq, wkv, gq, gk, cos, sin, True, 512)
    k_c, v_c = _mla_projections(cq_c, sm_c, gqa, gkva, wq, wkv, gq, gk, cos_c, sin_c, False, min(n_ctx, 512))
    y_b = _attention(q_x, k_c, v_c, k_x, v_x, 256)

    rwt = jnp.transpose(router_w[0])
    x_mid, h2, aff_t = _merge(y_a, y_b, gt_x, x, lat(mods[2]), lat(mods[3]), lat(mods[4]), norm2_g[0][None],
                              w_out_a[0].astype(BF16), w_out_b[0].astype(BF16), w_o[0].astype(BF16), rwt, 512)

    cap = EC_CAPACITY * n_lat // N_EXPERTS
    idx, gate = _expert_topk(aff_t, cap)
    idx = jnp.transpose(idx[:, :, :N_EXPERTS], (0, 2, 1))
    gate = jnp.transpose(gate[:, :, :N_EXPERTS], (0, 2, 1))
    ff = w_gate.shape[-1]
    ffp = -(-ff // MOE_FF_TILE) * MOE_FF_TILE
    wg = _pad_lanes(w_gate[0], ffp).astype(BF16)
    wu = _pad_lanes(w_up[0], ffp).astype(BF16)
    wd = jnp.concatenate([w_down[0], jnp.zeros((N_EXPERTS, ffp - ff, d), F32)], axis=1).astype(BF16)
    moe = _moe(idx, gate, h2, wg, wu, wd)
    return _final(x_mid, moe, lat(mods[5]), 512)
```

```python
import functools
import math

import jax
import jax.numpy as jnp
from jax import lax
from jax.experimental import pallas as pl
from jax.experimental.pallas import tpu as pltpu

F32 = jnp.float32
BF16 = jnp.bfloat16
EPS = 1e-6

N_HEADS = 8
HEAD_DIM = 128
DN_CONV = 5
DN_CHUNK = 64
MLA_Q_RANK = 384
MLA_KV_RANK = 256
MLA_ROPE = 64
MLA_QK_DIM = HEAD_DIM + MLA_ROPE
GRID_W = 64
ROPE_BASE = 10000.0
N_EXPERTS = 16
EC_CAPACITY = 2
N_BRANCHES = 2

LANES = 128
SUBLANES = 8
VMEM_LIMIT_BYTES = 56 * 1024 * 1024

HI = lax.Precision.HIGHEST


def _cparams(sem):
    return pltpu.CompilerParams(dimension_semantics=sem, vmem_limit_bytes=VMEM_LIMIT_BYTES)


def _dot(a, b, precision=None):
    return jnp.dot(a, b, preferred_element_type=F32, precision=precision)


def _dot_nt(a, b, precision=None):
    return lax.dot_general(a, b, (((1,), (1,)), ((), ())), preferred_element_type=F32, precision=precision)


def _dot_tn(a, b):
    return lax.dot_general(a, b, (((0,), (0,)), ((), ())), preferred_element_type=F32)


def _sigmoid(x):
    return 1.0 / (1.0 + jnp.exp(-x))


def _silu(x):
    return x * _sigmoid(x)


def _const_spec(shape):
    nd = len(shape)
    return pl.BlockSpec(shape, lambda *_: (0,) * nd, pipeline_mode=pl.Buffered(1))


def _mod_kernel(c_ref, w_ref, b_ref, o_ref):
    c = c_ref[...]
    o_ref[...] = _dot(_silu(c), w_ref[...], precision=HI) + b_ref[...]


def _modulation(cond, ada_w, ada_b):
    n, d = cond.shape
    n_out = ada_w.shape[1]
    tn = d
    return pl.pallas_call(
        _mod_kernel,
        grid=(n_out // tn,),
        in_specs=[pl.BlockSpec((n, d), lambda j: (0, 0)),
                  pl.BlockSpec((d, tn), lambda j: (0, j)),
                  pl.BlockSpec((1, tn), lambda j: (0, j))],
        out_specs=pl.BlockSpec((n, tn), lambda j: (0, j)),
        out_shape=jax.ShapeDtypeStruct((n, n_out), F32),
        compiler_params=_cparams(("parallel",)),
        name="modulation",
    )(cond, ada_w, ada_b.reshape(1, n_out))


def _modulated_norm(x, g, shift, scale):
    ms = jnp.mean(x * x, axis=-1, keepdims=True)
    return (x * lax.rsqrt(ms + EPS)) * g * (1.0 + scale) + shift


def _inproj_kernel(widths, x_ref, shift_ref, scale_ref, g_ref, wb_ref, ws_ref, wab_ref, alog_ref, dtb_ref,
                   *out_refs):
    big_refs, small_ref, gates_ref = out_refs[:-2], out_refs[-2], out_refs[-1]
    h = _modulated_norm(x_ref[0], g_ref[...], shift_ref[0], scale_ref[0]).astype(BF16)
    off = 0
    for o_ref, width in zip(big_refs, widths):
        for c0 in range(0, width, 1024):
            c1 = min(c0 + 1024, width)
            o_ref[0, :, c0:c1] = _dot(h, wb_ref[:, off + c0:off + c1]).astype(o_ref.dtype)
        off += width
    small_ref[0] = _dot(h, ws_ref[...])
    ab = _dot_nt(wab_ref[...], h)
    row_id = lax.broadcasted_iota(jnp.int32, ab.shape, 0)
    xs = ab + dtb_ref[...]
    softplus = jnp.maximum(xs, 0.0) + jnp.log1p(jnp.exp(-jnp.abs(xs)))
    gates_ref[0] = jnp.where(row_id < 2 * N_HEADS, -jnp.exp(alog_ref[...]) * softplus, _sigmoid(ab))


def _in_projection(x, shift, scale, g, w_big, w_small, w_ab_t, alog, dtb, widths, tl):
    b, n, d = x.shape
    row = lambda w: pl.BlockSpec((1, tl, w), lambda i, j: (i, j, 0))
    vec = pl.BlockSpec((1, 1, d), lambda i, j: (i, 0, 0))
    n_gate = w_ab_t.shape[0]
    outs = [jax.ShapeDtypeStruct((b, n, w), BF16) for w in widths]
    outs.append(jax.ShapeDtypeStruct((b, n, w_small.shape[1]), F32))
    outs.append(jax.ShapeDtypeStruct((b, n_gate, n), F32))
    return pl.pallas_call(
        functools.partial(_inproj_kernel, widths),
        grid=(b, n // tl),
        in_specs=[row(d), vec, vec, _const_spec((1, d)), _const_spec(w_big.shape), _const_spec(w_small.shape),
                  _const_spec(w_ab_t.shape), _const_spec(alog.shape), _const_spec(dtb.shape)],
        out_specs=[row(w) for w in widths] + [row(w_small.shape[1]),
                                              pl.BlockSpec((1, n_gate, tl), lambda i, j: (i, 0, j))],
        out_shape=outs,
        compiler_params=_cparams(("parallel", "parallel")),
        name="in_projection",
    )(x, shift, scale, g, w_big, w_small, w_ab_t, alog, dtb)


CONV_ROWS = 512
CONV_PAD = 8


def _conv_silu(raw_ref, w_ref, pad_ref, out_ref, n_tok, normalize):
    half = DN_CONV // 2
    zeros = jnp.zeros((CONV_PAD, LANES), F32)
    pad_ref[0:CONV_PAD, :] = zeros
    pad_ref[CONV_PAD + n_tok:2 * CONV_PAD + n_tok, :] = zeros
    pad_ref[CONV_PAD:CONV_PAD + n_tok, :] = raw_ref[0].astype(F32)
    w = w_ref[...]
    rb = min(CONV_ROWS, n_tok)
    for r0 in range(0, n_tok, rb):
        acc = None
        for j in range(DN_CONV):
            s = CONV_PAD + r0 + j - half
            term = pad_ref[s:s + rb, :] * w[j:j + 1, :]
            acc = term if acc is None else acc + term
        y = _silu(acc)
        if normalize:
            y = y * lax.rsqrt(jnp.sum(y * y, axis=-1, keepdims=True) + EPS)
        out_ref[r0:r0 + rb, :] = y


def _dn_masks():
    n = 2 * DN_CHUNK
    ri = lax.broadcasted_iota(jnp.int32, (n, n), 0)
    ci = lax.broadcasted_iota(jnp.int32, (n, n), 1)
    top = ri < DN_CHUNK
    same = (ri // DN_CHUNK) == (ci // DN_CHUNK)
    sgn = jnp.where(top, 1, -1)
    delta = (ri - ci) * sgn
    incl = same & (delta >= 0)
    strict = same & (delta > 0)
    incl_t = same & (delta <= 0)
    eye = ri == ci
    return top, incl, strict, incl_t, eye


DN_INV_BLOCK = 8
DN_PREP_GROUP = 8


def _hi_lo(a_f32):
    hi = a_f32.astype(BF16)
    return hi, (a_f32 - hi.astype(F32)).astype(BF16)


def _split_dot(a_f32, b_bf16):
    hi, lo = _hi_lo(a_f32)
    return _dot(hi, b_bf16) + _dot(lo, b_bf16)


def _bdot(a_f32, b_f32):
    return _dot(a_f32.astype(BF16), b_f32.astype(BF16))


def _dn_gate_rows(c0, group, gt_ref):
    n = 2 * DN_CHUNK
    t0 = pl.multiple_of(c0 * DN_CHUNK, n)
    lane = lax.broadcasted_iota(jnp.int32, (1, n), 1)
    rows = [[gt_ref[0, kind, dr, 0, :, pl.ds(t0, group * DN_CHUNK)] for dr in range(2)] for kind in range(2)]
    out = []
    for j in range(group):
        p, odd = divmod(j, 2)
        both = []
        for kind in range(2):
            fwd = rows[kind][0][:, p * n:(p + 1) * n]
            bwd = rows[kind][1][:, p * n:(p + 1) * n]
            if odd:
                both.append(jnp.where(lane < DN_CHUNK, pltpu.roll(fwd, DN_CHUNK, axis=1), bwd))
            else:
                both.append(jnp.where(lane < DN_CHUNK, fwd, pltpu.roll(bwd, DN_CHUNK, axis=1)))
        out.append(jnp.concatenate(both, axis=0))
    return out


def _dn_prep_load(c, q_ref, k_ref, v_ref):
    r0 = pl.multiple_of(c * DN_CHUNK, DN_CHUNK)
    return q_ref[pl.ds(r0, DN_CHUNK), :], k_ref[pl.ds(r0, DN_CHUNK), :], v_ref[pl.ds(r0, DN_CHUNK), :]


def _dn_prep_compute(q, k, v, gb):
    n = 2 * DN_CHUNK
    top, incl, strict, incl_t, eye = _dn_masks()
    g_row = jnp.broadcast_to(gb[0:1, :], (SUBLANES, n))
    beta_rows = jnp.broadcast_to(gb[1:2, :], (n, n))
    kst = jnp.concatenate([k, k], axis=0)
    qst = jnp.concatenate([q, q], axis=0)
    vst = jnp.concatenate([v, v], axis=0)
    kst_b = kst.astype(BF16)

    gc_row = _split_dot(g_row, jnp.where(incl_t, 1.0, 0.0).astype(BF16))[0:1, :]
    kk = _dot_nt(kst_b, kst_b)
    qk = _dot_nt(qst.astype(BF16), kst_b)
    yield
    c2 = jnp.broadcast_to(gc_row, (n, n))
    c1 = c2.T
    beta_c = beta_rows.T
    tot = jnp.where(top, c1[DN_CHUNK - 1:DN_CHUNK, :], c1[DN_CHUNK:DN_CHUNK + 1, :])
    decay = jnp.exp(jnp.where(incl, c1 - c2, -jnp.inf))
    e_gc = jnp.exp(c1)
    e_rest = jnp.exp(tot - c1)
    scale = HEAD_DIM ** -0.5

    m = jnp.where(strict, kk * beta_c * decay, 0.0)
    ri = lax.broadcasted_iota(jnp.int32, (n, n), 0)
    ci = lax.broadcasted_iota(jnp.int32, (n, n), 1)
    blk = DN_INV_BLOCK
    dg = jnp.where((ri // blk) == (ci // blk), m, 0.0)
    t = jnp.where(eye, 1.0, 0.0) - dg
    dp = _bdot(dg, dg)
    yield
    for it in range(blk.bit_length() - 2):
        t = t + _bdot(t, dp)
        if it + 1 < blk.bit_length() - 2:
            dp = _bdot(dp, dp)
        yield
    while blk < DN_CHUNK:
        off = jnp.where(((ri // (2 * blk)) == (ci // (2 * blk))) & ((ri // blk) != (ci // blk)), m, 0.0)
        ta = _bdot(t, off)
        yield
        t = t - _bdot(ta, t)
        yield
        blk *= 2

    kb = kst * beta_c
    rhs = jnp.concatenate([vst * beta_c, kb * e_gc], axis=1)
    uw = _bdot(t, rhs)
    yield
    return (uw[:, 0:HEAD_DIM],
            uw[:, HEAD_DIM:2 * HEAD_DIM].astype(BF16),
            (qst * e_gc * scale).astype(BF16),
            (kst * e_rest).astype(BF16),
            (jnp.where(incl, qk * decay, 0.0) * scale).astype(BF16),
            jnp.exp(tot)[DN_CHUNK - 4:DN_CHUNK + 4, :])


def _dn_prep_group(c0, group, in_refs, out_refs):
    *qkv_refs, gt_ref = in_refs
    gates = _dn_gate_rows(c0, group, gt_ref)
    gens = [_dn_prep_compute(*_dn_prep_load(c0 + j, *qkv_refs), gates[j]) for j in range(group)]
    results = [None] * group
    while any(r is None for r in results):
        for j, gen in enumerate(gens):
            try:
                next(gen)
            except StopIteration as done:
                results[j] = done.value
    for j, res in enumerate(results):
        for ref, val in zip(out_refs, res):
            ref[c0 + j] = val


def _dn_scan_step(cf, cb, u_ref, w_ref, qd_ref, kd_ref, qk_ref, ar_ref, s_ref, of_ref, ob_ref):
    h = DN_CHUNK
    d = HEAD_DIM

    def pick(ref):
        return jnp.concatenate([ref[cf, 0:h, :], ref[cb, h:2 * h, :]], axis=0)

    s = s_ref[...]
    lhs = jnp.concatenate([pick(w_ref), pick(qd_ref)], axis=0)
    r = _dot(lhs, s.astype(BF16))
    ws = jnp.concatenate([r[0:h, 0:d], r[h:2 * h, d:2 * d]], axis=0)
    qs = jnp.concatenate([r[2 * h:3 * h, 0:d], r[3 * h:4 * h, d:2 * d]], axis=0)
    vn = (pick(u_ref) - ws).astype(BF16)
    o = qs + _dot(pick(qk_ref), vn)
    if of_ref is not None:
        of_ref[pl.ds(pl.multiple_of(cf * h, h), h), :] = o[0:h]
        ob_ref[pl.ds(pl.multiple_of(cb * h, h), h), :] = o[h:2 * h]
    ri = lax.broadcasted_iota(jnp.int32, (2 * h, 2 * d), 0)
    ci = lax.broadcasted_iota(jnp.int32, (2 * h, 2 * d), 1)
    vn_bd = jnp.where((ri // h) == (ci // d), jnp.concatenate([vn, vn], axis=1), jnp.zeros((), BF16))
    upd = _dot_tn(pick(kd_ref), vn_bd)
    a = jnp.concatenate([jnp.broadcast_to(ar_ref[cf, 0:1, :], (d, d)),
                         jnp.broadcast_to(ar_ref[cb, 4:5, :], (d, d))], axis=1)
    s_ref[...] = s * a + upd


def _dn_kernel(n_lat, n_ctx,
               qx_ref, kx_ref, vx_ref, qc_ref, kc_ref, vc_ref, wq_ref, wk_ref, wv_ref,
               gbx_ref, gbc_ref, z_ref, ng_ref, y_ref,
               pad_ref, q_s, k_s, v_s, qc_s, kc_s, vc_s,
               u_x, w_x, qd_x, kd_x, qk_x, ar_x, u_c, w_c, qd_c, kd_c, qk_c, ar_c,
               s_ref, of_ref, ob_ref):
    nc_x = n_lat // DN_CHUNK
    nc_c = n_ctx // DN_CHUNK

    _conv_silu(qc_ref, wq_ref, pad_ref, qc_s, n_ctx, True)
    _conv_silu(kc_ref, wk_ref, pad_ref, kc_s, n_ctx, True)
    _conv_silu(vc_ref, wv_ref, pad_ref, vc_s, n_ctx, False)
    _conv_silu(qx_ref, wq_ref, pad_ref, q_s, n_lat, True)
    _conv_silu(kx_ref, wk_ref, pad_ref, k_s, n_lat, True)
    _conv_silu(vx_ref, wv_ref, pad_ref, v_s, n_lat, False)

    grp_c = math.gcd(nc_c, DN_PREP_GROUP)
    grp_x = math.gcd(nc_x, DN_PREP_GROUP)
    assert grp_c % 2 == 0 and grp_x % 2 == 0, "a group's tokens must cover whole 128-lane tiles"

    def prep_c(i, carry):
        _dn_prep_group(i * grp_c, grp_c, (qc_s, kc_s, vc_s, gbc_ref), (u_c, w_c, qd_c, kd_c, qk_c, ar_c))
        return carry

    def prep_x(i, carry):
        _dn_prep_group(i * grp_x, grp_x, (q_s, k_s, v_s, gbx_ref), (u_x, w_x, qd_x, kd_x, qk_x, ar_x))
        return carry

    lax.fori_loop(0, nc_c // grp_c, prep_c, 0)
    lax.fori_loop(0, nc_x // grp_x, prep_x, 0)

    s_ref[...] = jnp.zeros(s_ref.shape, F32)

    def scan_c(i, carry):
        _dn_scan_step(i, nc_c - 1 - i, u_c, w_c, qd_c, kd_c, qk_c, ar_c, s_ref, None, None)
        return carry

    def scan_x(i, carry):
        _dn_scan_step(i, nc_x - 1 - i, u_x, w_x, qd_x, kd_x, qk_x, ar_x, s_ref, of_ref, ob_ref)
        return carry

    lax.fori_loop(0, nc_c, scan_c, 0)
    lax.fori_loop(0, nc_x, scan_x, 0)

    rb = min(CONV_ROWS, n_lat)
    for r0 in range(0, n_lat, rb):
        o = of_ref[r0:r0 + rb, :] + ob_ref[r0:r0 + rb, :]
        ms = jnp.mean(o * o, axis=-1, keepdims=True)
        y = (o * lax.rsqrt(ms + EPS)) * ng_ref[...] * _silu(z_ref[0, r0:r0 + rb, :].astype(F32))
        y_ref[0, r0:r0 + rb, :] = y.astype(y_ref.dtype)


def _deltanet(qkv_x, qkv_c, conv_w, gt_x, gt_c, z, norm_g):
    b, n_lat, _ = qkv_x.shape
    n_ctx = qkv_c.shape[1]
    gbx = gt_x.reshape(b, 2, 2, N_HEADS, 1, n_lat)
    gbc = gt_c.reshape(b, 2, 2, N_HEADS, 1, n_ctx)
    hh = N_HEADS
    d = HEAD_DIM
    nc_x, nc_c = n_lat // DN_CHUNK, n_ctx // DN_CHUNK
    tok = lambda n, off: pl.BlockSpec((1, n, d), lambda i, j: (i, 0, off + j))
    cw = lambda off: pl.BlockSpec((DN_CONV, d), lambda i, j: (0, off + j))
    gspec = lambda n: pl.BlockSpec((1, 2, 2, 1, 1, n), lambda i, j: (i, 0, 0, j, 0, 0))
    chunk_scratch = lambda nc: [
        pltpu.VMEM((nc, 2 * DN_CHUNK, d), F32),
        pltpu.VMEM((nc, 2 * DN_CHUNK, d), BF16),
        pltpu.VMEM((nc, 2 * DN_CHUNK, d), BF16),
        pltpu.VMEM((nc, 2 * DN_CHUNK, d), BF16),
        pltpu.VMEM((nc, 2 * DN_CHUNK, 2 * DN_CHUNK), BF16),
        pltpu.VMEM((nc, SUBLANES, 2 * DN_CHUNK), F32),
    ]
    return pl.pallas_call(
        functools.partial(_dn_kernel, n_lat, n_ctx),
        grid=(b, hh),
        in_specs=[tok(n_lat, 0), tok(n_lat, hh), tok(n_lat, 2 * hh),
                  tok(n_ctx, 0), tok(n_ctx, hh), tok(n_ctx, 2 * hh),
                  cw(0), cw(hh), cw(2 * hh),
                  gspec(n_lat), gspec(n_ctx),
                  tok(n_lat, 0), _const_spec((1, d))],
        out_specs=tok(n_lat, 0),
        out_shape=jax.ShapeDtypeStruct((b, n_lat, hh * d), BF16),
        scratch_shapes=[pltpu.VMEM((n_lat + 2 * CONV_PAD, d), F32)]
        + [pltpu.VMEM((n_lat, d), F32)] * 3 + [pltpu.VMEM((n_ctx, d), F32)] * 3
        + chunk_scratch(nc_x) + chunk_scratch(nc_c)
        + [pltpu.VMEM((d, 2 * d), F32), pltpu.VMEM((n_lat, d), F32), pltpu.VMEM((n_lat, d), F32)],
        compiler_params=_cparams(("parallel", "parallel")),
        name="deltanet",
    )(qkv_x, qkv_x, qkv_x, qkv_c, qkv_c, qkv_c, conv_w, conv_w, conv_w, gbx, gbc, z, norm_g)


def _rope(x, cos, sin):
    lane = lax.broadcasted_iota(jnp.int32, x.shape, 1)
    quarter = MLA_ROPE // 4
    rot = jnp.where((lane // quarter) % 2 == 0,
                    -pltpu.roll(x, LANES - quarter, axis=1), pltpu.roll(x, quarter, axis=1))
    return x * cos + rot * sin


def _rms(x, n):
    return lax.rsqrt(jnp.sum(x * x, axis=-1, keepdims=True) / n + EPS)


def _mla_proj_kernel(with_q, cq_ref, small_ref, gqa_ref, gkva_ref, wq_ref, wkv_ref, gq_ref, gk_ref,
                     cos_ref, sin_ref, *out_refs):
    hd = HEAD_DIM
    hh = N_HEADS
    cos, sin = cos_ref[...], sin_ref[...]
    small = small_ref[0]
    ckv = small[:, 0:MLA_KV_RANK]
    kpe = small[:, MLA_KV_RANK:MLA_KV_RANK + LANES]
    lane = lax.broadcasted_iota(jnp.int32, kpe.shape, 1)
    kpe = jnp.where(lane < MLA_ROPE, kpe, 0.0)
    kpe_ssq = jnp.sum(kpe * kpe, axis=-1, keepdims=True)
    ckvn = (ckv * _rms(ckv, MLA_KV_RANK) * gkva_ref[...]).astype(BF16)
    kvf = _dot(ckvn, wkv_ref[...])
    if with_q:
        q_ref, k_ref, v_ref = out_refs
        cq = cq_ref[0].astype(F32)
        cqn = (cq * _rms(cq, MLA_Q_RANK) * gqa_ref[...]).astype(BF16)
        qf = _dot(cqn, wq_ref[...])
        q_scale = MLA_QK_DIM ** -0.5 * math.log2(math.e)
    else:
        k_ref, v_ref = out_refs
    gq, gk = gq_ref[...], gk_ref[...]
    for h in range(hh):
        if with_q:
            qn = qf[:, h * hd:(h + 1) * hd]
            qp = qf[:, (hh + h) * hd:(hh + h + 1) * hd]
            ssq = jnp.sum(qn * qn, axis=-1, keepdims=True) + jnp.sum(qp * qp, axis=-1, keepdims=True)
            r = lax.rsqrt(ssq / MLA_QK_DIM + EPS)
            q_ref[0, h, :, 0:hd] = (qn * r * gq[:, 0:hd] * q_scale).astype(q_ref.dtype)
            q_ref[0, h, :, hd:2 * hd] = (_rope(qp * r * gq[:, hd:2 * hd], cos, sin) * q_scale).astype(q_ref.dtype)
        kn = kvf[:, h * hd:(h + 1) * hd]
        r = lax.rsqrt((jnp.sum(kn * kn, axis=-1, keepdims=True) + kpe_ssq) / MLA_QK_DIM + EPS)
        k_ref[0, h, :, 0:hd] = (kn * r * gk[:, 0:hd]).astype(k_ref.dtype)
        k_ref[0, h, :, hd:2 * hd] = _rope(kpe * r * gk[:, hd:2 * hd], cos, sin).astype(k_ref.dtype)
        v_ref[0, h] = kvf[:, (hh + h) * hd:(hh + h + 1) * hd].T.astype(v_ref.dtype)


def _mla_projections(cq, small, gqa, gkva, wq, wkv, gq, gk, cos, sin, with_q, tl):
    b, n, _ = small.shape
    hh, hd = N_HEADS, HEAD_DIM
    row = lambda w: pl.BlockSpec((1, tl, w), lambda i, j: (i, j, 0))
    head = lambda w: pl.BlockSpec((1, hh, tl, w), lambda i, j: (i, 0, j, 0))
    tab = pl.BlockSpec((tl, LANES), lambda i, j: (j, 0))
    out_specs = [head(2 * hd), pl.BlockSpec((1, hh, hd, tl), lambda i, j: (i, 0, 0, j))]
    out_shape = [jax.ShapeDtypeStruct((b, hh, n, 2 * hd), BF16), jax.ShapeDtypeStruct((b, hh, hd, n), BF16)]
    if with_q:
        out_specs = [head(2 * hd)] + out_specs
        out_shape = [jax.ShapeDtypeStruct((b, hh, n, 2 * hd), BF16)] + out_shape
    return pl.pallas_call(
        functools.partial(_mla_proj_kernel, with_q),
        grid=(b, n // tl),
        in_specs=[row(cq.shape[-1]), row(small.shape[-1]), _const_spec(gqa.shape), _const_spec(gkva.shape),
                  _const_spec(wq.shape), _const_spec(wkv.shape), _const_spec(gq.shape), _const_spec(gk.shape),
                  tab, tab],
        out_specs=out_specs,
        out_shape=out_shape,
        compiler_params=_cparams(("parallel", "parallel")),
        name="mla_projections_q" if with_q else "mla_projections_kv",
    )(cq, small, gqa, gkva, wq, wkv, gq, gk, cos, sin)


ATTN_SUB_COLS = 256
ATTN_LOOKAHEAD = 2


def _attn_kernel(q_ref, kc_ref, vct_ref, kx_ref, vxt_ref, o_ref):
    tq = q_ref.shape[2]
    n_sub = tq // ATTN_SUB_COLS

    def scores(i):
        q = q_ref[0, 0, i * ATTN_SUB_COLS:(i + 1) * ATTN_SUB_COLS, :]
        return _dot_nt(kc_ref[0, 0], q), _dot_nt(kx_ref[0, 0], q)

    pending = [scores(i) for i in range(min(ATTN_LOOKAHEAD, n_sub))]
    for i in range(n_sub):
        if i + ATTN_LOOKAHEAD < n_sub:
            pending.append(scores(i + ATTN_LOOKAHEAD))
        s_c, s_x = pending.pop(0)
        m = jnp.maximum(jnp.max(s_c, axis=0, keepdims=True), jnp.max(s_x, axis=0, keepdims=True))
        p_c = jnp.exp2(s_c - m)
        p_x = jnp.exp2(s_x - m)
        l = jnp.sum(p_c, axis=0, keepdims=True) + jnp.sum(p_x, axis=0, keepdims=True)
        o_t = _dot(vct_ref[0, 0], p_c.astype(BF16)) + _dot(vxt_ref[0, 0], p_x.astype(BF16))
        o_ref[0, i * ATTN_SUB_COLS:(i + 1) * ATTN_SUB_COLS, :] = (o_t / l).T.astype(o_ref.dtype)


def _attention(q, k_c, vt_c, k_x, vt_x, tq):
    b, hh, n, dq = q.shape
    n_ctx = k_c.shape[2]
    hd = vt_x.shape[2]
    keys = lambda n_: pl.BlockSpec((1, 1, n_, dq), lambda i, j, t: (i, j, 0, 0))
    vals = lambda n_: pl.BlockSpec((1, 1, hd, n_), lambda i, j, t: (i, j, 0, 0))
    return pl.pallas_call(
        _attn_kernel,
        grid=(b, hh, n // tq),
        in_specs=[pl.BlockSpec((1, 1, tq, dq), lambda i, j, t: (i, j, t, 0)),
                  keys(n_ctx), vals(n_ctx), keys(n), vals(n)],
        out_specs=pl.BlockSpec((1, tq, hd), lambda i, j, t: (i, t, j)),
        out_shape=jax.ShapeDtypeStruct((b, n, hh * hd), BF16),
        compiler_params=_cparams(("parallel", "parallel", "parallel")),
        name="attention",
    )(q, k_c, vt_c, k_x, vt_x)


def _merge_kernel(ya_ref, yb_ref, gate_ref, x_ref, g1_ref, sh2_ref, sc2_ref, n2g_ref,
                  wa_ref, wb_ref, wo_ref, rwt_ref, xmid_ref, h2_ref, aff_ref):
    d = x_ref.shape[-1]
    ga = _sigmoid(gate_ref[0, :, 0:d].astype(F32))
    gb = _sigmoid(gate_ref[0, :, d:2 * d].astype(F32))
    mix = ga * _dot(ya_ref[0], wa_ref[...]) + gb * _dot(yb_ref[0], wb_ref[...])
    mix = _dot(mix.astype(BF16), wo_ref[...])
    xm = x_ref[0] + g1_ref[0] * mix
    xmid_ref[0] = xm
    h2 = _modulated_norm(xm, n2g_ref[...], sh2_ref[0], sc2_ref[0])
    h2_ref[0] = h2
    logits = _dot_nt(rwt_ref[...], h2, precision=HI)
    mx = jnp.max(logits, axis=0, keepdims=True)
    ex = jnp.exp(logits - mx)
    aff_ref[0] = ex / jnp.sum(ex, axis=0, keepdims=True)


def _merge(ya, yb, gate, x, g1, sh2, sc2, n2g, wa, wb, wo, rwt, tl):
    b, n, d = x.shape
    ne = rwt.shape[0]
    row = lambda w: pl.BlockSpec((1, tl, w), lambda i, j: (i, j, 0))
    vec = pl.BlockSpec((1, 1, d), lambda i, j: (i, 0, 0))
    return pl.pallas_call(
        _merge_kernel,
        grid=(b, n // tl),
        in_specs=[row(d), row(d), row(2 * d), row(d), vec, vec, vec, _const_spec((1, d)),
                  _const_spec(wa.shape), _const_spec(wb.shape), _const_spec(wo.shape), _const_spec(rwt.shape)],
        out_specs=[row(d), row(d), pl.BlockSpec((1, ne, tl), lambda i, j: (i, 0, j))],
        out_shape=[jax.ShapeDtypeStruct((b, n, d), F32), jax.ShapeDtypeStruct((b, n, d), F32),
                   jax.ShapeDtypeStruct((b, ne, n), F32)],
        compiler_params=_cparams(("parallel", "parallel")),
        name="merge_router",
    )(ya, yb, gate, x, g1, sh2, sc2, n2g, wa, wb, wo, rwt)


TOPK_RANK_ROWS = 64


def _lane_cumsum(x_bf16, tri_bf16, out_ref):
    ne, n = x_bf16.shape
    carry = jnp.zeros((ne, 1), F32)
    for j in range(n // LANES):
        blk = _dot(x_bf16[:, j * LANES:(j + 1) * LANES], tri_bf16) + carry
        out_ref[:, j * LANES:(j + 1) * LANES] = blk
        carry = blk[:, LANES - 1:LANES]


def _topk_kernel(cap, aff_ref, idx_ref, gate_ref, cum_ref, affsel_ref, cumrep_ref, affrep_ref):
    aff = aff_ref[0]
    ne, n = aff.shape
    bits = pltpu.bitcast(aff, jnp.int32)

    def search(i, t):
        cand = t | (jnp.int32(1) << (30 - i))
        cnt = jnp.sum(jnp.where(bits >= cand, 1.0, 0.0), axis=-1, keepdims=True)
        return jnp.where(cnt >= cap, cand, t)

    thr = lax.fori_loop(0, 31, search, jnp.zeros((ne, 1), jnp.int32))
    gt = bits > thr
    eq = bits == thr
    need = cap - jnp.sum(jnp.where(gt, 1.0, 0.0), axis=-1, keepdims=True)
    ri = lax.broadcasted_iota(jnp.int32, (LANES, LANES), 0)
    ci = lax.broadcasted_iota(jnp.int32, (LANES, LANES), 1)
    tri = jnp.where(ri <= ci, 1.0, 0.0).astype(BF16)
    _lane_cumsum(jnp.where(eq, 1.0, 0.0).astype(BF16), tri, cum_ref)
    sel = gt | (eq & (cum_ref[...] <= need))
    affsel_ref[...] = jnp.where(sel, aff, 0.0)
    _lane_cumsum(jnp.where(sel, 1.0, 0.0).astype(BF16), tri, cum_ref)
    for ex in range(ne):
        cumrep_ref[ex] = jnp.broadcast_to(cum_ref[ex:ex + 1, :], (SUBLANES, n))
        affrep_ref[ex] = jnp.broadcast_to(affsel_ref[ex:ex + 1, :], (SUBLANES, n))

    rows = TOPK_RANK_ROWS
    lane = lax.broadcasted_iota(jnp.int32, (rows, LANES), 1)
    idx_ref[0] = jnp.zeros((cap, LANES), jnp.int32)
    gate_ref[0] = jnp.zeros((cap, LANES), F32)

    def per_expert(e, carry):
        for r0 in range(0, cap, rows):
            rank = (lax.broadcasted_iota(jnp.int32, (rows, LANES), 0) + r0).astype(F32)
            cnt = jnp.zeros((rows, LANES), F32)
            gat = jnp.zeros((rows, LANES), F32)
            for j in range(n // LANES):
                c_blk = cumrep_ref[e, 0:1, j * LANES:(j + 1) * LANES]
                a_blk = affrep_ref[e, 0:1, j * LANES:(j + 1) * LANES]
                cnt = cnt + jnp.where(c_blk <= rank, 1.0, 0.0)
                gat = gat + jnp.where(c_blk == rank + 1.0, a_blk, 0.0)
            pos = jnp.sum(cnt, axis=-1, keepdims=True).astype(jnp.int32)
            g = jnp.sum(gat, axis=-1, keepdims=True)
            idx_ref[0, r0:r0 + rows, :] = jnp.where(lane == e, pos, idx_ref[0, r0:r0 + rows, :])
            gate_ref[0, r0:r0 + rows, :] = jnp.where(lane == e, g, gate_ref[0, r0:r0 + rows, :])
        return carry

    lax.fori_loop(0, ne, per_expert, 0)


def _expert_topk(aff_t, cap):
    b, ne, n = aff_t.shape
    return pl.pallas_call(
        functools.partial(_topk_kernel, cap),
        grid=(b,),
        in_specs=[pl.BlockSpec((1, ne, n), lambda i: (i, 0, 0))],
        out_specs=[pl.BlockSpec((1, cap, LANES), lambda i: (i, 0, 0))] * 2,
        out_shape=[jax.ShapeDtypeStruct((b, cap, LANES), jnp.int32), jax.ShapeDtypeStruct((b, cap, LANES), F32)],
        scratch_shapes=[pltpu.VMEM((ne, n), F32), pltpu.VMEM((ne, n), F32),
                        pltpu.VMEM((ne, SUBLANES, n), F32), pltpu.VMEM((ne, SUBLANES, n), F32)],
        compiler_params=_cparams(("parallel",)),
        name="expert_topk",
    )(aff_t)


MOE_FF_TILE = 512
MOE_UNROLL = 8


def _moe_kernel(cap, idx_ref, gate_ref, h_ref, wg_ref, wu_ref, wd_ref, o_ref, xe_ref, ye_ref):
    e = pl.program_id(1)
    f = pl.program_id(2)
    nf = pl.num_programs(2)

    @pl.when((e == 0) & (f == 0))
    def _():
        o_ref[...] = jnp.zeros(o_ref.shape, o_ref.dtype)

    @pl.when(f == 0)
    def _():
        def gather(i, carry):
            for u in range(MOE_UNROLL):
                r = i * MOE_UNROLL + u
                xe_ref[pl.ds(r, 1), :] = h_ref[0, pl.ds(idx_ref[0, 0, 0, r], 1), :]
            return carry
        lax.fori_loop(0, cap // MOE_UNROLL, gather, 0)

    x = xe_ref[...].astype(BF16)
    hid = (_silu(_dot(x, wg_ref[0])) * _dot(x, wu_ref[0])).astype(BF16)
    part = _dot(hid, wd_ref[0])

    @pl.when(f == 0)
    def _():
        ye_ref[...] = part

    @pl.when(f > 0)
    def _():
        ye_ref[...] += part

    @pl.when(f == nf - 1)
    def _():
        def scatter(i, carry):
            rows = [i * MOE_UNROLL + u for u in range(MOE_UNROLL)]
            toks = [idx_ref[0, 0, 0, r] for r in rows]
            new = [o_ref[0, pl.ds(t, 1), :] + gate_ref[0, 0, 0, r] * ye_ref[pl.ds(r, 1), :]
                   for r, t in zip(rows, toks)]
            for t, val in zip(toks, new):
                o_ref[0, pl.ds(t, 1), :] = val
            return carry
        lax.fori_loop(0, cap // MOE_UNROLL, scatter, 0)


def _moe(idx, gate, h2, wg, wu, wd):
    b, n, d = h2.shape
    ne, _, ffp = wg.shape
    cap = idx.shape[-1]
    tf = MOE_FF_TILE
    smem = lambda: pl.BlockSpec((1, 1, 1, cap), lambda i, j, k: (i, j, 0, 0), memory_space=pltpu.SMEM)
    return pl.pallas_call(
        functools.partial(_moe_kernel, cap),
        grid=(b, ne, ffp // tf),
        in_specs=[smem(), smem(),
                  pl.BlockSpec((1, n, d), lambda i, j, k: (i, 0, 0), pipeline_mode=pl.Buffered(1)),
                  pl.BlockSpec((1, d, tf), lambda i, j, k: (j, 0, k)),
                  pl.BlockSpec((1, d, tf), lambda i, j, k: (j, 0, k)),
                  pl.BlockSpec((1, tf, d), lambda i, j, k: (j, k, 0))],
        out_specs=pl.BlockSpec((1, n, d), lambda i, j, k: (i, 0, 0), pipeline_mode=pl.Buffered(1)),
        out_shape=jax.ShapeDtypeStruct((b, n, d), F32),
        scratch_shapes=[pltpu.VMEM((cap, d), F32), pltpu.VMEM((cap, d), F32)],
        compiler_params=_cparams(("parallel", "arbitrary", "arbitrary")),
        name="expert_ffn",
    )(idx[:, :, None, :], gate[:, :, None, :], h2, wg, wu, wd)


def _final_kernel(xm_ref, moe_ref, g_ref, o_ref):
    o_ref[0] = xm_ref[0] + g_ref[0] * moe_ref[0]


def _final(xmid, moe, g2, tl):
    b, n, d = xmid.shape
    row = pl.BlockSpec((1, tl, d), lambda i, j: (i, j, 0))
    return pl.pallas_call(
        _final_kernel,
        grid=(b, n // tl),
        in_specs=[row, row, pl.BlockSpec((1, 1, d), lambda i, j: (i, 0, 0))],
        out_specs=row,
        out_shape=jax.ShapeDtypeStruct((b, n, d), F32),
        compiler_params=_cparams(("parallel", "parallel")),
        name="final_residual",
    )(xmid, moe, g2)


def _rope_tables(n_lat):
    rows = n_lat // GRID_W
    row = jnp.repeat(jnp.arange(rows), GRID_W).astype(F32)
    col = jnp.broadcast_to(jnp.arange(GRID_W), (rows, GRID_W)).reshape(-1).astype(F32)
    n_freq = MLA_ROPE // 4
    inv_freq = ROPE_BASE ** (-jnp.arange(n_freq, dtype=F32) / n_freq)
    ang_r = row[:, None] * inv_freq
    ang_c = col[:, None] * inv_freq
    ang = jnp.concatenate([ang_r, ang_r, ang_c, ang_c], axis=-1)
    zeros = jnp.zeros((n_lat, LANES - MLA_ROPE), F32)
    return jnp.concatenate([jnp.cos(ang), zeros], axis=-1), jnp.concatenate([jnp.sin(ang), zeros], axis=-1)


def _pad_lanes(v, width):
    return jnp.concatenate([v, jnp.zeros(v.shape[:-1] + (width - v.shape[-1],), v.dtype)], axis=-1)


def kernel(x, c, ctx, c_ctx, ada_w, ada_b, norm1_g, norm2_g, w_in, conv_w, a_log, dt_bias, dn_norm_g, w_out_a, q_a_norm_g, w_uq, kv_a_norm_g, w_ukv, q_norm_g, k_norm_g, w_out_b, w_o, router_w, w_gate, w_up, w_down):
    assert ada_w.shape[0] == 1, "single-layer block"
    b, n_lat, d = x.shape
    n_ctx = ctx.shape[1]
    hh, hd = N_HEADS, HEAD_DIM
    qkv_dim = 3 * hh * hd
    nb = 2 * hh

    n_cond = -(-(b + 1) // SUBLANES) * SUBLANES
    cond = jnp.concatenate([c, c_ctx[None], jnp.zeros((n_cond - b - 1, d), F32)], axis=0)
    mod = _modulation(cond, ada_w[0], ada_b[0])
    mods = [mod[:, i * d:(i + 1) * d] for i in range(6)]
    lat = lambda m: m[:b, None, :]
    cvec = lambda m: jnp.broadcast_to(m[b][None, None, :], (b, 1, d))

    o_z = qkv_dim
    o_alpha = o_z + hh * hd
    o_beta = o_alpha + nb
    o_cq = o_beta + nb
    o_ckv = o_cq + MLA_Q_RANK
    o_kr = o_ckv + MLA_KV_RANK
    o_gate = o_kr + MLA_ROPE
    w = w_in[0]
    w_big = jnp.concatenate([w[:, :o_alpha], w[:, o_gate:], w[:, o_cq:o_ckv]], axis=1).astype(BF16)
    small_w = _pad_lanes(w[:, o_ckv:o_gate], MLA_KV_RANK + LANES).astype(BF16)
    w_ab_t = jnp.transpose(w[:, o_alpha:o_cq]).astype(BF16)
    gate_pad = jnp.zeros((nb, 1), F32)
    alog = jnp.concatenate([a_log[0].reshape(nb, 1), gate_pad], axis=0)
    dtb = jnp.concatenate([dt_bias[0].reshape(nb, 1), gate_pad], axis=0)
    widths = (qkv_dim, hh * hd, N_BRANCHES * d, MLA_Q_RANK)
    g1 = norm1_g[0][None]
    qkv_x, z_x, gt_x, cq_x, sm_x, dg_x = _in_projection(x, lat(mods[0]), lat(mods[1]), g1, w_big, small_w, w_ab_t,
                                                        alog, dtb, widths, 512)
    qkv_c, _, _, cq_c, sm_c, dg_c = _in_projection(ctx, cvec(mods[0]), cvec(mods[1]), g1, w_big, small_w, w_ab_t,
                                                   alog, dtb, widths, min(n_ctx, 512))

    y_a = _deltanet(qkv_x, qkv_c, conv_w[0], dg_x, dg_c, z_x, dn_norm_g[0][None])

    wq = w_uq[0].reshape(MLA_Q_RANK, hh, MLA_QK_DIM)
    wq = jnp.concatenate([wq[:, :, :hd].reshape(MLA_Q_RANK, hh * hd),
                          _pad_lanes(wq[:, :, hd:], hd).reshape(MLA_Q_RANK, hh * hd)], axis=1).astype(BF16)
    wkv = w_ukv[0].reshape(MLA_KV_RANK, hh, 2 * hd)
    wkv = jnp.concatenate([wkv[:, :, :hd].reshape(MLA_KV_RANK, hh * hd),
                           wkv[:, :, hd:].reshape(MLA_KV_RANK, hh * hd)], axis=1).astype(BF16)
    gq = _pad_lanes(q_norm_g[0], 2 * hd)[None]
    gk = _pad_lanes(k_norm_g[0], 2 * hd)[None]
    gqa, gkva = q_a_norm_g[0][None], kv_a_norm_g[0][None]
    cos, sin = _rope_tables(n_lat)
    cos_c = _pad_lanes(jnp.ones((n_ctx, MLA_ROPE), F32), LANES)
    sin_c = jnp.zeros((n_ctx, LANES), F32)
    q_x, k_x, v_x = _mla_projections(cq_x, sm_x, gqa, gkva, wq, wkv, gq, gk, cos, sin, True, 512)
    k_c, v_c = _mla_projections(cq_c, sm_c, gqa, gkva, wq, wkv, gq, gk, cos_c, sin_c, False, min(n_ctx, 512))
    y_b = _attention(q_x, k_c, v_c, k_x, v_x, min(n_lat, 1024))

    rwt = jnp.transpose(router_w[0])
    x_mid, h2, aff_t = _merge(y_a, y_b, gt_x, x, lat(mods[2]), lat(mods[3]), lat(mods[4]), norm2_g[0][None],
                              w_out_a[0].astype(BF16), w_out_b[0].astype(BF16), w_o[0].astype(BF16), rwt, 512)

    cap = EC_CAPACITY * n_lat // N_EXPERTS
    idx, gate = _expert_topk(aff_t, cap)
    idx = jnp.transpose(idx[:, :, :N_EXPERTS], (0, 2, 1))
    gate = jnp.transpose(gate[:, :, :N_EXPERTS], (0, 2, 1))
    ff = w_gate.shape[-1]
    ffp = -(-ff // MOE_FF_TILE) * MOE_FF_TILE
    wg = _pad_lanes(w_gate[0], ffp).astype(BF16)
    wu = _pad_lanes(w_up[0], ffp).astype(BF16)
    wd = jnp.concatenate([w_down[0], jnp.zeros((N_EXPERTS, ffp - ff, d), F32)], axis=1).astype(BF16)
    moe = _moe(idx, gate, h2, wg, wu, wd)
    return _final(x_mid, moe, lat(mods[5]), 512)
```

```python
import functools
import math

import jax
import jax.numpy as jnp
from jax import lax
from jax.experimental import pallas as pl
from jax.experimental.pallas import tpu as pltpu

F32 = jnp.float32
BF16 = jnp.bfloat16
EPS = 1e-6

N_HEADS = 8
HEAD_DIM = 128
DN_CONV = 5
DN_CHUNK = 64
MLA_Q_RANK = 384
MLA_KV_RANK = 256
MLA_ROPE = 64
MLA_QK_DIM = HEAD_DIM + MLA_ROPE
GRID_W = 64
ROPE_BASE = 10000.0
N_EXPERTS = 16
EC_CAPACITY = 2
N_BRANCHES = 2

LANES = 128
SUBLANES = 8
VMEM_LIMIT_BYTES = 56 * 1024 * 1024

HI = lax.Precision.HIGHEST


def _cparams(sem):
    return pltpu.CompilerParams(dimension_semantics=sem, vmem_limit_bytes=VMEM_LIMIT_BYTES)


def _dot(a, b, precision=None):
    return jnp.dot(a, b, preferred_element_type=F32, precision=precision)


def _dot_nt(a, b, precision=None):
    return lax.dot_general(a, b, (((1,), (1,)), ((), ())), preferred_element_type=F32, precision=precision)


def _dot_tn(a, b):
    return lax.dot_general(a, b, (((0,), (0,)), ((), ())), preferred_element_type=F32)


def _sigmoid(x):
    return 1.0 / (1.0 + jnp.exp(-x))


def _silu(x):
    return x * _sigmoid(x)


def _const_spec(shape):
    nd = len(shape)
    return pl.BlockSpec(shape, lambda *_: (0,) * nd, pipeline_mode=pl.Buffered(1))


def _to_tile_major(ref, x, lead=()):
    rows = x.shape[0]
    for s_ in range(SUBLANES):
        ref[lead + (pl.ds(s_, rows, stride=SUBLANES), slice(None))] = x[:, s_ * LANES:(s_ + 1) * LANES]


def _from_tile_major(ref, rows, lead=()):
    return jnp.concatenate([ref[lead + (pl.ds(s_, rows, stride=SUBLANES), slice(None))] for s_ in range(SUBLANES)],
                           axis=1)


def _tile(i):
    return pl.ds(pl.multiple_of(i * SUBLANES, SUBLANES), SUBLANES)


def _mod_kernel(c_ref, w_ref, b_ref, o_ref):
    c = c_ref[...]
    o_ref[...] = _dot(_silu(c), w_ref[...], precision=HI) + b_ref[...]


def _modulation(cond, ada_w, ada_b):
    n, d = cond.shape
    n_out = ada_w.shape[1]
    tn = d
    return pl.pallas_call(
        _mod_kernel,
        grid=(n_out // tn,),
        in_specs=[pl.BlockSpec((n, d), lambda j: (0, 0)),
                  pl.BlockSpec((d, tn), lambda j: (0, j)),
                  pl.BlockSpec((1, tn), lambda j: (0, j))],
        out_specs=pl.BlockSpec((n, tn), lambda j: (0, j)),
        out_shape=jax.ShapeDtypeStruct((n, n_out), F32),
        compiler_params=_cparams(("parallel",)),
        name="modulation",
    )(cond, ada_w, ada_b.reshape(1, n_out))


def _modulated_norm(x, g, shift, scale):
    ms = jnp.mean(x * x, axis=-1, keepdims=True)
    return (x * lax.rsqrt(ms + EPS)) * g * (1.0 + scale) + shift


def _inproj_kernel(widths, x_ref, shift_ref, scale_ref, g_ref, wb_ref, ws_ref, wab_ref, alog_ref, dtb_ref,
                   *out_refs):
    big_refs, small_ref, gates_ref = out_refs[:-2], out_refs[-2], out_refs[-1]
    h = _modulated_norm(x_ref[0], g_ref[...], shift_ref[0], scale_ref[0]).astype(BF16)
    off = 0
    for o_ref, width in zip(big_refs, widths):
        for c0 in range(0, width, 1024):
            c1 = min(c0 + 1024, width)
            o_ref[0, :, c0:c1] = _dot(h, wb_ref[:, off + c0:off + c1]).astype(o_ref.dtype)
        off += width
    small_ref[0] = _dot(h, ws_ref[...])
    ab = _dot_nt(wab_ref[...], h)
    row_id = lax.broadcasted_iota(jnp.int32, ab.shape, 0)
    xs = ab + dtb_ref[...]
    softplus = jnp.maximum(xs, 0.0) + jnp.log1p(jnp.exp(-jnp.abs(xs)))
    gates_ref[0] = jnp.where(row_id < 2 * N_HEADS, -jnp.exp(alog_ref[...]) * softplus, _sigmoid(ab))


def _in_projection(x, shift, scale, g, w_big, w_small, w_ab_t, alog, dtb, widths, tl):
    b, n, d = x.shape
    row = lambda w: pl.BlockSpec((1, tl, w), lambda i, j: (i, j, 0))
    vec = pl.BlockSpec((1, 1, d), lambda i, j: (i, 0, 0))
    n_gate = w_ab_t.shape[0]
    outs = [jax.ShapeDtypeStruct((b, n, w), BF16) for w in widths]
    outs.append(jax.ShapeDtypeStruct((b, n, w_small.shape[1]), F32))
    outs.append(jax.ShapeDtypeStruct((b, n_gate, n), F32))
    return pl.pallas_call(
        functools.partial(_inproj_kernel, widths),
        grid=(b, n // tl),
        in_specs=[row(d), vec, vec, _const_spec((1, d)), _const_spec(w_big.shape), _const_spec(w_small.shape),
                  _const_spec(w_ab_t.shape), _const_spec(alog.shape), _const_spec(dtb.shape)],
        out_specs=[row(w) for w in widths] + [row(w_small.shape[1]),
                                              pl.BlockSpec((1, n_gate, tl), lambda i, j: (i, 0, j))],
        out_shape=outs,
        compiler_params=_cparams(("parallel", "parallel")),
        name="in_projection",
    )(x, shift, scale, g, w_big, w_small, w_ab_t, alog, dtb)


CONV_ROWS = 512
CONV_PAD = 8


def _conv_silu(raw_ref, w_ref, pad_ref, out_ref, n_tok, normalize):
    half = DN_CONV // 2
    zeros = jnp.zeros((CONV_PAD, LANES), F32)
    pad_ref[0:CONV_PAD, :] = zeros
    pad_ref[CONV_PAD + n_tok:2 * CONV_PAD + n_tok, :] = zeros
    pad_ref[CONV_PAD:CONV_PAD + n_tok, :] = raw_ref[0].astype(F32)
    w = w_ref[...]
    rb = min(CONV_ROWS, n_tok)
    for r0 in range(0, n_tok, rb):
        acc = None
        for j in range(DN_CONV):
            s = CONV_PAD + r0 + j - half
            term = pad_ref[s:s + rb, :] * w[j:j + 1, :]
            acc = term if acc is None else acc + term
        y = _silu(acc)
        if normalize:
            y = y * lax.rsqrt(jnp.sum(y * y, axis=-1, keepdims=True) + EPS)
        out_ref[r0:r0 + rb, :] = y


def _dn_masks():
    n = 2 * DN_CHUNK
    ri = lax.broadcasted_iota(jnp.int32, (n, n), 0)
    ci = lax.broadcasted_iota(jnp.int32, (n, n), 1)
    top = ri < DN_CHUNK
    same = (ri // DN_CHUNK) == (ci // DN_CHUNK)
    sgn = jnp.where(top, 1, -1)
    delta = (ri - ci) * sgn
    incl = same & (delta >= 0)
    strict = same & (delta > 0)
    incl_t = same & (delta <= 0)
    eye = ri == ci
    return top, incl, strict, incl_t, eye


DN_INV_BLOCK = 8
DN_PREP_GROUP = 8


def _hi_lo(a_f32):
    hi = a_f32.astype(BF16)
    return hi, (a_f32 - hi.astype(F32)).astype(BF16)


def _split_dot(a_f32, b_bf16):
    hi, lo = _hi_lo(a_f32)
    return _dot(hi, b_bf16) + _dot(lo, b_bf16)


def _bdot(a_f32, b_f32):
    return _dot(a_f32.astype(BF16), b_f32.astype(BF16))


def _dn_gate_rows(c0, group, gt_ref):
    n = 2 * DN_CHUNK
    t0 = pl.multiple_of(c0 * DN_CHUNK, n)
    lane = lax.broadcasted_iota(jnp.int32, (1, n), 1)
    rows = [[gt_ref[0, kind, dr, 0, :, pl.ds(t0, group * DN_CHUNK)] for dr in range(2)] for kind in range(2)]
    out = []
    for j in range(group):
        p, odd = divmod(j, 2)
        both = []
        for kind in range(2):
            fwd = rows[kind][0][:, p * n:(p + 1) * n]
            bwd = rows[kind][1][:, p * n:(p + 1) * n]
            if odd:
                both.append(jnp.where(lane < DN_CHUNK, pltpu.roll(fwd, DN_CHUNK, axis=1), bwd))
            else:
                both.append(jnp.where(lane < DN_CHUNK, fwd, pltpu.roll(bwd, DN_CHUNK, axis=1)))
        out.append(jnp.concatenate(both, axis=0))
    return out


def _dn_prep_load(c, q_ref, k_ref, v_ref):
    r0 = pl.multiple_of(c * DN_CHUNK, DN_CHUNK)
    return q_ref[pl.ds(r0, DN_CHUNK), :], k_ref[pl.ds(r0, DN_CHUNK), :], v_ref[pl.ds(r0, DN_CHUNK), :]


def _dn_prep_compute(q, k, v, gb):
    n = 2 * DN_CHUNK
    top, incl, strict, incl_t, eye = _dn_masks()
    g_row = jnp.broadcast_to(gb[0:1, :], (SUBLANES, n))
    beta_rows = jnp.broadcast_to(gb[1:2, :], (n, n))
    kst = jnp.concatenate([k, k], axis=0)
    qst = jnp.concatenate([q, q], axis=0)
    vst = jnp.concatenate([v, v], axis=0)
    kst_b = kst.astype(BF16)

    gc_row = _split_dot(g_row, jnp.where(incl_t, 1.0, 0.0).astype(BF16))[0:1, :]
    kk = _dot_nt(kst_b, kst_b)
    qk = _dot_nt(qst.astype(BF16), kst_b)
    yield
    c2 = jnp.broadcast_to(gc_row, (n, n))
    c1 = c2.T
    beta_c = beta_rows.T
    tot = jnp.where(top, c1[DN_CHUNK - 1:DN_CHUNK, :], c1[DN_CHUNK:DN_CHUNK + 1, :])
    decay = jnp.exp(jnp.where(incl, c1 - c2, -jnp.inf))
    e_gc = jnp.exp(c1)
    e_rest = jnp.exp(tot - c1)
    scale = HEAD_DIM ** -0.5

    m = jnp.where(strict, kk * beta_c * decay, 0.0)
    ri = lax.broadcasted_iota(jnp.int32, (n, n), 0)
    ci = lax.broadcasted_iota(jnp.int32, (n, n), 1)
    blk = DN_INV_BLOCK
    dg = jnp.where((ri // blk) == (ci // blk), m, 0.0)
    t = jnp.where(eye, 1.0, 0.0) - dg
    dp = _bdot(dg, dg)
    yield
    for it in range(blk.bit_length() - 2):
        t = t + _bdot(t, dp)
        if it + 1 < blk.bit_length() - 2:
            dp = _bdot(dp, dp)
        yield
    while blk < DN_CHUNK:
        off = jnp.where(((ri // (2 * blk)) == (ci // (2 * blk))) & ((ri // blk) != (ci // blk)), m, 0.0)
        ta = _bdot(t, off)
        yield
        t = t - _bdot(ta, t)
        yield
        blk *= 2

    kb = kst * beta_c
    rhs = jnp.concatenate([vst * beta_c, kb * e_gc], axis=1)
    uw = _bdot(t, rhs)
    yield
    d = HEAD_DIM
    u_b = uw[:, 0:d].astype(BF16)
    w_b = uw[:, d:2 * d].astype(BF16)
    kd = (kst * e_rest).astype(BF16)
    qkm = (jnp.where(incl, qk * decay, 0.0) * scale).astype(BF16)
    ri2 = lax.broadcasted_iota(jnp.int32, (n, 2 * d), 0)
    ci2 = lax.broadcasted_iota(jnp.int32, (n, 2 * d), 1)
    own_dir = (ri2 // DN_CHUNK) == (ci2 // d)
    zero = jnp.zeros((), BF16)
    wu_bd = jnp.concatenate([jnp.where(own_dir, jnp.concatenate([w_b, w_b], axis=1), zero),
                             jnp.where(own_dir, jnp.concatenate([u_b, u_b], axis=1), zero)], axis=1)
    kw = _dot_tn(kd, wu_bd)
    qw = _dot(qkm, jnp.concatenate([w_b, u_b], axis=1))
    yield
    return ((-kw[:, 0:2 * d]).astype(BF16),
            kw[:, 2 * d:4 * d],
            (qst * e_gc * scale - qw[:, 0:d]).astype(BF16),
            qw[:, d:2 * d],
            jnp.exp(tot)[DN_CHUNK - 4:DN_CHUNK + 4, :])


def _dn_prep_group(c0, group, in_refs, out_refs):
    *qkv_refs, gt_ref = in_refs
    gates = _dn_gate_rows(c0, group, gt_ref)
    gens = [_dn_prep_compute(*_dn_prep_load(c0 + j, *qkv_refs), gates[j]) for j in range(group)]
    results = [None] * group
    while any(r is None for r in results):
        for j, gen in enumerate(gens):
            try:
                next(gen)
            except StopIteration as done:
                results[j] = done.value
    for j, res in enumerate(results):
        for ref, val in zip(out_refs, res):
            ref[c0 + j] = val


def _dn_scan_step(cf, cb, a_ref, b_ref, q_ref, o0_ref, ar_ref, s_ref, of_ref, ob_ref):
    h = DN_CHUNK
    d = HEAD_DIM
    s = s_ref[...]
    s_b = s.astype(BF16)
    zero = jnp.zeros((d, d), BF16)
    s_bd = jnp.concatenate([jnp.concatenate([s_b[:, 0:d], zero], axis=1),
                            jnp.concatenate([zero, s_b[:, d:2 * d]], axis=1)], axis=0)
    a_sel = jnp.concatenate([a_ref[cf, :, 0:d], a_ref[cb, :, d:2 * d]], axis=1)
    zq = jnp.zeros((h, d), BF16)
    q_rows = jnp.concatenate([jnp.concatenate([q_ref[cf, 0:h, :], zq], axis=1),
                              jnp.concatenate([zq, q_ref[cb, h:2 * h, :]], axis=1)], axis=0)
    r = _dot(jnp.concatenate([a_sel, q_rows], axis=0), s_bd)
    if of_ref is not None:
        of_ref[pl.ds(pl.multiple_of(cf * h, h), h), :] = r[d:d + h, 0:d] + o0_ref[cf, 0:h, :]
        ob_ref[pl.ds(pl.multiple_of(cb * h, h), h), :] = r[d + h:d + 2 * h, d:2 * d] + o0_ref[cb, h:2 * h, :]
    b_sel = jnp.concatenate([b_ref[cf, :, 0:d], b_ref[cb, :, d:2 * d]], axis=1)
    a = jnp.concatenate([jnp.broadcast_to(ar_ref[cf, 0:1, :], (d, d)),
                         jnp.broadcast_to(ar_ref[cb, 4:5, :], (d, d))], axis=1)
    s_ref[...] = s * a + r[0:d, :] + b_sel


def _dn_kernel(n_lat, n_ctx,
               qx_ref, kx_ref, vx_ref, qc_ref, kc_ref, vc_ref, wq_ref, wk_ref, wv_ref,
               gbx_ref, gbc_ref, z_ref, ng_ref, y_ref,
               pad_ref, q_s, k_s, v_s, qc_s, kc_s, vc_s,
               a_x, b_x, qp_x, o0_x, ar_x, a_c, b_c, qp_c, o0_c, ar_c,
               s_ref, of_ref, ob_ref):
    nc_x = n_lat // DN_CHUNK
    nc_c = n_ctx // DN_CHUNK

    _conv_silu(qc_ref, wq_ref, pad_ref, qc_s, n_ctx, True)
    _conv_silu(kc_ref, wk_ref, pad_ref, kc_s, n_ctx, True)
    _conv_silu(vc_ref, wv_ref, pad_ref, vc_s, n_ctx, False)
    _conv_silu(qx_ref, wq_ref, pad_ref, q_s, n_lat, True)
    _conv_silu(kx_ref, wk_ref, pad_ref, k_s, n_lat, True)
    _conv_silu(vx_ref, wv_ref, pad_ref, v_s, n_lat, False)

    grp_c = math.gcd(nc_c, DN_PREP_GROUP)
    grp_x = math.gcd(nc_x, DN_PREP_GROUP)
    assert grp_c % 2 == 0 and grp_x % 2 == 0, "a group's tokens must cover whole 128-lane tiles"

    def prep_c(i, carry):
        _dn_prep_group(i * grp_c, grp_c, (qc_s, kc_s, vc_s, gbc_ref), (a_c, b_c, qp_c, o0_c, ar_c))
        return carry

    def prep_x(i, carry):
        _dn_prep_group(i * grp_x, grp_x, (q_s, k_s, v_s, gbx_ref), (a_x, b_x, qp_x, o0_x, ar_x))
        return carry

    lax.fori_loop(0, nc_c // grp_c, prep_c, 0)
    lax.fori_loop(0, nc_x // grp_x, prep_x, 0)

    s_ref[...] = jnp.zeros(s_ref.shape, F32)

    def scan_c(i, carry):
        _dn_scan_step(i, nc_c - 1 - i, a_c, b_c, qp_c, o0_c, ar_c, s_ref, None, None)
        return carry

    def scan_x(i, carry):
        _dn_scan_step(i, nc_x - 1 - i, a_x, b_x, qp_x, o0_x, ar_x, s_ref, of_ref, ob_ref)
        return carry

    lax.fori_loop(0, nc_c, scan_c, 0)
    lax.fori_loop(0, nc_x, scan_x, 0)

    rb = min(CONV_ROWS, n_lat)
    for r0 in range(0, n_lat, rb):
        o = of_ref[r0:r0 + rb, :] + ob_ref[r0:r0 + rb, :]
        ms = jnp.mean(o * o, axis=-1, keepdims=True)
        y = (o * lax.rsqrt(ms + EPS)) * ng_ref[...] * _silu(z_ref[0, r0:r0 + rb, :].astype(F32))
        y_ref[0, r0:r0 + rb, :] = y.astype(y_ref.dtype)


def _deltanet(qkv_x, qkv_c, conv_w, gt_x, gt_c, z, norm_g):
    b, n_lat, _ = qkv_x.shape
    n_ctx = qkv_c.shape[1]
    gbx = gt_x.reshape(b, 2, 2, N_HEADS, 1, n_lat)
    gbc = gt_c.reshape(b, 2, 2, N_HEADS, 1, n_ctx)
    hh = N_HEADS
    d = HEAD_DIM
    nc_x, nc_c = n_lat // DN_CHUNK, n_ctx // DN_CHUNK
    tok = lambda n, off: pl.BlockSpec((1, n, d), lambda i, j: (i, 0, off + j))
    cw = lambda off: pl.BlockSpec((DN_CONV, d), lambda i, j: (0, off + j))
    gspec = lambda n: pl.BlockSpec((1, 2, 2, 1, 1, n), lambda i, j: (i, 0, 0, j, 0, 0))
    chunk_scratch = lambda nc: [
        pltpu.VMEM((nc, d, 2 * d), BF16),
        pltpu.VMEM((nc, d, 2 * d), F32),
        pltpu.VMEM((nc, 2 * DN_CHUNK, d), BF16),
        pltpu.VMEM((nc, 2 * DN_CHUNK, d), F32),
        pltpu.VMEM((nc, SUBLANES, 2 * DN_CHUNK), F32),
    ]
    return pl.pallas_call(
        functools.partial(_dn_kernel, n_lat, n_ctx),
        grid=(b, hh),
        in_specs=[tok(n_lat, 0), tok(n_lat, hh), tok(n_lat, 2 * hh),
                  tok(n_ctx, 0), tok(n_ctx, hh), tok(n_ctx, 2 * hh),
                  cw(0), cw(hh), cw(2 * hh),
                  gspec(n_lat), gspec(n_ctx),
                  tok(n_lat, 0), _const_spec((1, d))],
        out_specs=tok(n_lat, 0),
        out_shape=jax.ShapeDtypeStruct((b, n_lat, hh * d), BF16),
        scratch_shapes=[pltpu.VMEM((n_lat + 2 * CONV_PAD, d), F32)]
        + [pltpu.VMEM((n_lat, d), F32)] * 3 + [pltpu.VMEM((n_ctx, d), F32)] * 3
        + chunk_scratch(nc_x) + chunk_scratch(nc_c)
        + [pltpu.VMEM((d, 2 * d), F32), pltpu.VMEM((n_lat, d), F32), pltpu.VMEM((n_lat, d), F32)],
        compiler_params=_cparams(("parallel", "parallel")),
        name="deltanet",
    )(qkv_x, qkv_x, qkv_x, qkv_c, qkv_c, qkv_c, conv_w, conv_w, conv_w, gbx, gbc, z, norm_g)


def _rope(x, cos, sin):
    lane = lax.broadcasted_iota(jnp.int32, x.shape, 1)
    quarter = MLA_ROPE // 4
    rot = jnp.where((lane // quarter) % 2 == 0,
                    -pltpu.roll(x, LANES - quarter, axis=1), pltpu.roll(x, quarter, axis=1))
    return x * cos + rot * sin


def _rms(x, n):
    return lax.rsqrt(jnp.sum(x * x, axis=-1, keepdims=True) / n + EPS)


def _mla_proj_kernel(with_q, cq_ref, small_ref, gqa_ref, gkva_ref, wq_ref, wk_ref, wvt_ref, gq_ref, gk_ref,
                     cos_ref, sin_ref, *out_refs):
    hd = HEAD_DIM
    hh = N_HEADS
    cos, sin = cos_ref[...], sin_ref[...]
    small = small_ref[0]
    ckv = small[:, 0:MLA_KV_RANK]
    kpe = small[:, MLA_KV_RANK:MLA_KV_RANK + LANES]
    lane = lax.broadcasted_iota(jnp.int32, kpe.shape, 1)
    kpe = jnp.where(lane < MLA_ROPE, kpe, 0.0)
    kpe_ssq = jnp.sum(kpe * kpe, axis=-1, keepdims=True)
    ckvn = (ckv * _rms(ckv, MLA_KV_RANK) * gkva_ref[...]).astype(BF16)
    kf = _dot(ckvn, wk_ref[...])
    vt = _dot_nt(wvt_ref[...], ckvn)
    if with_q:
        q_ref, k_ref, v_ref = out_refs
        cq = cq_ref[0].astype(F32)
        cqn = (cq * _rms(cq, MLA_Q_RANK) * gqa_ref[...]).astype(BF16)
        qf = _dot(cqn, wq_ref[...])
        q_scale = MLA_QK_DIM ** -0.5 * math.log2(math.e)
    else:
        k_ref, v_ref = out_refs
    gq, gk = gq_ref[...], gk_ref[...]
    kpe_rot = _rope(kpe * gk[:, hd:2 * hd], cos, sin)
    for h in range(hh):
        if with_q:
            qn = qf[:, h * hd:(h + 1) * hd]
            qp = qf[:, (hh + h) * hd:(hh + h + 1) * hd]
            ssq = jnp.sum(qn * qn + qp * qp, axis=-1, keepdims=True)
            r = lax.rsqrt(ssq / MLA_QK_DIM + EPS)
            q_ref[0, h, :, 0:hd] = (qn * r * gq[:, 0:hd] * q_scale).astype(q_ref.dtype)
            q_ref[0, h, :, hd:2 * hd] = (_rope(qp * r * gq[:, hd:2 * hd], cos, sin) * q_scale).astype(q_ref.dtype)
        kn = kf[:, h * hd:(h + 1) * hd]
        r = lax.rsqrt((jnp.sum(kn * kn, axis=-1, keepdims=True) + kpe_ssq) / MLA_QK_DIM + EPS)
        k_ref[0, h, :, 0:hd] = (kn * r * gk[:, 0:hd]).astype(k_ref.dtype)
        k_ref[0, h, :, hd:2 * hd] = (kpe_rot * r).astype(k_ref.dtype)
        v_ref[0, h] = vt[h * hd:(h + 1) * hd, :].astype(v_ref.dtype)


def _mla_projections(cq, small, gqa, gkva, wq, wk, wvt, gq, gk, cos, sin, with_q, tl):
    b, n, _ = small.shape
    hh, hd = N_HEADS, HEAD_DIM
    row = lambda w: pl.BlockSpec((1, tl, w), lambda i, j: (i, j, 0))
    head = lambda w: pl.BlockSpec((1, hh, tl, w), lambda i, j: (i, 0, j, 0))
    tab = pl.BlockSpec((tl, LANES), lambda i, j: (j, 0))
    out_specs = [head(2 * hd), pl.BlockSpec((1, hh, hd, tl), lambda i, j: (i, 0, 0, j))]
    out_shape = [jax.ShapeDtypeStruct((b, hh, n, 2 * hd), BF16), jax.ShapeDtypeStruct((b, hh, hd, n), BF16)]
    if with_q:
        out_specs = [head(2 * hd)] + out_specs
        out_shape = [jax.ShapeDtypeStruct((b, hh, n, 2 * hd), BF16)] + out_shape
    return pl.pallas_call(
        functools.partial(_mla_proj_kernel, with_q),
        grid=(b, n // tl),
        in_specs=[row(cq.shape[-1]), row(small.shape[-1]), _const_spec(gqa.shape), _const_spec(gkva.shape),
                  _const_spec(wq.shape), _const_spec(wk.shape), _const_spec(wvt.shape), _const_spec(gq.shape),
                  _const_spec(gk.shape),
                  tab, tab],
        out_specs=out_specs,
        out_shape=out_shape,
        compiler_params=_cparams(("parallel", "parallel")),
        name="mla_projections_q" if with_q else "mla_projections_kv",
    )(cq, small, gqa, gkva, wq, wk, wvt, gq, gk, cos, sin)


ATTN_SUB_COLS = 256
ATTN_LOOKAHEAD = 2


def _attn_kernel(q_ref, kc_ref, vct_ref, kx_ref, vxt_ref, o_ref):
    tq = q_ref.shape[2]
    n_sub = tq // ATTN_SUB_COLS

    def scores(i):
        q = q_ref[0, 0, i * ATTN_SUB_COLS:(i + 1) * ATTN_SUB_COLS, :]
        return _dot_nt(kc_ref[0, 0], q), _dot_nt(kx_ref[0, 0], q)

    pending = [scores(i) for i in range(min(ATTN_LOOKAHEAD, n_sub))]
    for i in range(n_sub):
        if i + ATTN_LOOKAHEAD < n_sub:
            pending.append(scores(i + ATTN_LOOKAHEAD))
        s_c, s_x = pending.pop(0)
        m = jnp.maximum(jnp.max(s_c, axis=0, keepdims=True), jnp.max(s_x, axis=0, keepdims=True))
        p_c = jnp.exp2(s_c - m)
        p_x = jnp.exp2(s_x - m)
        l = jnp.sum(p_c, axis=0, keepdims=True) + jnp.sum(p_x, axis=0, keepdims=True)
        o_t = _dot(vct_ref[0, 0], p_c.astype(BF16)) + _dot(vxt_ref[0, 0], p_x.astype(BF16))
        o_ref[0, i * ATTN_SUB_COLS:(i + 1) * ATTN_SUB_COLS, :] = (o_t / l).T.astype(o_ref.dtype)


def _attention(q, k_c, vt_c, k_x, vt_x, tq):
    b, hh, n, dq = q.shape
    n_ctx = k_c.shape[2]
    hd = vt_x.shape[2]
    keys = lambda n_: pl.BlockSpec((1, 1, n_, dq), lambda i, j, t: (i, j, 0, 0))
    vals = lambda n_: pl.BlockSpec((1, 1, hd, n_), lambda i, j, t: (i, j, 0, 0))
    return pl.pallas_call(
        _attn_kernel,
        grid=(b, hh, n // tq),
        in_specs=[pl.BlockSpec((1, 1, tq, dq), lambda i, j, t: (i, j, t, 0)),
                  keys(n_ctx), vals(n_ctx), keys(n), vals(n)],
        out_specs=pl.BlockSpec((1, tq, hd), lambda i, j, t: (i, t, j)),
        out_shape=jax.ShapeDtypeStruct((b, n, hh * hd), BF16),
        compiler_params=_cparams(("parallel", "parallel", "parallel")),
        name="attention",
    )(q, k_c, vt_c, k_x, vt_x)


def _merge_kernel(ya_ref, yb_ref, gate_ref, x_ref, g1_ref, sh2_ref, sc2_ref, n2g_ref,
                  wa_ref, wb_ref, wo_ref, rwt_ref, xmid_ref, h2_ref, aff_ref):
    d = x_ref.shape[-1]
    ga = _sigmoid(gate_ref[0, :, 0:d].astype(F32))
    gb = _sigmoid(gate_ref[0, :, d:2 * d].astype(F32))
    mix = ga * _dot(ya_ref[0], wa_ref[...]) + gb * _dot(yb_ref[0], wb_ref[...])
    mix = _dot(mix.astype(BF16), wo_ref[...])
    xm = x_ref[0] + g1_ref[0] * mix
    xmid_ref[0] = xm
    h2 = _modulated_norm(xm, n2g_ref[...], sh2_ref[0], sc2_ref[0])
    _to_tile_major(h2_ref, h2, lead=(0,))
    logits = _dot_nt(rwt_ref[...], h2, precision=HI)
    mx = jnp.max(logits, axis=0, keepdims=True)
    ex = jnp.exp(logits - mx)
    aff_ref[0] = ex / jnp.sum(ex, axis=0, keepdims=True)


def _merge(ya, yb, gate, x, g1, sh2, sc2, n2g, wa, wb, wo, rwt, tl):
    b, n, d = x.shape
    ne = rwt.shape[0]
    row = lambda w: pl.BlockSpec((1, tl, w), lambda i, j: (i, j, 0))
    vec = pl.BlockSpec((1, 1, d), lambda i, j: (i, 0, 0))
    return pl.pallas_call(
        _merge_kernel,
        grid=(b, n // tl),
        in_specs=[row(d), row(d), row(2 * d), row(d), vec, vec, vec, _const_spec((1, d)),
                  _const_spec(wa.shape), _const_spec(wb.shape), _const_spec(wo.shape), _const_spec(rwt.shape)],
        out_specs=[row(d), pl.BlockSpec((1, tl * SUBLANES, LANES), lambda i, j: (i, j, 0)),
                   pl.BlockSpec((1, ne, tl), lambda i, j: (i, 0, j))],
        out_shape=[jax.ShapeDtypeStruct((b, n, d), F32), jax.ShapeDtypeStruct((b, n * SUBLANES, LANES), F32),
                   jax.ShapeDtypeStruct((b, ne, n), F32)],
        compiler_params=_cparams(("parallel", "parallel")),
        name="merge_router",
    )(ya, yb, gate, x, g1, sh2, sc2, n2g, wa, wb, wo, rwt)


TOPK_RANK_ROWS = 64


def _lane_cumsum(x_bf16, tri_bf16, out_ref):
    ne, n = x_bf16.shape
    carry = jnp.zeros((ne, 1), F32)
    for j in range(n // LANES):
        blk = _dot(x_bf16[:, j * LANES:(j + 1) * LANES], tri_bf16) + carry
        out_ref[:, j * LANES:(j + 1) * LANES] = blk
        carry = blk[:, LANES - 1:LANES]


def _topk_kernel(cap, aff_ref, idx_ref, gate_ref, cum_ref, affsel_ref, cumrep_ref, affrep_ref):
    aff = aff_ref[0]
    ne, n = aff.shape
    bits = pltpu.bitcast(aff, jnp.int32)

    def search(i, t):
        cand = t | (jnp.int32(1) << (30 - i))
        cnt = jnp.sum(jnp.where(bits >= cand, 1.0, 0.0), axis=-1, keepdims=True)
        return jnp.where(cnt >= cap, cand, t)

    thr = lax.fori_loop(0, 31, search, jnp.zeros((ne, 1), jnp.int32))
    gt = bits > thr
    eq = bits == thr
    need = cap - jnp.sum(jnp.where(gt, 1.0, 0.0), axis=-1, keepdims=True)
    ri = lax.broadcasted_iota(jnp.int32, (LANES, LANES), 0)
    ci = lax.broadcasted_iota(jnp.int32, (LANES, LANES), 1)
    tri = jnp.where(ri <= ci, 1.0, 0.0).astype(BF16)
    _lane_cumsum(jnp.where(eq, 1.0, 0.0).astype(BF16), tri, cum_ref)
    sel = gt | (eq & (cum_ref[...] <= need))
    affsel_ref[...] = jnp.where(sel, aff, 0.0)
    _lane_cumsum(jnp.where(sel, 1.0, 0.0).astype(BF16), tri, cum_ref)
    for ex in range(ne):
        cumrep_ref[ex] = jnp.broadcast_to(cum_ref[ex:ex + 1, :], (SUBLANES, n))
        affrep_ref[ex] = jnp.broadcast_to(affsel_ref[ex:ex + 1, :], (SUBLANES, n))

    rows = TOPK_RANK_ROWS
    lane = lax.broadcasted_iota(jnp.int32, (rows, LANES), 1)
    idx_ref[0] = jnp.zeros((cap, LANES), jnp.int32)
    gate_ref[0] = jnp.zeros((cap, LANES), F32)

    def per_expert(e, carry):
        for r0 in range(0, cap, rows):
            rank = (lax.broadcasted_iota(jnp.int32, (rows, LANES), 0) + r0).astype(F32)
            cnt = jnp.zeros((rows, LANES), F32)
            gat = jnp.zeros((rows, LANES), F32)
            for j in range(n // LANES):
                c_blk = cumrep_ref[e, 0:1, j * LANES:(j + 1) * LANES]
                a_blk = affrep_ref[e, 0:1, j * LANES:(j + 1) * LANES]
                cnt = cnt + jnp.where(c_blk <= rank, 1.0, 0.0)
                gat = gat + jnp.where(c_blk == rank + 1.0, a_blk, 0.0)
            pos = jnp.sum(cnt, axis=-1, keepdims=True).astype(jnp.int32)
            g = jnp.sum(gat, axis=-1, keepdims=True)
            idx_ref[0, r0:r0 + rows, :] = jnp.where(lane == e, pos, idx_ref[0, r0:r0 + rows, :])
            gate_ref[0, r0:r0 + rows, :] = jnp.where(lane == e, g, gate_ref[0, r0:r0 + rows, :])
        return carry

    lax.fori_loop(0, ne, per_expert, 0)


def _expert_topk(aff_t, cap):
    b, ne, n = aff_t.shape
    return pl.pallas_call(
        functools.partial(_topk_kernel, cap),
        grid=(b,),
        in_specs=[pl.BlockSpec((1, ne, n), lambda i: (i, 0, 0))],
        out_specs=[pl.BlockSpec((1, cap, LANES), lambda i: (i, 0, 0))] * 2,
        out_shape=[jax.ShapeDtypeStruct((b, cap, LANES), jnp.int32), jax.ShapeDtypeStruct((b, cap, LANES), F32)],
        scratch_shapes=[pltpu.VMEM((ne, n), F32), pltpu.VMEM((ne, n), F32),
                        pltpu.VMEM((ne, SUBLANES, n), F32), pltpu.VMEM((ne, SUBLANES, n), F32)],
        compiler_params=_cparams(("parallel",)),
        name="expert_topk",
    )(aff_t)


MOE_FF_TILE = 512
MOE_UNROLL = 8


def _moe_kernel(cap, idx_ref, gate_ref, h_ref, wg_ref, wu_ref, wd_ref, o_ref, xt_ref, x_ref, y_ref, yt_ref):
    e = pl.program_id(1)
    f = pl.program_id(2)
    nf = pl.num_programs(2)

    @pl.when((e == 0) & (f == 0))
    def _():
        o_ref[...] = jnp.zeros(o_ref.shape, o_ref.dtype)

    @pl.when(f == 0)
    def _():
        def gather(i, carry):
            for u in range(MOE_UNROLL):
                r = i * MOE_UNROLL + u
                xt_ref[_tile(r), :] = h_ref[0, _tile(idx_ref[0, 0, 0, r]), :]
            return carry
        lax.fori_loop(0, cap // MOE_UNROLL, gather, 0)
        x_ref[...] = _from_tile_major(xt_ref, cap).astype(x_ref.dtype)

    x = x_ref[...]
    hid = (_silu(_dot(x, wg_ref[0])) * _dot(x, wu_ref[0])).astype(BF16)
    part = _dot(hid, wd_ref[0])

    @pl.when(f == 0)
    def _():
        y_ref[...] = part

    @pl.when(f > 0)
    def _():
        y_ref[...] += part

    @pl.when(f == nf - 1)
    def _():
        _to_tile_major(yt_ref, y_ref[...])

        def scatter(i, carry):
            rows = [i * MOE_UNROLL + u for u in range(MOE_UNROLL)]
            toks = [idx_ref[0, 0, 0, r] for r in rows]
            new = [o_ref[0, _tile(t), :] + gate_ref[0, 0, 0, r] * yt_ref[_tile(r), :] for r, t in zip(rows, toks)]
            for t, val in zip(toks, new):
                o_ref[0, _tile(t), :] = val
            return carry
        lax.fori_loop(0, cap // MOE_UNROLL, scatter, 0)


def _moe(idx, gate, h2t, wg, wu, wd):
    b, rows, _ = h2t.shape
    ne, d, ffp = wg.shape
    cap = idx.shape[-1]
    tf = MOE_FF_TILE
    smem = lambda: pl.BlockSpec((1, 1, 1, cap), lambda i, j, k: (i, j, 0, 0), memory_space=pltpu.SMEM)
    act = lambda: pl.BlockSpec((1, rows, LANES), lambda i, j, k: (i, 0, 0), pipeline_mode=pl.Buffered(1))
    return pl.pallas_call(
        functools.partial(_moe_kernel, cap),
        grid=(b, ne, ffp // tf),
        in_specs=[smem(), smem(), act(),
                  pl.BlockSpec((1, d, tf), lambda i, j, k: (j, 0, k)),
                  pl.BlockSpec((1, d, tf), lambda i, j, k: (j, 0, k)),
                  pl.BlockSpec((1, tf, d), lambda i, j, k: (j, k, 0))],
        out_specs=act(),
        out_shape=jax.ShapeDtypeStruct((b, rows, LANES), F32),
        scratch_shapes=[pltpu.VMEM((cap * SUBLANES, LANES), F32), pltpu.VMEM((cap, d), BF16),
                        pltpu.VMEM((cap, d), F32), pltpu.VMEM((cap * SUBLANES, LANES), F32)],
        compiler_params=_cparams(("parallel", "arbitrary", "arbitrary")),
        name="expert_ffn",
    )(idx[:, :, None, :], gate[:, :, None, :], h2t, wg, wu, wd)


def _final_kernel(xm_ref, moe_ref, g_ref, o_ref):
    o_ref[0] = xm_ref[0] + g_ref[0] * _from_tile_major(moe_ref, xm_ref.shape[1], lead=(0,))


def _final(xmid, moe_t, g2, tl):
    b, n, d = xmid.shape
    row = pl.BlockSpec((1, tl, d), lambda i, j: (i, j, 0))
    return pl.pallas_call(
        _final_kernel,
        grid=(b, n // tl),
        in_specs=[row, pl.BlockSpec((1, tl * SUBLANES, LANES), lambda i, j: (i, j, 0)),
                  pl.BlockSpec((1, 1, d), lambda i, j: (i, 0, 0))],
        out_specs=row,
        out_shape=jax.ShapeDtypeStruct((b, n, d), F32),
        compiler_params=_cparams(("parallel", "parallel")),
        name="final_residual",
    )(xmid, moe_t, g2)


def _rope_tables(n_lat):
    rows = n_lat // GRID_W
    row = jnp.repeat(jnp.arange(rows), GRID_W).astype(F32)
    col = jnp.broadcast_to(jnp.arange(GRID_W), (rows, GRID_W)).reshape(-1).astype(F32)
    n_freq = MLA_ROPE // 4
    inv_freq = ROPE_BASE ** (-jnp.arange(n_freq, dtype=F32) / n_freq)
    ang_r = row[:, None] * inv_freq
    ang_c = col[:, None] * inv_freq
    ang = jnp.concatenate([ang_r, ang_r, ang_c, ang_c], axis=-1)
    zeros = jnp.zeros((n_lat, LANES - MLA_ROPE), F32)
    return jnp.concatenate([jnp.cos(ang), zeros], axis=-1), jnp.concatenate([jnp.sin(ang), zeros], axis=-1)


def _pad_lanes(v, width):
    return jnp.concatenate([v, jnp.zeros(v.shape[:-1] + (width - v.shape[-1],), v.dtype)], axis=-1)


def kernel(x, c, ctx, c_ctx, ada_w, ada_b, norm1_g, norm2_g, w_in, conv_w, a_log, dt_bias, dn_norm_g, w_out_a, q_a_norm_g, w_uq, kv_a_norm_g, w_ukv, q_norm_g, k_norm_g, w_out_b, w_o, router_w, w_gate, w_up, w_down):
    assert ada_w.shape[0] == 1, "single-layer block"
    b, n_lat, d = x.shape
    n_ctx = ctx.shape[1]
    hh, hd = N_HEADS, HEAD_DIM
    qkv_dim = 3 * hh * hd
    nb = 2 * hh

    n_cond = -(-(b + 1) // SUBLANES) * SUBLANES
    cond = jnp.concatenate([c, c_ctx[None], jnp.zeros((n_cond - b - 1, d), F32)], axis=0)
    mod = _modulation(cond, ada_w[0], ada_b[0])
    mods = [mod[:, i * d:(i + 1) * d] for i in range(6)]
    lat = lambda m: m[:b, None, :]
    cvec = lambda m: jnp.broadcast_to(m[b][None, None, :], (b, 1, d))

    o_z = qkv_dim
    o_alpha = o_z + hh * hd
    o_beta = o_alpha + nb
    o_cq = o_beta + nb
    o_ckv = o_cq + MLA_Q_RANK
    o_kr = o_ckv + MLA_KV_RANK
    o_gate = o_kr + MLA_ROPE
    w = w_in[0]
    w_big = jnp.concatenate([w[:, :o_alpha], w[:, o_gate:], w[:, o_cq:o_ckv]], axis=1).astype(BF16)
    small_w = _pad_lanes(w[:, o_ckv:o_gate], MLA_KV_RANK + LANES).astype(BF16)
    w_ab_t = jnp.transpose(w[:, o_alpha:o_cq]).astype(BF16)
    gate_pad = jnp.zeros((nb, 1), F32)
    alog = jnp.concatenate([a_log[0].reshape(nb, 1), gate_pad], axis=0)
    dtb = jnp.concatenate([dt_bias[0].reshape(nb, 1), gate_pad], axis=0)
    widths = (qkv_dim, hh * hd, N_BRANCHES * d, MLA_Q_RANK)
    g1 = norm1_g[0][None]
    qkv_x, z_x, gt_x, cq_x, sm_x, dg_x = _in_projection(x, lat(mods[0]), lat(mods[1]), g1, w_big, small_w, w_ab_t,
                                                        alog, dtb, widths, 512)
    qkv_c, _, _, cq_c, sm_c, dg_c = _in_projection(ctx, cvec(mods[0]), cvec(mods[1]), g1, w_big, small_w, w_ab_t,
                                                   alog, dtb, widths, min(n_ctx, 512))

    y_a = _deltanet(qkv_x, qkv_c, conv_w[0], dg_x, dg_c, z_x, dn_norm_g[0][None])

    wq = w_uq[0].reshape(MLA_Q_RANK, hh, MLA_QK_DIM)
    wq = jnp.concatenate([wq[:, :, :hd].reshape(MLA_Q_RANK, hh * hd),
                          _pad_lanes(wq[:, :, hd:], hd).reshape(MLA_Q_RANK, hh * hd)], axis=1).astype(BF16)
    wkv = w_ukv[0].reshape(MLA_KV_RANK, hh, 2 * hd)
    wk = wkv[:, :, :hd].reshape(MLA_KV_RANK, hh * hd).astype(BF16)
    wvt = jnp.transpose(wkv[:, :, hd:].reshape(MLA_KV_RANK, hh * hd)).astype(BF16)
    gq = _pad_lanes(q_norm_g[0], 2 * hd)[None]
    gk = _pad_lanes(k_norm_g[0], 2 * hd)[None]
    gqa, gkva = q_a_norm_g[0][None], kv_a_norm_g[0][None]
    cos, sin = _rope_tables(n_lat)
    cos_c = _pad_lanes(jnp.ones((n_ctx, MLA_ROPE), F32), LANES)
    sin_c = jnp.zeros((n_ctx, LANES), F32)
    q_x, k_x, v_x = _mla_projections(cq_x, sm_x, gqa, gkva, wq, wk, wvt, gq, gk, cos, sin, True, 512)
    k_c, v_c = _mla_projections(cq_c, sm_c, gqa, gkva, wq, wk, wvt, gq, gk, cos_c, sin_c, False, min(n_ctx, 512))
    y_b = _attention(q_x, k_c, v_c, k_x, v_x, min(n_lat, 1024))

    rwt = jnp.transpose(router_w[0])
    x_mid, h2, aff_t = _merge(y_a, y_b, gt_x, x, lat(mods[2]), lat(mods[3]), lat(mods[4]), norm2_g[0][None],
                              w_out_a[0].astype(BF16), w_out_b[0].astype(BF16), w_o[0].astype(BF16), rwt, 512)

    cap = EC_CAPACITY * n_lat // N_EXPERTS
    idx, gate = _expert_topk(aff_t, cap)
    idx = jnp.transpose(idx[:, :, :N_EXPERTS], (0, 2, 1))
    gate = jnp.transpose(gate[:, :, :N_EXPERTS], (0, 2, 1))
    ff = w_gate.shape[-1]
    ffp = -(-ff // MOE_FF_TILE) * MOE_FF_TILE
    wg = _pad_lanes(w_gate[0], ffp).astype(BF16)
    wu = _pad_lanes(w_up[0], ffp).astype(BF16)
    wd = jnp.concatenate([w_down[0], jnp.zeros((N_EXPERTS, ffp - ff, d), F32)], axis=1).astype(BF16)
    moe = _moe(idx, gate, h2, wg, wu, wd)
    return _final(x_mid, moe, lat(mods[5]), 512)
```

```python
import functools
import math

import jax
import jax.numpy as jnp
from jax import lax
from jax.experimental import pallas as pl
from jax.experimental.pallas import tpu as pltpu

F32 = jnp.float32
BF16 = jnp.bfloat16
F8 = jnp.float8_e4m3fn
EPS = 1e-6

N_HEADS = 8
HEAD_DIM = 128
DN_CONV = 5
DN_CHUNK = 64
MLA_Q_RANK = 384
MLA_KV_RANK = 256
MLA_ROPE = 64
MLA_QK_DIM = HEAD_DIM + MLA_ROPE
GRID_W = 64
ROPE_BASE = 10000.0
N_EXPERTS = 16
EC_CAPACITY = 2
N_BRANCHES = 2

LANES = 128
SUBLANES = 8
VMEM_LIMIT_BYTES = 56 * 1024 * 1024

HI = lax.Precision.HIGHEST


def _cparams(sem):
    return pltpu.CompilerParams(dimension_semantics=sem, vmem_limit_bytes=VMEM_LIMIT_BYTES)


def _dot(a, b, precision=None):
    return jnp.dot(a, b, preferred_element_type=F32, precision=precision)


def _dot_nt(a, b, precision=None):
    return lax.dot_general(a, b, (((1,), (1,)), ((), ())), preferred_element_type=F32, precision=precision)


def _dot_tn(a, b):
    return lax.dot_general(a, b, (((0,), (0,)), ((), ())), preferred_element_type=F32)


def _sigmoid(x):
    return 1.0 / (1.0 + jnp.exp(-x))


def _silu(x):
    return x * _sigmoid(x)


def _const_spec(shape):
    nd = len(shape)
    return pl.BlockSpec(shape, lambda *_: (0,) * nd, pipeline_mode=pl.Buffered(1))


def _to_tile_major(ref, x, lead=()):
    rows = x.shape[0]
    for s_ in range(SUBLANES):
        ref[lead + (pl.ds(s_, rows, stride=SUBLANES), slice(None))] = x[:, s_ * LANES:(s_ + 1) * LANES]


def _from_tile_major(ref, rows, lead=()):
    return jnp.concatenate([ref[lead + (pl.ds(s_, rows, stride=SUBLANES), slice(None))] for s_ in range(SUBLANES)],
                           axis=1)


def _tile(i):
    return pl.ds(pl.multiple_of(i * SUBLANES, SUBLANES), SUBLANES)


def _mod_kernel(c_ref, w_ref, b_ref, o_ref):
    c = c_ref[...]
    o_ref[...] = _dot(_silu(c), w_ref[...], precision=HI) + b_ref[...]


def _modulation(cond, ada_w, ada_b):
    n, d = cond.shape
    n_out = ada_w.shape[1]
    tn = d
    return pl.pallas_call(
        _mod_kernel,
        grid=(n_out // tn,),
        in_specs=[pl.BlockSpec((n, d), lambda j: (0, 0)),
                  pl.BlockSpec((d, tn), lambda j: (0, j)),
                  pl.BlockSpec((1, tn), lambda j: (0, j))],
        out_specs=pl.BlockSpec((n, tn), lambda j: (0, j)),
        out_shape=jax.ShapeDtypeStruct((n, n_out), F32),
        compiler_params=_cparams(("parallel",)),
        name="modulation",
    )(cond, ada_w, ada_b.reshape(1, n_out))


def _modulated_norm(x, g, shift, scale):
    ms = jnp.mean(x * x, axis=-1, keepdims=True)
    return (x * lax.rsqrt(ms + EPS)) * g * (1.0 + scale) + shift


def _inproj_kernel(widths, x_ref, shift_ref, scale_ref, g_ref, wb_ref, ws_ref, wab_ref, alog_ref, dtb_ref,
                   *out_refs):
    big_refs, small_ref, gates_ref = out_refs[:-2], out_refs[-2], out_refs[-1]
    h = _modulated_norm(x_ref[0], g_ref[...], shift_ref[0], scale_ref[0]).astype(BF16)
    off = 0
    for o_ref, width in zip(big_refs, widths):
        for c0 in range(0, width, 1024):
            c1 = min(c0 + 1024, width)
            o_ref[0, :, c0:c1] = _dot(h, wb_ref[:, off + c0:off + c1]).astype(o_ref.dtype)
        off += width
    small_ref[0] = _dot(h, ws_ref[...])
    ab = _dot_nt(wab_ref[...], h)
    row_id = lax.broadcasted_iota(jnp.int32, ab.shape, 0)
    xs = ab + dtb_ref[...]
    softplus = jnp.maximum(xs, 0.0) + jnp.log1p(jnp.exp(-jnp.abs(xs)))
    gates_ref[0] = jnp.where(row_id < 2 * N_HEADS, -jnp.exp(alog_ref[...]) * softplus, _sigmoid(ab))


def _in_projection(x, shift, scale, g, w_big, w_small, w_ab_t, alog, dtb, widths, tl):
    b, n, d = x.shape
    row = lambda w: pl.BlockSpec((1, tl, w), lambda i, j: (i, j, 0))
    vec = pl.BlockSpec((1, 1, d), lambda i, j: (i, 0, 0))
    n_gate = w_ab_t.shape[0]
    outs = [jax.ShapeDtypeStruct((b, n, w), BF16) for w in widths]
    outs.append(jax.ShapeDtypeStruct((b, n, w_small.shape[1]), F32))
    outs.append(jax.ShapeDtypeStruct((b, n_gate, n), F32))
    return pl.pallas_call(
        functools.partial(_inproj_kernel, widths),
        grid=(b, n // tl),
        in_specs=[row(d), vec, vec, _const_spec((1, d)), _const_spec(w_big.shape), _const_spec(w_small.shape),
                  _const_spec(w_ab_t.shape), _const_spec(alog.shape), _const_spec(dtb.shape)],
        out_specs=[row(w) for w in widths] + [row(w_small.shape[1]),
                                              pl.BlockSpec((1, n_gate, tl), lambda i, j: (i, 0, j))],
        out_shape=outs,
        compiler_params=_cparams(("parallel", "parallel")),
        name="in_projection",
    )(x, shift, scale, g, w_big, w_small, w_ab_t, alog, dtb)


CONV_ROWS = 512
CONV_PAD = 8


def _conv_fill(raw_ref, pad_ref, n_tok):
    zeros = jnp.zeros((CONV_PAD, LANES), F32)
    pad_ref[0:CONV_PAD, :] = zeros
    pad_ref[CONV_PAD + n_tok:2 * CONV_PAD + n_tok, :] = zeros
    pad_ref[CONV_PAD:CONV_PAD + n_tok, :] = raw_ref[0].astype(F32)


def _conv_block(pad_ref, w, r0, rows, normalize):
    half = DN_CONV // 2
    acc = None
    for j in range(DN_CONV):
        term = pad_ref[pl.ds(r0 + (CONV_PAD + j - half), rows), :] * w[j:j + 1, :]
        acc = term if acc is None else acc + term
    y = _silu(acc)
    if normalize:
        y = y * lax.rsqrt(jnp.sum(y * y, axis=-1, keepdims=True) + EPS)
    return y


def _conv_silu(raw_ref, w_ref, pad_ref, out_ref, n_tok, normalize):
    _conv_fill(raw_ref, pad_ref, n_tok)
    w = w_ref[...]
    rb = min(CONV_ROWS, n_tok)
    for r0 in range(0, n_tok, rb):
        out_ref[r0:r0 + rb, :] = _conv_block(pad_ref, w, r0, rb, normalize)


def _dn_masks():
    n = 2 * DN_CHUNK
    ri = lax.broadcasted_iota(jnp.int32, (n, n), 0)
    ci = lax.broadcasted_iota(jnp.int32, (n, n), 1)
    top = ri < DN_CHUNK
    same = (ri // DN_CHUNK) == (ci // DN_CHUNK)
    sgn = jnp.where(top, 1, -1)
    delta = (ri - ci) * sgn
    incl = same & (delta >= 0)
    strict = same & (delta > 0)
    incl_t = same & (delta <= 0)
    eye = ri == ci
    return top, incl, strict, incl_t, eye


DN_INV_BLOCK = 8
DN_PREP_GROUP = 8


def _hi_lo(a_f32):
    hi = a_f32.astype(BF16)
    return hi, (a_f32 - hi.astype(F32)).astype(BF16)


def _split_dot(a_f32, b_bf16):
    hi, lo = _hi_lo(a_f32)
    return _dot(hi, b_bf16) + _dot(lo, b_bf16)


def _bdot(a_f32, b_f32):
    return _dot(a_f32.astype(BF16), b_f32.astype(BF16))


def _dn_gate_rows(c0, group, gt_ref):
    n = 2 * DN_CHUNK
    t0 = pl.multiple_of(c0 * DN_CHUNK, n)
    lane = lax.broadcasted_iota(jnp.int32, (1, n), 1)
    rows = [[gt_ref[0, kind, dr, 0, :, pl.ds(t0, group * DN_CHUNK)] for dr in range(2)] for kind in range(2)]
    out = []
    for j in range(group):
        p, odd = divmod(j, 2)
        both = []
        for kind in range(2):
            fwd = rows[kind][0][:, p * n:(p + 1) * n]
            bwd = rows[kind][1][:, p * n:(p + 1) * n]
            if odd:
                both.append(jnp.where(lane < DN_CHUNK, pltpu.roll(fwd, DN_CHUNK, axis=1), bwd))
            else:
                both.append(jnp.where(lane < DN_CHUNK, fwd, pltpu.roll(bwd, DN_CHUNK, axis=1)))
        out.append(jnp.concatenate(both, axis=0))
    return out


def _dn_prep_load(c, q_ref, k_ref, v_ref):
    r0 = pl.multiple_of(c * DN_CHUNK, DN_CHUNK)
    return q_ref[pl.ds(r0, DN_CHUNK), :], k_ref[pl.ds(r0, DN_CHUNK), :], v_ref[pl.ds(r0, DN_CHUNK), :]


def _dn_prep_compute(q, k, v, gb):
    n = 2 * DN_CHUNK
    top, incl, strict, incl_t, eye = _dn_masks()
    g_row = jnp.broadcast_to(gb[0:1, :], (SUBLANES, n))
    beta_rows = jnp.broadcast_to(gb[1:2, :], (n, n))
    kst = jnp.concatenate([k, k], axis=0)
    qst = jnp.concatenate([q, q], axis=0)
    vst = jnp.concatenate([v, v], axis=0)
    kst_b = kst.astype(BF16)

    gc_row = _split_dot(g_row, jnp.where(incl_t, 1.0, 0.0).astype(BF16))[0:1, :]
    kk = _dot_nt(kst_b, kst_b)
    qk = _dot_nt(qst.astype(BF16), kst_b)
    yield
    c2 = jnp.broadcast_to(gc_row, (n, n))
    c1 = c2.T
    beta_c = beta_rows.T
    tot = jnp.where(top, c1[DN_CHUNK - 1:DN_CHUNK, :], c1[DN_CHUNK:DN_CHUNK + 1, :])
    decay = jnp.exp(jnp.where(incl, c1 - c2, -jnp.inf))
    e_gc = jnp.exp(c1)
    e_rest = jnp.exp(tot - c1)
    scale = HEAD_DIM ** -0.5

    m = jnp.where(strict, kk * beta_c * decay, 0.0)
    ri = lax.broadcasted_iota(jnp.int32, (n, n), 0)
    ci = lax.broadcasted_iota(jnp.int32, (n, n), 1)
    blk = DN_INV_BLOCK
    dg = jnp.where((ri // blk) == (ci // blk), m, 0.0)
    t = jnp.where(eye, 1.0, 0.0) - dg
    dp = _bdot(dg, dg)
    yield
    for it in range(blk.bit_length() - 2):
        t = t + _bdot(t, dp)
        if it + 1 < blk.bit_length() - 2:
            dp = _bdot(dp, dp)
        yield
    while blk < DN_CHUNK:
        off = jnp.where(((ri // (2 * blk)) == (ci // (2 * blk))) & ((ri // blk) != (ci // blk)), m, 0.0)
        ta = _bdot(t, off)
        yield
        t = t - _bdot(ta, t)
        yield
        blk *= 2

    kb = kst * beta_c
    rhs = jnp.concatenate([vst * beta_c, kb * e_gc], axis=1)
    uw = _bdot(t, rhs)
    yield
    d = HEAD_DIM
    u_b = uw[:, 0:d].astype(BF16)
    w_b = uw[:, d:2 * d].astype(BF16)
    kd = (kst * e_rest).astype(BF16)
    qkm = (jnp.where(incl, qk * decay, 0.0) * scale).astype(BF16)
    ri2 = lax.broadcasted_iota(jnp.int32, (n, 2 * d), 0)
    ci2 = lax.broadcasted_iota(jnp.int32, (n, 2 * d), 1)
    own_dir = (ri2 // DN_CHUNK) == (ci2 // d)
    zero = jnp.zeros((), BF16)
    wu_bd = jnp.concatenate([jnp.where(own_dir, jnp.concatenate([w_b, w_b], axis=1), zero),
                             jnp.where(own_dir, jnp.concatenate([u_b, u_b], axis=1), zero)], axis=1)
    kw = _dot_tn(kd, wu_bd)
    qw = _dot(qkm, jnp.concatenate([w_b, u_b], axis=1))
    yield
    return ((-kw[:, 0:2 * d]).astype(BF16),
            kw[:, 2 * d:4 * d],
            (qst * e_gc * scale - qw[:, 0:d]).astype(BF16),
            qw[:, d:2 * d],
            jnp.exp(tot)[DN_CHUNK - 4:DN_CHUNK + 4, :])


def _dn_prep_group(c0, group, in_refs, out_refs):
    *qkv_refs, gt_ref = in_refs
    gates = _dn_gate_rows(c0, group, gt_ref)
    gens = [_dn_prep_compute(*_dn_prep_load(c0 + j, *qkv_refs), gates[j]) for j in range(group)]
    results = [None] * group
    while any(r is None for r in results):
        for j, gen in enumerate(gens):
            try:
                next(gen)
            except StopIteration as done:
                results[j] = done.value
    for j, res in enumerate(results):
        for ref, val in zip(out_refs, res):
            ref[c0 + j] = val


def _dn_scan_step(cf, cb, a_ref, b_ref, q_ref, o0_ref, ar_ref, s_ref, of_ref, ob_ref):
    h = DN_CHUNK
    d = HEAD_DIM
    s = s_ref[...]
    s_b = s.astype(BF16)
    zero = jnp.zeros((d, d), BF16)
    s_bd = jnp.concatenate([jnp.concatenate([s_b[:, 0:d], zero], axis=1),
                            jnp.concatenate([zero, s_b[:, d:2 * d]], axis=1)], axis=0)
    a_sel = jnp.concatenate([a_ref[cf, :, 0:d], a_ref[cb, :, d:2 * d]], axis=1)
    zq = jnp.zeros((h, d), BF16)
    q_rows = jnp.concatenate([jnp.concatenate([q_ref[cf, 0:h, :], zq], axis=1),
                              jnp.concatenate([zq, q_ref[cb, h:2 * h, :]], axis=1)], axis=0)
    r = _dot(jnp.concatenate([a_sel, q_rows], axis=0), s_bd)
    if of_ref is not None:
        of_ref[pl.ds(pl.multiple_of(cf * h, h), h), :] = r[d:d + h, 0:d] + o0_ref[cf, 0:h, :]
        ob_ref[pl.ds(pl.multiple_of(cb * h, h), h), :] = r[d + h:d + 2 * h, d:2 * d] + o0_ref[cb, h:2 * h, :]
    b_sel = jnp.concatenate([b_ref[cf, :, 0:d], b_ref[cb, :, d:2 * d]], axis=1)
    a = jnp.concatenate([jnp.broadcast_to(ar_ref[cf, 0:1, :], (d, d)),
                         jnp.broadcast_to(ar_ref[cb, 4:5, :], (d, d))], axis=1)
    s_ref[...] = s * a + r[0:d, :] + b_sel


def _dn_kernel(n_lat, n_ctx,
               qx_ref, kx_ref, vx_ref, qc_ref, kc_ref, vc_ref, wq_ref, wk_ref, wv_ref,
               gbx_ref, gbc_ref, z_ref, ng_ref, y_ref,
               pad_ref, q_s, k_s, v_s, qc_s, kc_s, vc_s,
               a_x, b_x, qp_x, o0_x, ar_x, a_c, b_c, qp_c, o0_c, ar_c,
               s_ref, of_ref, ob_ref):
    nc_x = n_lat // DN_CHUNK
    nc_c = n_ctx // DN_CHUNK

    _conv_silu(qc_ref, wq_ref, pad_ref, qc_s, n_ctx, True)
    _conv_silu(kc_ref, wk_ref, pad_ref, kc_s, n_ctx, True)
    _conv_silu(vc_ref, wv_ref, pad_ref, vc_s, n_ctx, False)
    _conv_silu(qx_ref, wq_ref, pad_ref, q_s, n_lat, True)
    _conv_silu(kx_ref, wk_ref, pad_ref, k_s, n_lat, True)
    _conv_silu(vx_ref, wv_ref, pad_ref, v_s, n_lat, False)
    grp_c = math.gcd(nc_c, DN_PREP_GROUP)
    grp_x = math.gcd(nc_x, DN_PREP_GROUP)
    assert grp_c % 2 == 0 and grp_x % 2 == 0, "a group's tokens must cover whole 128-lane tiles"
    assert nc_x % 2 == 0

    def prep_c(i, carry):
        _dn_prep_group(i * grp_c, grp_c, (qc_s, kc_s, vc_s, gbc_ref), (a_c, b_c, qp_c, o0_c, ar_c))
        return carry

    def prep_x(i, carry):
        _dn_prep_group(i * grp_x, grp_x, (q_s, k_s, v_s, gbx_ref), (a_x, b_x, qp_x, o0_x, ar_x))
        return carry

    lax.fori_loop(0, nc_c // grp_c, prep_c, 0)
    lax.fori_loop(0, nc_x // grp_x, prep_x, 0)

    s_ref[...] = jnp.zeros(s_ref.shape, F32)

    def scan_c(i, carry):
        _dn_scan_step(i, nc_c - 1 - i, a_c, b_c, qp_c, o0_c, ar_c, s_ref, None, None)
        return carry

    def scan_x(i, carry):
        _dn_scan_step(i, nc_x - 1 - i, a_x, b_x, qp_x, o0_x, ar_x, s_ref, of_ref, ob_ref)
        return carry

    def finish(c):
        start = c * DN_CHUNK
        rows = pl.ds(start if isinstance(c, int) else pl.multiple_of(start, DN_CHUNK), DN_CHUNK)
        o = of_ref[rows, :] + ob_ref[rows, :]
        ms = jnp.mean(o * o, axis=-1, keepdims=True)
        y = (o * lax.rsqrt(ms + EPS)) * ng_ref[...] * _silu(z_ref[0, rows, :].astype(F32))
        y_ref[0, rows, :] = y.astype(y_ref.dtype)

    def scan_x_finish(i, carry):
        finish(i - 1)
        finish(nc_x - i)
        scan_x(i, carry)
        return carry

    half = nc_x // 2
    lax.fori_loop(0, nc_c, scan_c, 0)
    lax.fori_loop(0, half + 1, scan_x, 0)
    lax.fori_loop(half + 1, nc_x, scan_x_finish, 0)
    finish(nc_x - 1)
    finish(0)


def _deltanet(qkv_x, qkv_c, conv_w, gt_x, gt_c, z, norm_g):
    b, n_lat, _ = qkv_x.shape
    n_ctx = qkv_c.shape[1]
    gbx = gt_x.reshape(b, 2, 2, N_HEADS, 1, n_lat)
    gbc = gt_c.reshape(b, 2, 2, N_HEADS, 1, n_ctx)
    hh = N_HEADS
    d = HEAD_DIM
    nc_x, nc_c = n_lat // DN_CHUNK, n_ctx // DN_CHUNK
    tok = lambda n, off: pl.BlockSpec((1, n, d), lambda i, j: (i, 0, off + j))
    cw = lambda off: pl.BlockSpec((DN_CONV, d), lambda i, j: (0, off + j))
    gspec = lambda n: pl.BlockSpec((1, 2, 2, 1, 1, n), lambda i, j: (i, 0, 0, j, 0, 0))
    chunk_scratch = lambda nc: [
        pltpu.VMEM((nc, d, 2 * d), BF16),
        pltpu.VMEM((nc, d, 2 * d), F32),
        pltpu.VMEM((nc, 2 * DN_CHUNK, d), BF16),
        pltpu.VMEM((nc, 2 * DN_CHUNK, d), F32),
        pltpu.VMEM((nc, SUBLANES, 2 * DN_CHUNK), F32),
    ]
    return pl.pallas_call(
        functools.partial(_dn_kernel, n_lat, n_ctx),
        grid=(b, hh),
        in_specs=[tok(n_lat, 0), tok(n_lat, hh), tok(n_lat, 2 * hh),
                  tok(n_ctx, 0), tok(n_ctx, hh), tok(n_ctx, 2 * hh),
                  cw(0), cw(hh), cw(2 * hh),
                  gspec(n_lat), gspec(n_ctx),
                  tok(n_lat, 0), _const_spec((1, d))],
        out_specs=tok(n_lat, 0),
        out_shape=jax.ShapeDtypeStruct((b, n_lat, hh * d), BF16),
        scratch_shapes=[pltpu.VMEM((n_lat + 2 * CONV_PAD, d), F32)]
        + [pltpu.VMEM((n_lat, d), F32)] * 3 + [pltpu.VMEM((n_ctx, d), F32)] * 3
        + chunk_scratch(nc_x) + chunk_scratch(nc_c)
        + [pltpu.VMEM((d, 2 * d), F32), pltpu.VMEM((n_lat, d), F32), pltpu.VMEM((n_lat, d), F32)],
        compiler_params=_cparams(("parallel", "parallel")),
        name="deltanet",
    )(qkv_x, qkv_x, qkv_x, qkv_c, qkv_c, qkv_c, conv_w, conv_w, conv_w, gbx, gbc, z, norm_g)


def _rope(x, cos, sin):
    lane = lax.broadcasted_iota(jnp.int32, x.shape, 1)
    quarter = MLA_ROPE // 4
    rot = jnp.where((lane // quarter) % 2 == 0,
                    -pltpu.roll(x, LANES - quarter, axis=1), pltpu.roll(x, quarter, axis=1))
    return x * cos + rot * sin


def _rms(x, n):
    return lax.rsqrt(jnp.sum(x * x, axis=-1, keepdims=True) / n + EPS)


def _mla_proj_kernel(with_q, cq_ref, small_ref, gqa_ref, gkva_ref, wq_ref, wk_ref, wvt_ref, gq_ref, gk_ref,
                     cos_ref, sin_ref, *out_refs):
    hd = HEAD_DIM
    hh = N_HEADS
    cos, sin = cos_ref[...], sin_ref[...]
    small = small_ref[0]
    ckv = small[:, 0:MLA_KV_RANK]
    kpe = small[:, MLA_KV_RANK:MLA_KV_RANK + LANES]
    lane = lax.broadcasted_iota(jnp.int32, kpe.shape, 1)
    kpe = jnp.where(lane < MLA_ROPE, kpe, 0.0)
    kpe_ssq = jnp.sum(kpe * kpe, axis=-1, keepdims=True)
    ckvn = (ckv * _rms(ckv, MLA_KV_RANK) * gkva_ref[...]).astype(BF16)
    kf = _dot(ckvn, wk_ref[...])
    vt = _dot_nt(wvt_ref[...], ckvn)
    if with_q:
        q_ref, k_ref, v_ref = out_refs
        cq = cq_ref[0].astype(F32)
        cqn = (cq * _rms(cq, MLA_Q_RANK) * gqa_ref[...]).astype(BF16)
        qf = _dot(cqn, wq_ref[...])
        q_scale = MLA_QK_DIM ** -0.5 * math.log2(math.e) * ATTN_QK_BALANCE
    else:
        k_ref, v_ref = out_refs
    gq, gk = gq_ref[...], gk_ref[...]
    kpe_rot = _rope(kpe * gk[:, hd:2 * hd], cos, sin)
    for h in range(hh):
        if with_q:
            qn = qf[:, h * hd:(h + 1) * hd]
            qp = qf[:, (hh + h) * hd:(hh + h + 1) * hd]
            ssq = jnp.sum(qn * qn + qp * qp, axis=-1, keepdims=True)
            r = lax.rsqrt(ssq / MLA_QK_DIM + EPS)
            q_ref[0, h, :, 0:hd] = (qn * r * gq[:, 0:hd] * q_scale).astype(q_ref.dtype)
            q_ref[0, h, :, hd:2 * hd] = (_rope(qp * r * gq[:, hd:2 * hd], cos, sin) * q_scale).astype(q_ref.dtype)
        kn = kf[:, h * hd:(h + 1) * hd]
        r = lax.rsqrt((jnp.sum(kn * kn, axis=-1, keepdims=True) + kpe_ssq) / MLA_QK_DIM + EPS)
        rk = r * (1.0 / ATTN_QK_BALANCE)
        k_ref[0, h, :, 0:hd] = (kn * rk * gk[:, 0:hd]).astype(k_ref.dtype)
        k_ref[0, h, :, hd:2 * hd] = (kpe_rot * rk).astype(k_ref.dtype)
        v_ref[0, h] = vt[h * hd:(h + 1) * hd, :].astype(v_ref.dtype)


def _mla_projections(cq, small, gqa, gkva, wq, wk, wvt, gq, gk, cos, sin, with_q, tl):
    b, n, _ = small.shape
    hh, hd = N_HEADS, HEAD_DIM
    row = lambda w: pl.BlockSpec((1, tl, w), lambda i, j: (i, j, 0))
    head = lambda w: pl.BlockSpec((1, hh, tl, w), lambda i, j: (i, 0, j, 0))
    tab = pl.BlockSpec((tl, LANES), lambda i, j: (j, 0))
    out_specs = [head(2 * hd), pl.BlockSpec((1, hh, hd, tl), lambda i, j: (i, 0, 0, j))]
    out_shape = [jax.ShapeDtypeStruct((b, hh, n, 2 * hd), F8), jax.ShapeDtypeStruct((b, hh, hd, n), F8)]
    if with_q:
        out_specs = [head(2 * hd)] + out_specs
        out_shape = [jax.ShapeDtypeStruct((b, hh, n, 2 * hd), F8)] + out_shape
    return pl.pallas_call(
        functools.partial(_mla_proj_kernel, with_q),
        grid=(b, n // tl),
        in_specs=[row(cq.shape[-1]), row(small.shape[-1]), _const_spec(gqa.shape), _const_spec(gkva.shape),
                  _const_spec(wq.shape), _const_spec(wk.shape), _const_spec(wvt.shape), _const_spec(gq.shape),
                  _const_spec(gk.shape),
                  tab, tab],
        out_specs=out_specs,
        out_shape=out_shape,
        compiler_params=_cparams(("parallel", "parallel")),
        name="mla_projections_q" if with_q else "mla_projections_kv",
    )(cq, small, gqa, gkva, wq, wk, wvt, gq, gk, cos, sin)


ATTN_SUB_COLS = 256
ATTN_LOOKAHEAD = 3
ATTN_QK_BALANCE = 4.0
ATTN_P_SHIFT = 8.0


def _attn_kernel(q_ref, kc_ref, vct_ref, kx_ref, vxt_ref, o_ref):
    tq = q_ref.shape[2]
    n_sub = tq // ATTN_SUB_COLS

    def scores(i):
        q = q_ref[0, 0, i * ATTN_SUB_COLS:(i + 1) * ATTN_SUB_COLS, :]
        return _dot_nt(kc_ref[0, 0], q), _dot_nt(kx_ref[0, 0], q)

    pending = [scores(i) for i in range(min(ATTN_LOOKAHEAD, n_sub))]
    for i in range(n_sub):
        if i + ATTN_LOOKAHEAD < n_sub:
            pending.append(scores(i + ATTN_LOOKAHEAD))
        s_c, s_x = pending.pop(0)
        m = jnp.maximum(jnp.max(s_c, axis=0, keepdims=True), jnp.max(s_x, axis=0, keepdims=True)) - ATTN_P_SHIFT
        p_c = jnp.exp2(s_c - m)
        p_x = jnp.exp2(s_x - m)
        l = jnp.sum(p_c, axis=0, keepdims=True) + jnp.sum(p_x, axis=0, keepdims=True)
        o_t = _dot(vct_ref[0, 0], p_c.astype(F8)) + _dot(vxt_ref[0, 0], p_x.astype(F8))
        o_ref[0, i * ATTN_SUB_COLS:(i + 1) * ATTN_SUB_COLS, :] = (o_t / l).T.astype(o_ref.dtype)


def _attention(q, k_c, vt_c, k_x, vt_x, tq):
    b, hh, n, dq = q.shape
    n_ctx = k_c.shape[2]
    hd = vt_x.shape[2]
    keys = lambda n_: pl.BlockSpec((1, 1, n_, dq), lambda i, j, t: (i, j, 0, 0))
    vals = lambda n_: pl.BlockSpec((1, 1, hd, n_), lambda i, j, t: (i, j, 0, 0))
    return pl.pallas_call(
        _attn_kernel,
        grid=(b, hh, n // tq),
        in_specs=[pl.BlockSpec((1, 1, tq, dq), lambda i, j, t: (i, j, t, 0)),
                  keys(n_ctx), vals(n_ctx), keys(n), vals(n)],
        out_specs=pl.BlockSpec((1, tq, hd), lambda i, j, t: (i, t, j)),
        out_shape=jax.ShapeDtypeStruct((b, n, hh * hd), BF16),
        compiler_params=_cparams(("parallel", "parallel", "parallel")),
        name="attention",
    )(q, k_c, vt_c, k_x, vt_x)


def _merge_kernel(ya_ref, yb_ref, gate_ref, x_ref, g1_ref, sh2_ref, sc2_ref, n2g_ref,
                  wa_ref, wb_ref, wo_ref, rwt_ref, xmid_ref, h2_ref, aff_ref):
    d = x_ref.shape[-1]
    ga = _sigmoid(gate_ref[0, :, 0:d].astype(F32))
    gb = _sigmoid(gate_ref[0, :, d:2 * d].astype(F32))
    mix = ga * _dot(ya_ref[0], wa_ref[...]) + gb * _dot(yb_ref[0], wb_ref[...])
    mix = _dot(mix.astype(BF16), wo_ref[...])
    xm = x_ref[0] + g1_ref[0] * mix
    xmid_ref[0] = xm
    h2 = _modulated_norm(xm, n2g_ref[...], sh2_ref[0], sc2_ref[0])
    _to_tile_major(h2_ref, h2, lead=(0,))
    rw_hi, rw_lo = _hi_lo(rwt_ref[...])
    h_hi, h_lo = _hi_lo(h2)
    logits = _dot_nt(rw_hi, h_hi) + (_dot_nt(rw_hi, h_lo) + _dot_nt(rw_lo, h_hi))
    mx = jnp.max(logits, axis=0, keepdims=True)
    ex = jnp.exp(logits - mx)
    aff_ref[0] = ex / jnp.sum(ex, axis=0, keepdims=True)


def _merge(ya, yb, gate, x, g1, sh2, sc2, n2g, wa, wb, wo, rwt, tl):
    b, n, d = x.shape
    ne = rwt.shape[0]
    row = lambda w: pl.BlockSpec((1, tl, w), lambda i, j: (i, j, 0))
    vec = pl.BlockSpec((1, 1, d), lambda i, j: (i, 0, 0))
    return pl.pallas_call(
        _merge_kernel,
        grid=(b, n // tl),
        in_specs=[row(d), row(d), row(2 * d), row(d), vec, vec, vec, _const_spec((1, d)),
                  _const_spec(wa.shape), _const_spec(wb.shape), _const_spec(wo.shape), _const_spec(rwt.shape)],
        out_specs=[row(d), pl.BlockSpec((1, tl * SUBLANES, LANES), lambda i, j: (i, j, 0)),
                   pl.BlockSpec((1, ne, tl), lambda i, j: (i, 0, j))],
        out_shape=[jax.ShapeDtypeStruct((b, n, d), F32), jax.ShapeDtypeStruct((b, n * SUBLANES, LANES), F32),
                   jax.ShapeDtypeStruct((b, ne, n), F32)],
        compiler_params=_cparams(("parallel", "parallel")),
        name="merge_router",
    )(ya, yb, gate, x, g1, sh2, sc2, n2g, wa, wb, wo, rwt)


TOPK_RANK_ROWS = 64


def _lane_cumsum(x_bf16, tri_bf16, out_ref):
    ne, n = x_bf16.shape
    carry = jnp.zeros((ne, 1), F32)
    for j in range(n // LANES):
        blk = _dot(x_bf16[:, j * LANES:(j + 1) * LANES], tri_bf16) + carry
        out_ref[:, j * LANES:(j + 1) * LANES] = blk
        carry = blk[:, LANES - 1:LANES]


def _topk_kernel(cap, aff_ref, idx_ref, gate_ref, cum_ref, affsel_ref, cumrep_ref, affrep_ref):
    aff = aff_ref[0]
    ne, n = aff.shape
    bits = pltpu.bitcast(aff, jnp.int32)

    def search(i, t):
        cand = t | (jnp.int32(1) << (30 - i))
        cnt = jnp.sum(jnp.where(bits >= cand, 1.0, 0.0), axis=-1, keepdims=True)
        return jnp.where(cnt >= cap, cand, t)

    thr = lax.fori_loop(0, 31, search, jnp.zeros((ne, 1), jnp.int32))
    gt = bits > thr
    eq = bits == thr
    need = cap - jnp.sum(jnp.where(gt, 1.0, 0.0), axis=-1, keepdims=True)
    ri = lax.broadcasted_iota(jnp.int32, (LANES, LANES), 0)
    ci = lax.broadcasted_iota(jnp.int32, (LANES, LANES), 1)
    tri = jnp.where(ri <= ci, 1.0, 0.0).astype(BF16)
    _lane_cumsum(jnp.where(eq, 1.0, 0.0).astype(BF16), tri, cum_ref)
    sel = gt | (eq & (cum_ref[...] <= need))
    affsel_ref[...] = jnp.where(sel, aff, 0.0)
    _lane_cumsum(jnp.where(sel, 1.0, 0.0).astype(BF16), tri, cum_ref)
    for ex in range(ne):
        cumrep_ref[ex] = jnp.broadcast_to(cum_ref[ex:ex + 1, :], (SUBLANES, n))
        affrep_ref[ex] = jnp.broadcast_to(affsel_ref[ex:ex + 1, :], (SUBLANES, n))

    rows = TOPK_RANK_ROWS
    lane = lax.broadcasted_iota(jnp.int32, (rows, LANES), 1)
    idx_ref[0] = jnp.zeros((cap, LANES), jnp.int32)
    gate_ref[0] = jnp.zeros((cap, LANES), F32)

    def per_expert(e, carry):
        for r0 in range(0, cap, rows):
            rank = (lax.broadcasted_iota(jnp.int32, (rows, LANES), 0) + r0).astype(F32)
            cnt = jnp.zeros((rows, LANES), F32)
            gat = jnp.zeros((rows, LANES), F32)
            for j in range(n // LANES):
                c_blk = cumrep_ref[e, 0:1, j * LANES:(j + 1) * LANES]
                a_blk = affrep_ref[e, 0:1, j * LANES:(j + 1) * LANES]
                cnt = cnt + jnp.where(c_blk <= rank, 1.0, 0.0)
                gat = gat + jnp.where(c_blk == rank + 1.0, a_blk, 0.0)
            pos = jnp.sum(cnt, axis=-1, keepdims=True).astype(jnp.int32)
            g = jnp.sum(gat, axis=-1, keepdims=True)
            idx_ref[0, r0:r0 + rows, :] = jnp.where(lane == e, pos, idx_ref[0, r0:r0 + rows, :])
            gate_ref[0, r0:r0 + rows, :] = jnp.where(lane == e, g, gate_ref[0, r0:r0 + rows, :])
        return carry

    lax.fori_loop(0, ne, per_expert, 0)


def _expert_topk(aff_t, cap):
    b, ne, n = aff_t.shape
    return pl.pallas_call(
        functools.partial(_topk_kernel, cap),
        grid=(b,),
        in_specs=[pl.BlockSpec((1, ne, n), lambda i: (i, 0, 0))],
        out_specs=[pl.BlockSpec((1, cap, LANES), lambda i: (i, 0, 0))] * 2,
        out_shape=[jax.ShapeDtypeStruct((b, cap, LANES), jnp.int32), jax.ShapeDtypeStruct((b, cap, LANES), F32)],
        scratch_shapes=[pltpu.VMEM((ne, n), F32), pltpu.VMEM((ne, n), F32),
                        pltpu.VMEM((ne, SUBLANES, n), F32), pltpu.VMEM((ne, SUBLANES, n), F32)],
        compiler_params=_cparams(("parallel",)),
        name="expert_topk",
    )(aff_t)


MOE_FF_TILE = 512
MOE_UNROLL = 8


def _moe_kernel(cap, idx_ref, gate_ref, h_ref, wg_ref, wu_ref, wd_ref, o_ref, xt_ref, x_ref, y_ref, yt_ref):
    e = pl.program_id(1)
    f = pl.program_id(2)
    nf = pl.num_programs(2)

    @pl.when((e == 0) & (f == 0))
    def _():
        o_ref[...] = jnp.zeros(o_ref.shape, o_ref.dtype)

    @pl.when(f == 0)
    def _():
        def gather(i, carry):
            for u in range(MOE_UNROLL):
                r = i * MOE_UNROLL + u
                xt_ref[_tile(r), :] = h_ref[0, _tile(idx_ref[0, 0, 0, r]), :]
            return carry
        lax.fori_loop(0, cap // MOE_UNROLL, gather, 0)
        x_ref[...] = _from_tile_major(xt_ref, cap).astype(x_ref.dtype)

    x = x_ref[...]
    hid = (_silu(_dot(x, wg_ref[0])) * _dot(x, wu_ref[0])).astype(BF16)
    part = _dot(hid, wd_ref[0])

    @pl.when(f == 0)
    def _():
        y_ref[...] = part

    @pl.when(f > 0)
    def _():
        y_ref[...] += part

    @pl.when(f == nf - 1)
    def _():
        _to_tile_major(yt_ref, y_ref[...])

        def scatter(i, carry):
            rows = [i * MOE_UNROLL + u for u in range(MOE_UNROLL)]
            toks = [idx_ref[0, 0, 0, r] for r in rows]
            new = [o_ref[0, _tile(t), :] + gate_ref[0, 0, 0, r] * yt_ref[_tile(r), :] for r, t in zip(rows, toks)]
            for t, val in zip(toks, new):
                o_ref[0, _tile(t), :] = val
            return carry
        lax.fori_loop(0, cap // MOE_UNROLL, scatter, 0)


def _moe(idx, gate, h2t, wg, wu, wd):
    b, rows, _ = h2t.shape
    ne, d, ffp = wg.shape
    cap = idx.shape[-1]
    tf = MOE_FF_TILE
    smem = lambda: pl.BlockSpec((1, 1, 1, cap), lambda i, j, k: (i, j, 0, 0), memory_space=pltpu.SMEM)
    act = lambda: pl.BlockSpec((1, rows, LANES), lambda i, j, k: (i, 0, 0), pipeline_mode=pl.Buffered(1))
    return pl.pallas_call(
        functools.partial(_moe_kernel, cap),
        grid=(b, ne, ffp // tf),
        in_specs=[smem(), smem(), act(),
                  pl.BlockSpec((1, d, tf), lambda i, j, k: (j, 0, k)),
                  pl.BlockSpec((1, d, tf), lambda i, j, k: (j, 0, k)),
                  pl.BlockSpec((1, tf, d), lambda i, j, k: (j, k, 0))],
        out_specs=act(),
        out_shape=jax.ShapeDtypeStruct((b, rows, LANES), F32),
        scratch_shapes=[pltpu.VMEM((cap * SUBLANES, LANES), F32), pltpu.VMEM((cap, d), BF16),
                        pltpu.VMEM((cap, d), F32), pltpu.VMEM((cap * SUBLANES, LANES), F32)],
        compiler_params=_cparams(("parallel", "arbitrary", "arbitrary")),
        name="expert_ffn",
    )(idx[:, :, None, :], gate[:, :, None, :], h2t, wg, wu, wd)


def _final_kernel(xm_ref, moe_ref, g_ref, o_ref):
    o_ref[0] = xm_ref[0] + g_ref[0] * _from_tile_major(moe_ref, xm_ref.shape[1], lead=(0,))


def _final(xmid, moe_t, g2, tl):
    b, n, d = xmid.shape
    row = pl.BlockSpec((1, tl, d), lambda i, j: (i, j, 0))
    return pl.pallas_call(
        _final_kernel,
        grid=(b, n // tl),
        in_specs=[row, pl.BlockSpec((1, tl * SUBLANES, LANES), lambda i, j: (i, j, 0)),
                  pl.BlockSpec((1, 1, d), lambda i, j: (i, 0, 0))],
        out_specs=row,
        out_shape=jax.ShapeDtypeStruct((b, n, d), F32),
        compiler_params=_cparams(("parallel", "parallel")),
        name="final_residual",
    )(xmid, moe_t, g2)


def _rope_tables(n_lat):
    rows = n_lat // GRID_W
    row = jnp.repeat(jnp.arange(rows), GRID_W).astype(F32)
    col = jnp.broadcast_to(jnp.arange(GRID_W), (rows, GRID_W)).reshape(-1).astype(F32)
    n_freq = MLA_ROPE // 4
    inv_freq = ROPE_BASE ** (-jnp.arange(n_freq, dtype=F32) / n_freq)
    ang_r = row[:, None] * inv_freq
    ang_c = col[:, None] * inv_freq
    ang = jnp.concatenate([ang_r, ang_r, ang_c, ang_c], axis=-1)
    zeros = jnp.zeros((n_lat, LANES - MLA_ROPE), F32)
    return jnp.concatenate([jnp.cos(ang), zeros], axis=-1), jnp.concatenate([jnp.sin(ang), zeros], axis=-1)


def _pad_lanes(v, width):
    return jnp.concatenate([v, jnp.zeros(v.shape[:-1] + (width - v.shape[-1],), v.dtype)], axis=-1)


def kernel(x, c, ctx, c_ctx, ada_w, ada_b, norm1_g, norm2_g, w_in, conv_w, a_log, dt_bias, dn_norm_g, w_out_a, q_a_norm_g, w_uq, kv_a_norm_g, w_ukv, q_norm_g, k_norm_g, w_out_b, w_o, router_w, w_gate, w_up, w_down):
    assert ada_w.shape[0] == 1, "single-layer block"
    b, n_lat, d = x.shape
    n_ctx = ctx.shape[1]
    hh, hd = N_HEADS, HEAD_DIM
    qkv_dim = 3 * hh * hd
    nb = 2 * hh

    n_cond = -(-(b + 1) // SUBLANES) * SUBLANES
    cond = jnp.concatenate([c, c_ctx[None], jnp.zeros((n_cond - b - 1, d), F32)], axis=0)
    mod = _modulation(cond, ada_w[0], ada_b[0])
    mods = [mod[:, i * d:(i + 1) * d] for i in range(6)]
    lat = lambda m: m[:b, None, :]
    cvec = lambda m: jnp.broadcast_to(m[b][None, None, :], (b, 1, d))

    o_z = qkv_dim
    o_alpha = o_z + hh * hd
    o_beta = o_alpha + nb
    o_cq = o_beta + nb
    o_ckv = o_cq + MLA_Q_RANK
    o_kr = o_ckv + MLA_KV_RANK
    o_gate = o_kr + MLA_ROPE
    w = w_in[0]
    w_big = jnp.concatenate([w[:, :o_alpha], w[:, o_gate:], w[:, o_cq:o_ckv]], axis=1).astype(BF16)
    small_w = _pad_lanes(w[:, o_ckv:o_gate], MLA_KV_RANK + LANES).astype(BF16)
    w_ab_t = jnp.transpose(w[:, o_alpha:o_cq]).astype(BF16)
    gate_pad = jnp.zeros((nb, 1), F32)
    alog = jnp.concatenate([a_log[0].reshape(nb, 1), gate_pad], axis=0)
    dtb = jnp.concatenate([dt_bias[0].reshape(nb, 1), gate_pad], axis=0)
    widths = (qkv_dim, hh * hd, N_BRANCHES * d, MLA_Q_RANK)
    g1 = norm1_g[0][None]
    qkv_x, z_x, gt_x, cq_x, sm_x, dg_x = _in_projection(x, lat(mods[0]), lat(mods[1]), g1, w_big, small_w, w_ab_t,
                                                        alog, dtb, widths, 512)
    qkv_c, _, _, cq_c, sm_c, dg_c = _in_projection(ctx, cvec(mods[0]), cvec(mods[1]), g1, w_big, small_w, w_ab_t,
                                                   alog, dtb, widths, min(n_ctx, 512))

    y_a = _deltanet(qkv_x, qkv_c, conv_w[0], dg_x, dg_c, z_x, dn_norm_g[0][None])

    wq = w_uq[0].reshape(MLA_Q_RANK, hh, MLA_QK_DIM)
    wq = jnp.concatenate([wq[:, :, :hd].reshape(MLA_Q_RANK, hh * hd),
                          _pad_lanes(wq[:, :, hd:], hd).reshape(MLA_Q_RANK, hh * hd)], axis=1).astype(BF16)
    wkv = w_ukv[0].reshape(MLA_KV_RANK, hh, 2 * hd)
    wk = wkv[:, :, :hd].reshape(MLA_KV_RANK, hh * hd).astype(BF16)
    wvt = jnp.transpose(wkv[:, :, hd:].reshape(MLA_KV_RANK, hh * hd)).astype(BF16)
    gq = _pad_lanes(q_norm_g[0], 2 * hd)[None]
    gk = _pad_lanes(k_norm_g[0], 2 * hd)[None]
    gqa, gkva = q_a_norm_g[0][None], kv_a_norm_g[0][None]
    cos, sin = _rope_tables(n_lat)
    cos_c = _pad_lanes(jnp.ones((n_ctx, MLA_ROPE), F32), LANES)
    sin_c = jnp.zeros((n_ctx, LANES), F32)
    q_x, k_x, v_x = _mla_projections(cq_x, sm_x, gqa, gkva, wq, wk, wvt, gq, gk, cos, sin, True, 512)
    k_c, v_c = _mla_projections(cq_c, sm_c, gqa, gkva, wq, wk, wvt, gq, gk, cos_c, sin_c, False, min(n_ctx, 512))
    y_b = _attention(q_x, k_c, v_c, k_x, v_x, min(n_lat, 1024))

    rwt = jnp.transpose(router_w[0])
    x_mid, h2, aff_t = _merge(y_a, y_b, gt_x, x, lat(mods[2]), lat(mods[3]), lat(mods[4]), norm2_g[0][None],
                              w_out_a[0].astype(BF16), w_out_b[0].astype(BF16), w_o[0].astype(BF16), rwt, 512)

    cap = EC_CAPACITY * n_lat // N_EXPERTS
    idx, gate = _expert_topk(aff_t, cap)
    idx = jnp.transpose(idx[:, :, :N_EXPERTS], (0, 2, 1))
    gate = jnp.transpose(gate[:, :, :N_EXPERTS], (0, 2, 1))
    ff = w_gate.shape[-1]
    ffp = -(-ff // MOE_FF_TILE) * MOE_FF_TILE
    wg = _pad_lanes(w_gate[0], ffp).astype(BF16)
    wu = _pad_lanes(w_up[0], ffp).astype(BF16)
    wd = jnp.concatenate([w_down[0], jnp.zeros((N_EXPERTS, ffp - ff, d), F32)], axis=1).astype(BF16)
    moe = _moe(idx, gate, h2, wg, wu, wd)
    return _final(x_mid, moe, lat(mods[5]), 512)
```

```python
import functools
import math

import jax
import jax.numpy as jnp
from jax import lax
from jax.experimental import pallas as pl
from jax.experimental.pallas import tpu as pltpu

F32 = jnp.float32
BF16 = jnp.bfloat16
F8 = jnp.float8_e4m3fn
EPS = 1e-6

N_HEADS = 8
HEAD_DIM = 128
DN_CONV = 5
DN_CHUNK = 64
MLA_Q_RANK = 384
MLA_KV_RANK = 256
MLA_ROPE = 64
MLA_QK_DIM = HEAD_DIM + MLA_ROPE
GRID_W = 64
ROPE_BASE = 10000.0
N_EXPERTS = 16
EC_CAPACITY = 2
N_BRANCHES = 2

LANES = 128
SUBLANES = 8
VMEM_LIMIT_BYTES = 56 * 1024 * 1024

HI = lax.Precision.HIGHEST


def _cparams(sem):
    return pltpu.CompilerParams(dimension_semantics=sem, vmem_limit_bytes=VMEM_LIMIT_BYTES)


def _dot(a, b, precision=None):
    return jnp.dot(a, b, preferred_element_type=F32, precision=precision)


def _dot_nt(a, b, precision=None):
    return lax.dot_general(a, b, (((1,), (1,)), ((), ())), preferred_element_type=F32, precision=precision)


def _dot_tn(a, b):
    return lax.dot_general(a, b, (((0,), (0,)), ((), ())), preferred_element_type=F32)


def _sigmoid(x):
    return 1.0 / (1.0 + jnp.exp(-x))


def _silu(x):
    return x * _sigmoid(x)


def _const_spec(shape):
    nd = len(shape)
    return pl.BlockSpec(shape, lambda *_: (0,) * nd, pipeline_mode=pl.Buffered(1))


def _to_tile_major(ref, x, lead=()):
    rows = x.shape[0]
    for s_ in range(SUBLANES):
        ref[lead + (pl.ds(s_, rows, stride=SUBLANES), slice(None))] = x[:, s_ * LANES:(s_ + 1) * LANES]


def _from_tile_major(ref, rows, lead=()):
    return jnp.concatenate([ref[lead + (pl.ds(s_, rows, stride=SUBLANES), slice(None))] for s_ in range(SUBLANES)],
                           axis=1)


def _tile(i):
    return pl.ds(pl.multiple_of(i * SUBLANES, SUBLANES), SUBLANES)


def _mod_kernel(c_ref, w_ref, b_ref, o_ref):
    c = c_ref[...]
    o_ref[...] = _dot(_silu(c), w_ref[...], precision=HI) + b_ref[...]


def _modulation(cond, ada_w, ada_b):
    n, d = cond.shape
    n_out = ada_w.shape[1]
    tn = d
    return pl.pallas_call(
        _mod_kernel,
        grid=(n_out // tn,),
        in_specs=[pl.BlockSpec((n, d), lambda j: (0, 0)),
                  pl.BlockSpec((d, tn), lambda j: (0, j)),
                  pl.BlockSpec((1, tn), lambda j: (0, j))],
        out_specs=pl.BlockSpec((n, tn), lambda j: (0, j)),
        out_shape=jax.ShapeDtypeStruct((n, n_out), F32),
        compiler_params=_cparams(("parallel",)),
        name="modulation",
    )(cond, ada_w, ada_b.reshape(1, n_out))


def _modulated_norm(x, g, shift, scale):
    ms = jnp.mean(x * x, axis=-1, keepdims=True)
    return (x * lax.rsqrt(ms + EPS)) * g * (1.0 + scale) + shift


def _inproj_kernel(widths, x_ref, shift_ref, scale_ref, g_ref, wb_ref, ws_ref, wab_ref, alog_ref, dtb_ref,
                   *out_refs):
    big_refs, small_ref, gates_ref = out_refs[:-2], out_refs[-2], out_refs[-1]
    h = _modulated_norm(x_ref[0], g_ref[...], shift_ref[0], scale_ref[0]).astype(BF16)
    off = 0
    for o_ref, width in zip(big_refs, widths):
        for c0 in range(0, width, 1024):
            c1 = min(c0 + 1024, width)
            o_ref[0, :, c0:c1] = _dot(h, wb_ref[:, off + c0:off + c1]).astype(o_ref.dtype)
        off += width
    small_ref[0] = _dot(h, ws_ref[...])
    ab = _dot_nt(wab_ref[...], h)
    row_id = lax.broadcasted_iota(jnp.int32, ab.shape, 0)
    xs = ab + dtb_ref[...]
    softplus = jnp.maximum(xs, 0.0) + jnp.log1p(jnp.exp(-jnp.abs(xs)))
    gates_ref[0] = jnp.where(row_id < 2 * N_HEADS, -jnp.exp(alog_ref[...]) * softplus, _sigmoid(ab))


def _in_projection(x, shift, scale, g, w_big, w_small, w_ab_t, alog, dtb, widths, tl):
    b, n, d = x.shape
    row = lambda w: pl.BlockSpec((1, tl, w), lambda i, j: (i, j, 0))
    vec = pl.BlockSpec((1, 1, d), lambda i, j: (i, 0, 0))
    n_gate = w_ab_t.shape[0]
    outs = [jax.ShapeDtypeStruct((b, n, w), BF16) for w in widths]
    outs.append(jax.ShapeDtypeStruct((b, n, w_small.shape[1]), F32))
    outs.append(jax.ShapeDtypeStruct((b, n_gate, n), F32))
    return pl.pallas_call(
        functools.partial(_inproj_kernel, widths),
        grid=(b, n // tl),
        in_specs=[row(d), vec, vec, _const_spec((1, d)), _const_spec(w_big.shape), _const_spec(w_small.shape),
                  _const_spec(w_ab_t.shape), _const_spec(alog.shape), _const_spec(dtb.shape)],
        out_specs=[row(w) for w in widths] + [row(w_small.shape[1]),
                                              pl.BlockSpec((1, n_gate, tl), lambda i, j: (i, 0, j))],
        out_shape=outs,
        compiler_params=_cparams(("parallel", "parallel")),
        name="in_projection",
    )(x, shift, scale, g, w_big, w_small, w_ab_t, alog, dtb)


CONV_ROWS = 512
CONV_PAD = 8


def _conv_fill(raw_ref, pad_ref, n_tok):
    zeros = jnp.zeros((CONV_PAD, LANES), F32)
    pad_ref[0:CONV_PAD, :] = zeros
    pad_ref[CONV_PAD + n_tok:2 * CONV_PAD + n_tok, :] = zeros
    pad_ref[CONV_PAD:CONV_PAD + n_tok, :] = raw_ref[0].astype(F32)


def _conv_block(pad_ref, w, r0, rows, normalize):
    half = DN_CONV // 2
    acc = None
    for j in range(DN_CONV):
        term = pad_ref[pl.ds(r0 + (CONV_PAD + j - half), rows), :] * w[j:j + 1, :]
        acc = term if acc is None else acc + term
    y = _silu(acc)
    if normalize:
        y = y * lax.rsqrt(jnp.sum(y * y, axis=-1, keepdims=True) + EPS)
    return y


def _conv_silu(raw_ref, w_ref, pad_ref, out_ref, n_tok, normalize):
    _conv_fill(raw_ref, pad_ref, n_tok)
    w = w_ref[...]
    rb = min(CONV_ROWS, n_tok)
    for r0 in range(0, n_tok, rb):
        out_ref[r0:r0 + rb, :] = _conv_block(pad_ref, w, r0, rb, normalize)


def _dn_masks():
    n = 2 * DN_CHUNK
    ri = lax.broadcasted_iota(jnp.int32, (n, n), 0)
    ci = lax.broadcasted_iota(jnp.int32, (n, n), 1)
    top = ri < DN_CHUNK
    same = (ri // DN_CHUNK) == (ci // DN_CHUNK)
    sgn = jnp.where(top, 1, -1)
    delta = (ri - ci) * sgn
    incl = same & (delta >= 0)
    strict = same & (delta > 0)
    incl_t = same & (delta <= 0)
    eye = ri == ci
    return top, incl, strict, incl_t, eye


DN_INV_BLOCK = 8
DN_PREP_GROUP = 16


def _hi_lo(a_f32):
    hi = a_f32.astype(BF16)
    return hi, (a_f32 - hi.astype(F32)).astype(BF16)


def _split_dot(a_f32, b_bf16):
    hi, lo = _hi_lo(a_f32)
    return _dot(hi, b_bf16) + _dot(lo, b_bf16)


def _bdot(a_f32, b_f32):
    return _dot(a_f32.astype(BF16), b_f32.astype(BF16))


def _dn_gate_rows(c0, group, gt_ref):
    n = 2 * DN_CHUNK
    t0 = pl.multiple_of(c0 * DN_CHUNK, n)
    lane = lax.broadcasted_iota(jnp.int32, (1, n), 1)
    rows = [[gt_ref[0, kind, dr, 0, :, pl.ds(t0, group * DN_CHUNK)] for dr in range(2)] for kind in range(2)]
    out = []
    for j in range(group):
        p, odd = divmod(j, 2)
        both = []
        for kind in range(2):
            fwd = rows[kind][0][:, p * n:(p + 1) * n]
            bwd = rows[kind][1][:, p * n:(p + 1) * n]
            if odd:
                both.append(jnp.where(lane < DN_CHUNK, pltpu.roll(fwd, DN_CHUNK, axis=1), bwd))
            else:
                both.append(jnp.where(lane < DN_CHUNK, fwd, pltpu.roll(bwd, DN_CHUNK, axis=1)))
        out.append(jnp.concatenate(both, axis=0))
    return out


def _dn_prep_load(c, q_ref, k_ref, v_ref):
    r0 = pl.multiple_of(c * DN_CHUNK, DN_CHUNK)
    return q_ref[pl.ds(r0, DN_CHUNK), :], k_ref[pl.ds(r0, DN_CHUNK), :], v_ref[pl.ds(r0, DN_CHUNK), :]


def _dn_prep_compute(q, k, v, gb):
    n = 2 * DN_CHUNK
    top, incl, strict, incl_t, eye = _dn_masks()
    g_row = jnp.broadcast_to(gb[0:1, :], (SUBLANES, n))
    beta_rows = jnp.broadcast_to(gb[1:2, :], (n, n))
    kst = jnp.concatenate([k, k], axis=0)
    qst = jnp.concatenate([q, q], axis=0)
    vst = jnp.concatenate([v, v], axis=0)
    kst_b = kst.astype(BF16)

    gc_row = _split_dot(g_row, jnp.where(incl_t, 1.0, 0.0).astype(BF16))[0:1, :]
    kk = _dot_nt(kst_b, kst_b)
    qk = _dot_nt(qst.astype(BF16), kst_b)
    yield
    c2 = jnp.broadcast_to(gc_row, (n, n))
    c1 = c2.T
    beta_c = beta_rows.T
    tot = jnp.where(top, c1[DN_CHUNK - 1:DN_CHUNK, :], c1[DN_CHUNK:DN_CHUNK + 1, :])
    decay = jnp.exp(jnp.where(incl, c1 - c2, -jnp.inf))
    e_gc = jnp.exp(c1)
    e_rest = jnp.exp(tot - c1)
    scale = HEAD_DIM ** -0.5

    m = jnp.where(strict, kk * beta_c * decay, 0.0)
    ri = lax.broadcasted_iota(jnp.int32, (n, n), 0)
    ci = lax.broadcasted_iota(jnp.int32, (n, n), 1)
    blk = DN_INV_BLOCK
    dg = jnp.where((ri // blk) == (ci // blk), m, 0.0)
    t = jnp.where(eye, 1.0, 0.0) - dg
    dp = _bdot(dg, dg)
    yield
    for it in range(blk.bit_length() - 2):
        t = t + _bdot(t, dp)
        if it + 1 < blk.bit_length() - 2:
            dp = _bdot(dp, dp)
        yield
    while blk < DN_CHUNK:
        off = jnp.where(((ri // (2 * blk)) == (ci // (2 * blk))) & ((ri // blk) != (ci // blk)), m, 0.0)
        ta = _bdot(t, off)
        yield
        t = t - _bdot(ta, t)
        yield
        blk *= 2

    kb = kst * beta_c
    rhs = jnp.concatenate([vst * beta_c, kb * e_gc], axis=1)
    uw = _bdot(t, rhs)
    yield
    d = HEAD_DIM
    u_b = uw[:, 0:d].astype(BF16)
    w_b = uw[:, d:2 * d].astype(BF16)
    kd = (kst * e_rest).astype(BF16)
    qkm = (jnp.where(incl, qk * decay, 0.0) * scale).astype(BF16)
    ri2 = lax.broadcasted_iota(jnp.int32, (n, 2 * d), 0)
    ci2 = lax.broadcasted_iota(jnp.int32, (n, 2 * d), 1)
    own_dir = (ri2 // DN_CHUNK) == (ci2 // d)
    zero = jnp.zeros((), BF16)
    wu_bd = jnp.concatenate([jnp.where(own_dir, jnp.concatenate([w_b, w_b], axis=1), zero),
                             jnp.where(own_dir, jnp.concatenate([u_b, u_b], axis=1), zero)], axis=1)
    kw = _dot_tn(kd, wu_bd)
    qw = _dot(qkm, jnp.concatenate([w_b, u_b], axis=1))
    yield
    return ((-kw[:, 0:2 * d]).astype(BF16),
            kw[:, 2 * d:4 * d],
            (qst * e_gc * scale - qw[:, 0:d]).astype(BF16),
            qw[:, d:2 * d],
            jnp.exp(tot)[DN_CHUNK - 4:DN_CHUNK + 4, :])


def _dn_prep_group(c0, group, in_refs, out_refs):
    *qkv_refs, gt_ref = in_refs
    gates = _dn_gate_rows(c0, group, gt_ref)
    gens = [_dn_prep_compute(*_dn_prep_load(c0 + j, *qkv_refs), gates[j]) for j in range(group)]
    results = [None] * group
    while any(r is None for r in results):
        for j, gen in enumerate(gens):
            try:
                next(gen)
            except StopIteration as done:
                results[j] = done.value
    for j, res in enumerate(results):
        for ref, val in zip(out_refs, res):
            ref[c0 + j] = val


def _dn_scan_step(cf, cb, a_ref, b_ref, q_ref, o0_ref, ar_ref, s_ref, of_ref, ob_ref):
    h = DN_CHUNK
    d = HEAD_DIM
    s = s_ref[...]
    s_b = s.astype(BF16)
    zero = jnp.zeros((d, d), BF16)
    s_bd = jnp.concatenate([jnp.concatenate([s_b[:, 0:d], zero], axis=1),
                            jnp.concatenate([zero, s_b[:, d:2 * d]], axis=1)], axis=0)
    a_sel = jnp.concatenate([a_ref[cf, :, 0:d], a_ref[cb, :, d:2 * d]], axis=1)
    zq = jnp.zeros((h, d), BF16)
    q_rows = jnp.concatenate([jnp.concatenate([q_ref[cf, 0:h, :], zq], axis=1),
                              jnp.concatenate([zq, q_ref[cb, h:2 * h, :]], axis=1)], axis=0)
    r = _dot(jnp.concatenate([a_sel, q_rows], axis=0), s_bd)
    if of_ref is not None:
        of_ref[pl.ds(pl.multiple_of(cf * h, h), h), :] = r[d:d + h, 0:d] + o0_ref[cf, 0:h, :]
        ob_ref[pl.ds(pl.multiple_of(cb * h, h), h), :] = r[d + h:d + 2 * h, d:2 * d] + o0_ref[cb, h:2 * h, :]
    b_sel = jnp.concatenate([b_ref[cf, :, 0:d], b_ref[cb, :, d:2 * d]], axis=1)
    a = jnp.concatenate([jnp.broadcast_to(ar_ref[cf, 0:1, :], (d, d)),
                         jnp.broadcast_to(ar_ref[cb, 4:5, :], (d, d))], axis=1)
    s_ref[...] = s * a + r[0:d, :] + b_sel


def _dn_kernel(n_lat, n_ctx,
               qx_ref, kx_ref, vx_ref, qc_ref, kc_ref, vc_ref, wq_ref, wk_ref, wv_ref,
               gbx_ref, gbc_ref, z_ref, ng_ref, y_ref,
               pad_ref, q_s, k_s, v_s, qc_s, kc_s, vc_s,
               a_x, b_x, qp_x, o0_x, ar_x, a_c, b_c, qp_c, o0_c, ar_c,
               s_ref, of_ref, ob_ref):
    nc_x = n_lat // DN_CHUNK
    nc_c = n_ctx // DN_CHUNK

    _conv_silu(qc_ref, wq_ref, pad_ref, qc_s, n_ctx, True)
    _conv_silu(kc_ref, wk_ref, pad_ref, kc_s, n_ctx, True)
    _conv_silu(vc_ref, wv_ref, pad_ref, vc_s, n_ctx, False)
    _conv_silu(qx_ref, wq_ref, pad_ref, q_s, n_lat, True)
    _conv_silu(kx_ref, wk_ref, pad_ref, k_s, n_lat, True)
    _conv_silu(vx_ref, wv_ref, pad_ref, v_s, n_lat, False)
    grp_c = math.gcd(nc_c, DN_PREP_GROUP)
    grp_x = math.gcd(nc_x, DN_PREP_GROUP)
    assert grp_c % 2 == 0 and grp_x % 2 == 0, "a group's tokens must cover whole 128-lane tiles"
    assert nc_x % 2 == 0

    def prep_c(i, carry):
        _dn_prep_group(i * grp_c, grp_c, (qc_s, kc_s, vc_s, gbc_ref), (a_c, b_c, qp_c, o0_c, ar_c))
        return carry

    def prep_x(i, carry):
        _dn_prep_group(i * grp_x, grp_x, (q_s, k_s, v_s, gbx_ref), (a_x, b_x, qp_x, o0_x, ar_x))
        return carry

    lax.fori_loop(0, nc_c // grp_c, prep_c, 0)
    lax.fori_loop(0, nc_x // grp_x, prep_x, 0)

    s_ref[...] = jnp.zeros(s_ref.shape, F32)

    def scan_c(i, carry):
        _dn_scan_step(i, nc_c - 1 - i, a_c, b_c, qp_c, o0_c, ar_c, s_ref, None, None)
        return carry

    def scan_x(i, carry):
        _dn_scan_step(i, nc_x - 1 - i, a_x, b_x, qp_x, o0_x, ar_x, s_ref, of_ref, ob_ref)
        return carry

    def finish(c):
        start = c * DN_CHUNK
        rows = pl.ds(start if isinstance(c, int) else pl.multiple_of(start, DN_CHUNK), DN_CHUNK)
        o = of_ref[rows, :] + ob_ref[rows, :]
        ms = jnp.mean(o * o, axis=-1, keepdims=True)
        y = (o * lax.rsqrt(ms + EPS)) * ng_ref[...] * _silu(z_ref[0, rows, :].astype(F32))
        y_ref[0, rows, :] = y.astype(y_ref.dtype)

    def scan_x_finish(i, carry):
        finish(i - 1)
        finish(nc_x - i)
        scan_x(i, carry)
        return carry

    half = nc_x // 2
    lax.fori_loop(0, nc_c, scan_c, 0)
    lax.fori_loop(0, half + 1, scan_x, 0)
    lax.fori_loop(half + 1, nc_x, scan_x_finish, 0)
    finish(nc_x - 1)
    finish(0)


def _deltanet(qkv_x, qkv_c, conv_w, gt_x, gt_c, z, norm_g):
    b, n_lat, _ = qkv_x.shape
    n_ctx = qkv_c.shape[1]
    gbx = gt_x.reshape(b, 2, 2, N_HEADS, 1, n_lat)
    gbc = gt_c.reshape(b, 2, 2, N_HEADS, 1, n_ctx)
    hh = N_HEADS
    d = HEAD_DIM
    nc_x, nc_c = n_lat // DN_CHUNK, n_ctx // DN_CHUNK
    tok = lambda n, off: pl.BlockSpec((1, n, d), lambda i, j: (i, 0, off + j))
    cw = lambda off: pl.BlockSpec((DN_CONV, d), lambda i, j: (0, off + j))
    gspec = lambda n: pl.BlockSpec((1, 2, 2, 1, 1, n), lambda i, j: (i, 0, 0, j, 0, 0))
    chunk_scratch = lambda nc: [
        pltpu.VMEM((nc, d, 2 * d), BF16),
        pltpu.VMEM((nc, d, 2 * d), F32),
        pltpu.VMEM((nc, 2 * DN_CHUNK, d), BF16),
        pltpu.VMEM((nc, 2 * DN_CHUNK, d), F32),
        pltpu.VMEM((nc, SUBLANES, 2 * DN_CHUNK), F32),
    ]
    return pl.pallas_call(
        functools.partial(_dn_kernel, n_lat, n_ctx),
        grid=(b, hh),
        in_specs=[tok(n_lat, 0), tok(n_lat, hh), tok(n_lat, 2 * hh),
                  tok(n_ctx, 0), tok(n_ctx, hh), tok(n_ctx, 2 * hh),
                  cw(0), cw(hh), cw(2 * hh),
                  gspec(n_lat), gspec(n_ctx),
                  tok(n_lat, 0), _const_spec((1, d))],
        out_specs=tok(n_lat, 0),
        out_shape=jax.ShapeDtypeStruct((b, n_lat, hh * d), BF16),
        scratch_shapes=[pltpu.VMEM((n_lat + 2 * CONV_PAD, d), F32)]
        + [pltpu.VMEM((n_lat, d), F32)] * 3 + [pltpu.VMEM((n_ctx, d), F32)] * 3
        + chunk_scratch(nc_x) + chunk_scratch(nc_c)
        + [pltpu.VMEM((d, 2 * d), F32), pltpu.VMEM((n_lat, d), F32), pltpu.VMEM((n_lat, d), F32)],
        compiler_params=_cparams(("parallel", "parallel")),
        name="deltanet",
    )(qkv_x, qkv_x, qkv_x, qkv_c, qkv_c, qkv_c, conv_w, conv_w, conv_w, gbx, gbc, z, norm_g)


ATTN_ONES_ROWS = 32


def _rope(x, cos, sin):
    lane = lax.broadcasted_iota(jnp.int32, x.shape, 1)
    quarter = MLA_ROPE // 4
    rot = jnp.where((lane // quarter) % 2 == 0,
                    -pltpu.roll(x, LANES - quarter, axis=1), pltpu.roll(x, quarter, axis=1))
    return x * cos + rot * sin


def _rms(x, n):
    return lax.rsqrt(jnp.sum(x * x, axis=-1, keepdims=True) / n + EPS)


def _mla_proj_kernel(with_q, cq_ref, small_ref, gqa_ref, gkva_ref, wq_ref, wk_ref, wvt_ref, gq_ref, gk_ref,
                     cos_ref, sin_ref, *out_refs):
    hd = HEAD_DIM
    hh = N_HEADS
    cos, sin = cos_ref[...], sin_ref[...]
    small = small_ref[0]
    ckv = small[:, 0:MLA_KV_RANK]
    kpe = small[:, MLA_KV_RANK:MLA_KV_RANK + LANES]
    lane = lax.broadcasted_iota(jnp.int32, kpe.shape, 1)
    kpe = jnp.where(lane < MLA_ROPE, kpe, 0.0)
    kpe_ssq = jnp.sum(kpe * kpe, axis=-1, keepdims=True)
    ckvn = (ckv * _rms(ckv, MLA_KV_RANK) * gkva_ref[...]).astype(BF16)
    kf = _dot(ckvn, wk_ref[...])
    vt = _dot_nt(wvt_ref[...], ckvn)
    if with_q:
        q_ref, k_ref, v_ref = out_refs
        cq = cq_ref[0].astype(F32)
        cqn = (cq * _rms(cq, MLA_Q_RANK) * gqa_ref[...]).astype(BF16)
        qf = _dot(cqn, wq_ref[...])
        q_scale = MLA_QK_DIM ** -0.5 * math.log2(math.e) * ATTN_QK_BALANCE
    else:
        k_ref, v_ref = out_refs
    gq, gk = gq_ref[...], gk_ref[...]
    kpe_rot = _rope(kpe * gk[:, hd:2 * hd], cos, sin)
    for h in range(hh):
        if with_q:
            qn = qf[:, h * hd:(h + 1) * hd]
            qp = qf[:, (hh + h) * hd:(hh + h + 1) * hd]
            ssq = jnp.sum(qn * qn + qp * qp, axis=-1, keepdims=True)
            r = lax.rsqrt(ssq / MLA_QK_DIM + EPS)
            q_ref[0, h, :, 0:hd] = (qn * r * gq[:, 0:hd] * q_scale).astype(q_ref.dtype)
            q_ref[0, h, :, hd:2 * hd] = (_rope(qp * r * gq[:, hd:2 * hd], cos, sin) * q_scale).astype(q_ref.dtype)
        kn = kf[:, h * hd:(h + 1) * hd]
        r = lax.rsqrt((jnp.sum(kn * kn, axis=-1, keepdims=True) + kpe_ssq) / MLA_QK_DIM + EPS)
        rk = r * (1.0 / ATTN_QK_BALANCE)
        k_ref[0, h, :, 0:hd] = (kn * rk * gk[:, 0:hd]).astype(k_ref.dtype)
        k_ref[0, h, :, hd:2 * hd] = (kpe_rot * rk).astype(k_ref.dtype)
        v_ref[0, h, 0:hd, :] = vt[h * hd:(h + 1) * hd, :].astype(v_ref.dtype)
        v_ref[0, h, hd:hd + ATTN_ONES_ROWS, :] = jnp.ones((ATTN_ONES_ROWS, vt.shape[1]), v_ref.dtype)


def _mla_projections(cq, small, gqa, gkva, wq, wk, wvt, gq, gk, cos, sin, with_q, tl):
    b, n, _ = small.shape
    hh, hd = N_HEADS, HEAD_DIM
    row = lambda w: pl.BlockSpec((1, tl, w), lambda i, j: (i, j, 0))
    head = lambda w: pl.BlockSpec((1, hh, tl, w), lambda i, j: (i, 0, j, 0))
    tab = pl.BlockSpec((tl, LANES), lambda i, j: (j, 0))
    hv = hd + ATTN_ONES_ROWS
    out_specs = [head(2 * hd), pl.BlockSpec((1, hh, hv, tl), lambda i, j: (i, 0, 0, j))]
    out_shape = [jax.ShapeDtypeStruct((b, hh, n, 2 * hd), F8), jax.ShapeDtypeStruct((b, hh, hv, n), F8)]
    if with_q:
        out_specs = [head(2 * hd)] + out_specs
        out_shape = [jax.ShapeDtypeStruct((b, hh, n, 2 * hd), F8)] + out_shape
    return pl.pallas_call(
        functools.partial(_mla_proj_kernel, with_q),
        grid=(b, n // tl),
        in_specs=[row(cq.shape[-1]), row(small.shape[-1]), _const_spec(gqa.shape), _const_spec(gkva.shape),
                  _const_spec(wq.shape), _const_spec(wk.shape), _const_spec(wvt.shape), _const_spec(gq.shape),
                  _const_spec(gk.shape),
                  tab, tab],
        out_specs=out_specs,
        out_shape=out_shape,
        compiler_params=_cparams(("parallel", "parallel")),
        name="mla_projections_q" if with_q else "mla_projections_kv",
    )(cq, small, gqa, gkva, wq, wk, wvt, gq, gk, cos, sin)


ATTN_SUB_COLS = 256
ATTN_LOOKAHEAD = 3
ATTN_QK_BALANCE = 4.0
ATTN_P_SHIFT = 8.0


def _attn_kernel(q_ref, kc_ref, vct_ref, kx_ref, vxt_ref, o_ref):
    tq = q_ref.shape[2]
    n_sub = tq // ATTN_SUB_COLS

    def scores(i):
        q = q_ref[0, 0, i * ATTN_SUB_COLS:(i + 1) * ATTN_SUB_COLS, :]
        return _dot_nt(kc_ref[0, 0], q), _dot_nt(kx_ref[0, 0], q)

    pending = [scores(i) for i in range(min(ATTN_LOOKAHEAD, n_sub))]
    for i in range(n_sub):
        if i + ATTN_LOOKAHEAD < n_sub:
            pending.append(scores(i + ATTN_LOOKAHEAD))
        s_c, s_x = pending.pop(0)
        s_c, s_x = s_c.astype(BF16), s_x.astype(BF16)
        m = jnp.maximum(jnp.max(s_c, axis=0, keepdims=True), jnp.max(s_x, axis=0, keepdims=True)) - ATTN_P_SHIFT
        p_c = jnp.exp2(s_c - m).astype(F8)
        p_x = jnp.exp2(s_x - m).astype(F8)
        o_t = _dot(vct_ref[0, 0], p_c) + _dot(vxt_ref[0, 0], p_x)
        hd = o_ref.shape[-1]
        l = o_t[hd:hd + 1, :]
        o_ref[0, i * ATTN_SUB_COLS:(i + 1) * ATTN_SUB_COLS, :] = (o_t[0:hd, :] / l).T.astype(o_ref.dtype)


def _attention(q, k_c, vt_c, k_x, vt_x, tq):
    b, hh, n, dq = q.shape
    n_ctx = k_c.shape[2]
    hv = vt_x.shape[2]
    hd = hv - ATTN_ONES_ROWS
    keys = lambda n_: pl.BlockSpec((1, 1, n_, dq), lambda i, j, t: (i, j, 0, 0))
    vals = lambda n_: pl.BlockSpec((1, 1, hv, n_), lambda i, j, t: (i, j, 0, 0))
    return pl.pallas_call(
        _attn_kernel,
        grid=(b, hh, n // tq),
        in_specs=[pl.BlockSpec((1, 1, tq, dq), lambda i, j, t: (i, j, t, 0)),
                  keys(n_ctx), vals(n_ctx), keys(n), vals(n)],
        out_specs=pl.BlockSpec((1, tq, hd), lambda i, j, t: (i, t, j)),
        out_shape=jax.ShapeDtypeStruct((b, n, hh * hd), BF16),
        compiler_params=_cparams(("parallel", "parallel", "parallel")),
        name="attention",
    )(q, k_c, vt_c, k_x, vt_x)


def _merge_kernel(ya_ref, yb_ref, gate_ref, x_ref, g1_ref, sh2_ref, sc2_ref, n2g_ref,
                  wa_ref, wb_ref, wo_ref, rwt_ref, xmid_ref, h2_ref, aff_ref):
    d = x_ref.shape[-1]
    ga = _sigmoid(gate_ref[0, :, 0:d].astype(F32))
    gb = _sigmoid(gate_ref[0, :, d:2 * d].astype(F32))
    mix = ga * _dot(ya_ref[0], wa_ref[...]) + gb * _dot(yb_ref[0], wb_ref[...])
    mix = _dot(mix.astype(BF16), wo_ref[...])
    xm = x_ref[0] + g1_ref[0] * mix
    xmid_ref[0] = xm
    h2 = _modulated_norm(xm, n2g_ref[...], sh2_ref[0], sc2_ref[0])
    _to_tile_major(h2_ref, h2, lead=(0,))
    rw_hi, rw_lo = _hi_lo(rwt_ref[...])
    h_hi, h_lo = _hi_lo(h2)
    logits = _dot_nt(rw_hi, h_hi) + (_dot_nt(rw_hi, h_lo) + _dot_nt(rw_lo, h_hi))
    mx = jnp.max(logits, axis=0, keepdims=True)
    ex = jnp.exp(logits - mx)
    aff_ref[0] = ex / jnp.sum(ex, axis=0, keepdims=True)


def _merge(ya, yb, gate, x, g1, sh2, sc2, n2g, wa, wb, wo, rwt, tl):
    b, n, d = x.shape
    ne = rwt.shape[0]
    row = lambda w: pl.BlockSpec((1, tl, w), lambda i, j: (i, j, 0))
    vec = pl.BlockSpec((1, 1, d), lambda i, j: (i, 0, 0))
    return pl.pallas_call(
        _merge_kernel,
        grid=(b, n // tl),
        in_specs=[row(d), row(d), row(2 * d), row(d), vec, vec, vec, _const_spec((1, d)),
                  _const_spec(wa.shape), _const_spec(wb.shape), _const_spec(wo.shape), _const_spec(rwt.shape)],
        out_specs=[row(d), pl.BlockSpec((1, tl * SUBLANES, LANES), lambda i, j: (i, j, 0)),
                   pl.BlockSpec((1, ne, tl), lambda i, j: (i, 0, j))],
        out_shape=[jax.ShapeDtypeStruct((b, n, d), F32), jax.ShapeDtypeStruct((b, n * SUBLANES, LANES), F32),
                   jax.ShapeDtypeStruct((b, ne, n), F32)],
        compiler_params=_cparams(("parallel", "parallel")),
        name="merge_router",
    )(ya, yb, gate, x, g1, sh2, sc2, n2g, wa, wb, wo, rwt)


TOPK_RANK_ROWS = 64


def _lane_cumsum(x_bf16, tri_bf16, out_ref):
    ne, n = x_bf16.shape
    carry = jnp.zeros((ne, 1), F32)
    for j in range(n // LANES):
        blk = _dot(x_bf16[:, j * LANES:(j + 1) * LANES], tri_bf16) + carry
        out_ref[:, j * LANES:(j + 1) * LANES] = blk
        carry = blk[:, LANES - 1:LANES]


def _topk_kernel(cap, aff_ref, idx_ref, gate_ref, cum_ref, affsel_ref, cumrep_ref, affrep_ref):
    aff = aff_ref[0]
    ne, n = aff.shape
    bits = pltpu.bitcast(aff, jnp.int32)

    def search(i, t):
        cand = t | (jnp.int32(1) << (30 - i))
        cnt = jnp.sum(jnp.where(bits >= cand, 1.0, 0.0), axis=-1, keepdims=True)
        return jnp.where(cnt >= cap, cand, t)

    thr = lax.fori_loop(0, 31, search, jnp.zeros((ne, 1), jnp.int32))
    gt = bits > thr
    eq = bits == thr
    need = cap - jnp.sum(jnp.where(gt, 1.0, 0.0), axis=-1, keepdims=True)
    ri = lax.broadcasted_iota(jnp.int32, (LANES, LANES), 0)
    ci = lax.broadcasted_iota(jnp.int32, (LANES, LANES), 1)
    tri = jnp.where(ri <= ci, 1.0, 0.0).astype(BF16)
    _lane_cumsum(jnp.where(eq, 1.0, 0.0).astype(BF16), tri, cum_ref)
    sel = gt | (eq & (cum_ref[...] <= need))
    affsel_ref[...] = jnp.where(sel, aff, 0.0)
    _lane_cumsum(jnp.where(sel, 1.0, 0.0).astype(BF16), tri, cum_ref)
    for ex in range(ne):
        cumrep_ref[ex] = jnp.broadcast_to(cum_ref[ex:ex + 1, :], (SUBLANES, n))
        affrep_ref[ex] = jnp.broadcast_to(affsel_ref[ex:ex + 1, :], (SUBLANES, n))

    rows = TOPK_RANK_ROWS
    lane = lax.broadcasted_iota(jnp.int32, (rows, LANES), 1)
    idx_ref[0] = jnp.zeros((cap, LANES), jnp.int32)
    gate_ref[0] = jnp.zeros((cap, LANES), F32)

    def per_expert(e, carry):
        for r0 in range(0, cap, rows):
            rank = (lax.broadcasted_iota(jnp.int32, (rows, LANES), 0) + r0).astype(F32)
            cnt = jnp.zeros((rows, LANES), F32)
            gat = jnp.zeros((rows, LANES), F32)
            for j in range(n // LANES):
                c_blk = cumrep_ref[e, 0:1, j * LANES:(j + 1) * LANES]
                a_blk = affrep_ref[e, 0:1, j * LANES:(j + 1) * LANES]
                cnt = cnt + jnp.where(c_blk <= rank, 1.0, 0.0)
                gat = gat + jnp.where(c_blk == rank + 1.0, a_blk, 0.0)
            pos = jnp.sum(cnt, axis=-1, keepdims=True).astype(jnp.int32)
            g = jnp.sum(gat, axis=-1, keepdims=True)
            idx_ref[0, r0:r0 + rows, :] = jnp.where(lane == e, pos, idx_ref[0, r0:r0 + rows, :])
            gate_ref[0, r0:r0 + rows, :] = jnp.where(lane == e, g, gate_ref[0, r0:r0 + rows, :])
        return carry

    lax.fori_loop(0, ne, per_expert, 0)


def _expert_topk(aff_t, cap):
    b, ne, n = aff_t.shape
    return pl.pallas_call(
        functools.partial(_topk_kernel, cap),
        grid=(b,),
        in_specs=[pl.BlockSpec((1, ne, n), lambda i: (i, 0, 0))],
        out_specs=[pl.BlockSpec((1, cap, LANES), lambda i: (i, 0, 0))] * 2,
        out_shape=[jax.ShapeDtypeStruct((b, cap, LANES), jnp.int32), jax.ShapeDtypeStruct((b, cap, LANES), F32)],
        scratch_shapes=[pltpu.VMEM((ne, n), F32), pltpu.VMEM((ne, n), F32),
                        pltpu.VMEM((ne, SUBLANES, n), F32), pltpu.VMEM((ne, SUBLANES, n), F32)],
        compiler_params=_cparams(("parallel",)),
        name="expert_topk",
    )(aff_t)


MOE_FF_TILE = 768
MOE_UNROLL = 8


def _moe_kernel(cap, idx_ref, gate_ref, h_ref, wg_ref, wu_ref, wd_ref, o_ref, xt_ref, x_ref, y_ref, yt_ref):
    e = pl.program_id(1)
    f = pl.program_id(2)
    nf = pl.num_programs(2)

    @pl.when((e == 0) & (f == 0))
    def _():
        o_ref[...] = jnp.zeros(o_ref.shape, o_ref.dtype)

    @pl.when(f == 0)
    def _():
        def gather(i, carry):
            for u in range(MOE_UNROLL):
                r = i * MOE_UNROLL + u
                xt_ref[_tile(r), :] = h_ref[0, _tile(idx_ref[0, 0, 0, r]), :]
            return carry
        lax.fori_loop(0, cap // MOE_UNROLL, gather, 0)
        x_ref[...] = _from_tile_major(xt_ref, cap).astype(x_ref.dtype)

    x = x_ref[...]
    hid = (_silu(_dot(x, wg_ref[0])) * _dot(x, wu_ref[0])).astype(BF16)
    part = _dot(hid, wd_ref[0])

    @pl.when(f == 0)
    def _():
        y_ref[...] = part

    @pl.when(f > 0)
    def _():
        y_ref[...] += part

    @pl.when(f == nf - 1)
    def _():
        _to_tile_major(yt_ref, y_ref[...])

        def scatter(i, carry):
            rows = [i * MOE_UNROLL + u for u in range(MOE_UNROLL)]
            toks = [idx_ref[0, 0, 0, r] for r in rows]
            new = [o_ref[0, _tile(t), :] + gate_ref[0, 0, 0, r] * yt_ref[_tile(r), :] for r, t in zip(rows, toks)]
            for t, val in zip(toks, new):
                o_ref[0, _tile(t), :] = val
            return carry
        lax.fori_loop(0, cap // MOE_UNROLL, scatter, 0)


def _moe(idx, gate, h2t, wg, wu, wd):
    b, rows, _ = h2t.shape
    ne, d, ffp = wg.shape
    cap = idx.shape[-1]
    tf = MOE_FF_TILE
    smem = lambda: pl.BlockSpec((1, 1, 1, cap), lambda i, j, k: (i, j, 0, 0), memory_space=pltpu.SMEM)
    act = lambda: pl.BlockSpec((1, rows, LANES), lambda i, j, k: (i, 0, 0), pipeline_mode=pl.Buffered(1))
    return pl.pallas_call(
        functools.partial(_moe_kernel, cap),
        grid=(b, ne, ffp // tf),
        in_specs=[smem(), smem(), act(),
                  pl.BlockSpec((1, d, tf), lambda i, j, k: (j, 0, k)),
                  pl.BlockSpec((1, d, tf), lambda i, j, k: (j, 0, k)),
                  pl.BlockSpec((1, tf, d), lambda i, j, k: (j, k, 0))],
        out_specs=act(),
        out_shape=jax.ShapeDtypeStruct((b, rows, LANES), F32),
        scratch_shapes=[pltpu.VMEM((cap * SUBLANES, LANES), F32), pltpu.VMEM((cap, d), BF16),
                        pltpu.VMEM((cap, d), F32), pltpu.VMEM((cap * SUBLANES, LANES), F32)],
        compiler_params=_cparams(("parallel", "arbitrary", "arbitrary")),
        name="expert_ffn",
    )(idx[:, :, None, :], gate[:, :, None, :], h2t, wg, wu, wd)


def _final_kernel(xm_ref, moe_ref, g_ref, o_ref):
    o_ref[0] = xm_ref[0] + g_ref[0] * _from_tile_major(moe_ref, xm_ref.shape[1], lead=(0,))


def _final(xmid, moe_t, g2, tl):
    b, n, d = xmid.shape
    row = pl.BlockSpec((1, tl, d), lambda i, j: (i, j, 0))
    return pl.pallas_call(
        _final_kernel,
        grid=(b, n // tl),
        in_specs=[row, pl.BlockSpec((1, tl * SUBLANES, LANES), lambda i, j: (i, j, 0)),
                  pl.BlockSpec((1, 1, d), lambda i, j: (i, 0, 0))],
        out_specs=row,
        out_shape=jax.ShapeDtypeStruct((b, n, d), F32),
        compiler_params=_cparams(("parallel", "parallel")),
        name="final_residual",
    )(xmid, moe_t, g2)


def _rope_tables(n_lat):
    rows = n_lat // GRID_W
    row = jnp.repeat(jnp.arange(rows), GRID_W).astype(F32)
    col = jnp.broadcast_to(jnp.arange(GRID_W), (rows, GRID_W)).reshape(-1).astype(F32)
    n_freq = MLA_ROPE // 4
    inv_freq = ROPE_BASE ** (-jnp.arange(n_freq, dtype=F32) / n_freq)
    ang_r = row[:, None] * inv_freq
    ang_c = col[:, None] * inv_freq
    ang = jnp.concatenate([ang_r, ang_r, ang_c, ang_c], axis=-1)
    zeros = jnp.zeros((n_lat, LANES - MLA_ROPE), F32)
    return jnp.concatenate([jnp.cos(ang), zeros], axis=-1), jnp.concatenate([jnp.sin(ang), zeros], axis=-1)


def _pad_lanes(v, width):
    return jnp.concatenate([v, jnp.zeros(v.shape[:-1] + (width - v.shape[-1],), v.dtype)], axis=-1)


def kernel(x, c, ctx, c_ctx, ada_w, ada_b, norm1_g, norm2_g, w_in, conv_w, a_log, dt_bias, dn_norm_g, w_out_a, q_a_norm_g, w_uq, kv_a_norm_g, w_ukv, q_norm_g, k_norm_g, w_out_b, w_o, router_w, w_gate, w_up, w_down):
    assert ada_w.shape[0] == 1, "single-layer block"
    b, n_lat, d = x.shape
    n_ctx = ctx.shape[1]
    hh, hd = N_HEADS, HEAD_DIM
    qkv_dim = 3 * hh * hd
    nb = 2 * hh

    n_cond = -(-(b + 1) // SUBLANES) * SUBLANES
    cond = jnp.concatenate([c, c_ctx[None], jnp.zeros((n_cond - b - 1, d), F32)], axis=0)
    mod = _modulation(cond, ada_w[0], ada_b[0])
    mods = [mod[:, i * d:(i + 1) * d] for i in range(6)]
    lat = lambda m: m[:b, None, :]
    cvec = lambda m: jnp.broadcast_to(m[b][None, None, :], (b, 1, d))

    o_z = qkv_dim
    o_alpha = o_z + hh * hd
    o_beta = o_alpha + nb
    o_cq = o_beta + nb
    o_ckv = o_cq + MLA_Q_RANK
    o_kr = o_ckv + MLA_KV_RANK
    o_gate = o_kr + MLA_ROPE
    w = w_in[0]
    w_big = jnp.concatenate([w[:, :o_alpha], w[:, o_gate:], w[:, o_cq:o_ckv]], axis=1).astype(BF16)
    small_w = _pad_lanes(w[:, o_ckv:o_gate], MLA_KV_RANK + LANES).astype(BF16)
    w_ab_t = jnp.transpose(w[:, o_alpha:o_cq]).astype(BF16)
    gate_pad = jnp.zeros((nb, 1), F32)
    alog = jnp.concatenate([a_log[0].reshape(nb, 1), gate_pad], axis=0)
    dtb = jnp.concatenate([dt_bias[0].reshape(nb, 1), gate_pad], axis=0)
    widths = (qkv_dim, hh * hd, N_BRANCHES * d, MLA_Q_RANK)
    g1 = norm1_g[0][None]
    qkv_x, z_x, gt_x, cq_x, sm_x, dg_x = _in_projection(x, lat(mods[0]), lat(mods[1]), g1, w_big, small_w, w_ab_t,
                                                        alog, dtb, widths, 512)
    qkv_c, _, _, cq_c, sm_c, dg_c = _in_projection(ctx, cvec(mods[0]), cvec(mods[1]), g1, w_big, small_w, w_ab_t,
                                                   alog, dtb, widths, min(n_ctx, 512))

    y_a = _deltanet(qkv_x, qkv_c, conv_w[0], dg_x, dg_c, z_x, dn_norm_g[0][None])

    wq = w_uq[0].reshape(MLA_Q_RANK, hh, MLA_QK_DIM)
    wq = jnp.concatenate([wq[:, :, :hd].reshape(MLA_Q_RANK, hh * hd),
                          _pad_lanes(wq[:, :, hd:], hd).reshape(MLA_Q_RANK, hh * hd)], axis=1).astype(BF16)
    wkv = w_ukv[0].reshape(MLA_KV_RANK, hh, 2 * hd)
    wk = wkv[:, :, :hd].reshape(MLA_KV_RANK, hh * hd).astype(BF16)
    wvt = jnp.transpose(wkv[:, :, hd:].reshape(MLA_KV_RANK, hh * hd)).astype(BF16)
    gq = _pad_lanes(q_norm_g[0], 2 * hd)[None]
    gk = _pad_lanes(k_norm_g[0], 2 * hd)[None]
    gqa, gkva = q_a_norm_g[0][None], kv_a_norm_g[0][None]
    cos, sin = _rope_tables(n_lat)
    cos_c = _pad_lanes(jnp.ones((n_ctx, MLA_ROPE), F32), LANES)
    sin_c = jnp.zeros((n_ctx, LANES), F32)
    q_x, k_x, v_x = _mla_projections(cq_x, sm_x, gqa, gkva, wq, wk, wvt, gq, gk, cos, sin, True, 512)
    k_c, v_c = _mla_projections(cq_c, sm_c, gqa, gkva, wq, wk, wvt, gq, gk, cos_c, sin_c, False, min(n_ctx, 512))
    y_b = _attention(q_x, k_c, v_c, k_x, v_x, min(n_lat, 1024))

    rwt = jnp.transpose(router_w[0])
    x_mid, h2, aff_t = _merge(y_a, y_b, gt_x, x, lat(mods[2]), lat(mods[3]), lat(mods[4]), norm2_g[0][None],
                              w_out_a[0].astype(BF16), w_out_b[0].astype(BF16), w_o[0].astype(BF16), rwt, 512)

    cap = EC_CAPACITY * n_lat // N_EXPERTS
    idx, gate = _expert_topk(aff_t, cap)
    idx = jnp.transpose(idx[:, :, :N_EXPERTS], (0, 2, 1))
    gate = jnp.transpose(gate[:, :, :N_EXPERTS], (0, 2, 1))
    ff = w_gate.shape[-1]
    ffp = -(-ff // MOE_FF_TILE) * MOE_FF_TILE
    wg = _pad_lanes(w_gate[0], ffp).astype(BF16)
    wu = _pad_lanes(w_up[0], ffp).astype(BF16)
    wd = jnp.concatenate([w_down[0], jnp.zeros((N_EXPERTS, ffp - ff, d), F32)], axis=1).astype(BF16)
    moe = _moe(idx, gate, h2, wg, wu, wd)
    return _final(x_mid, moe, lat(mods[5]), 512)
```

```python
import functools
import math

import jax
import jax.numpy as jnp
from jax import lax
from jax.experimental import pallas as pl
from jax.experimental.pallas import tpu as pltpu

F32 = jnp.float32
BF16 = jnp.bfloat16
F8 = jnp.float8_e4m3fn
EPS = 1e-6

N_HEADS = 8
HEAD_DIM = 128
DN_CONV = 5
DN_CHUNK = 64
MLA_Q_RANK = 384
MLA_KV_RANK = 256
MLA_ROPE = 64
MLA_QK_DIM = HEAD_DIM + MLA_ROPE
GRID_W = 64
ROPE_BASE = 10000.0
N_EXPERTS = 16
EC_CAPACITY = 2
N_BRANCHES = 2

LANES = 128
SUBLANES = 8
VMEM_LIMIT_BYTES = 56 * 1024 * 1024

HI = lax.Precision.HIGHEST


def _cparams(sem):
    return pltpu.CompilerParams(dimension_semantics=sem, vmem_limit_bytes=VMEM_LIMIT_BYTES)


def _dot(a, b, precision=None):
    return jnp.dot(a, b, preferred_element_type=F32, precision=precision)


def _dot_nt(a, b, precision=None):
    return lax.dot_general(a, b, (((1,), (1,)), ((), ())), preferred_element_type=F32, precision=precision)


def _dot_tn(a, b):
    return lax.dot_general(a, b, (((0,), (0,)), ((), ())), preferred_element_type=F32)


def _sigmoid(x):
    return 1.0 / (1.0 + jnp.exp(-x))


def _silu(x):
    return x * _sigmoid(x)


def _const_spec(shape):
    nd = len(shape)
    return pl.BlockSpec(shape, lambda *_: (0,) * nd, pipeline_mode=pl.Buffered(1))


def _to_tile_major(ref, x, lead=()):
    rows = x.shape[0]
    for s_ in range(SUBLANES):
        ref[lead + (pl.ds(s_, rows, stride=SUBLANES), slice(None))] = x[:, s_ * LANES:(s_ + 1) * LANES]


def _from_tile_major(ref, rows, lead=()):
    return jnp.concatenate([ref[lead + (pl.ds(s_, rows, stride=SUBLANES), slice(None))] for s_ in range(SUBLANES)],
                           axis=1)


def _tile(i):
    return pl.ds(pl.multiple_of(i * SUBLANES, SUBLANES), SUBLANES)


def _mod_kernel(c_ref, w_ref, b_ref, o_ref):
    c = c_ref[...]
    o_ref[...] = _dot(_silu(c), w_ref[...], precision=HI) + b_ref[...]


def _modulation(cond, ada_w, ada_b):
    n, d = cond.shape
    n_out = ada_w.shape[1]
    tn = d
    return pl.pallas_call(
        _mod_kernel,
        grid=(n_out // tn,),
        in_specs=[pl.BlockSpec((n, d), lambda j: (0, 0)),
                  pl.BlockSpec((d, tn), lambda j: (0, j)),
                  pl.BlockSpec((1, tn), lambda j: (0, j))],
        out_specs=pl.BlockSpec((n, tn), lambda j: (0, j)),
        out_shape=jax.ShapeDtypeStruct((n, n_out), F32),
        compiler_params=_cparams(("parallel",)),
        name="modulation",
    )(cond, ada_w, ada_b.reshape(1, n_out))


def _modulated_norm(x, g, shift, scale):
    ms = jnp.mean(x * x, axis=-1, keepdims=True)
    return (x * lax.rsqrt(ms + EPS)) * g * (1.0 + scale) + shift


def _inproj_kernel(widths, x_ref, shift_ref, scale_ref, g_ref, wb_ref, ws_ref, wab_ref, alog_ref, dtb_ref,
                   *out_refs):
    big_refs, small_ref, gates_ref = out_refs[:-2], out_refs[-2], out_refs[-1]
    h = _modulated_norm(x_ref[0], g_ref[...], shift_ref[0], scale_ref[0]).astype(BF16)
    off = 0
    for o_ref, width in zip(big_refs, widths):
        for c0 in range(0, width, 1024):
            c1 = min(c0 + 1024, width)
            o_ref[0, :, c0:c1] = _dot(h, wb_ref[:, off + c0:off + c1]).astype(o_ref.dtype)
        off += width
    small_ref[0] = _dot(h, ws_ref[...])
    ab = _dot_nt(wab_ref[...], h)
    row_id = lax.broadcasted_iota(jnp.int32, ab.shape, 0)
    xs = ab + dtb_ref[...]
    softplus = jnp.maximum(xs, 0.0) + jnp.log1p(jnp.exp(-jnp.abs(xs)))
    gates_ref[0] = jnp.where(row_id < 2 * N_HEADS, -jnp.exp(alog_ref[...]) * softplus, _sigmoid(ab))


def _in_projection(x, shift, scale, g, w_big, w_small, w_ab_t, alog, dtb, widths, tl):
    b, n, d = x.shape
    row = lambda w: pl.BlockSpec((1, tl, w), lambda i, j: (i, j, 0))
    vec = pl.BlockSpec((1, 1, d), lambda i, j: (i, 0, 0))
    n_gate = w_ab_t.shape[0]
    outs = [jax.ShapeDtypeStruct((b, n, w), BF16) for w in widths]
    outs.append(jax.ShapeDtypeStruct((b, n, w_small.shape[1]), F32))
    outs.append(jax.ShapeDtypeStruct((b, n_gate, n), F32))
    return pl.pallas_call(
        functools.partial(_inproj_kernel, widths),
        grid=(b, n // tl),
        in_specs=[row(d), vec, vec, _const_spec((1, d)), _const_spec(w_big.shape), _const_spec(w_small.shape),
                  _const_spec(w_ab_t.shape), _const_spec(alog.shape), _const_spec(dtb.shape)],
        out_specs=[row(w) for w in widths] + [row(w_small.shape[1]),
                                              pl.BlockSpec((1, n_gate, tl), lambda i, j: (i, 0, j))],
        out_shape=outs,
        compiler_params=_cparams(("parallel", "parallel")),
        name="in_projection",
    )(x, shift, scale, g, w_big, w_small, w_ab_t, alog, dtb)


CONV_ROWS = 512
CONV_PAD = 8


def _conv_fill(raw_ref, pad_ref, n_tok):
    zeros = jnp.zeros((CONV_PAD, LANES), F32)
    pad_ref[0:CONV_PAD, :] = zeros
    pad_ref[CONV_PAD + n_tok:2 * CONV_PAD + n_tok, :] = zeros
    pad_ref[CONV_PAD:CONV_PAD + n_tok, :] = raw_ref[0].astype(F32)


def _conv_block(pad_ref, w, r0, rows, normalize):
    half = DN_CONV // 2
    acc = None
    for j in range(DN_CONV):
        term = pad_ref[pl.ds(r0 + (CONV_PAD + j - half), rows), :] * w[j:j + 1, :]
        acc = term if acc is None else acc + term
    y = _silu(acc)
    if normalize:
        y = y * lax.rsqrt(jnp.sum(y * y, axis=-1, keepdims=True) + EPS)
    return y


def _conv_silu(raw_ref, w_ref, pad_ref, out_ref, n_tok, normalize):
    _conv_fill(raw_ref, pad_ref, n_tok)
    w = w_ref[...]
    rb = min(CONV_ROWS, n_tok)
    for r0 in range(0, n_tok, rb):
        out_ref[r0:r0 + rb, :] = _conv_block(pad_ref, w, r0, rb, normalize)


def _dn_masks():
    n = 2 * DN_CHUNK
    ri = lax.broadcasted_iota(jnp.int32, (n, n), 0)
    ci = lax.broadcasted_iota(jnp.int32, (n, n), 1)
    top = ri < DN_CHUNK
    same = (ri // DN_CHUNK) == (ci // DN_CHUNK)
    sgn = jnp.where(top, 1, -1)
    delta = (ri - ci) * sgn
    incl = same & (delta >= 0)
    strict = same & (delta > 0)
    incl_t = same & (delta <= 0)
    eye = ri == ci
    return top, incl, strict, incl_t, eye


DN_INV_BLOCK = 8
DN_PREP_GROUP = 16


def _hi_lo(a_f32):
    hi = a_f32.astype(BF16)
    return hi, (a_f32 - hi.astype(F32)).astype(BF16)


def _split_dot(a_f32, b_bf16):
    hi, lo = _hi_lo(a_f32)
    return _dot(hi, b_bf16) + _dot(lo, b_bf16)


def _bdot(a_f32, b_f32):
    return _dot(a_f32.astype(BF16), b_f32.astype(BF16))


def _dn_gate_rows(c0, group, gt_ref):
    n = 2 * DN_CHUNK
    t0 = pl.multiple_of(c0 * DN_CHUNK, n)
    lane = lax.broadcasted_iota(jnp.int32, (1, n), 1)
    rows = [[gt_ref[0, kind, dr, 0, :, pl.ds(t0, group * DN_CHUNK)] for dr in range(2)] for kind in range(2)]
    out = []
    for j in range(group):
        p, odd = divmod(j, 2)
        both = []
        for kind in range(2):
            fwd = rows[kind][0][:, p * n:(p + 1) * n]
            bwd = rows[kind][1][:, p * n:(p + 1) * n]
            if odd:
                both.append(jnp.where(lane < DN_CHUNK, pltpu.roll(fwd, DN_CHUNK, axis=1), bwd))
            else:
                both.append(jnp.where(lane < DN_CHUNK, fwd, pltpu.roll(bwd, DN_CHUNK, axis=1)))
        out.append(jnp.concatenate(both, axis=0))
    return out


def _dn_prep_load(c, q_ref, k_ref, v_ref):
    r0 = pl.multiple_of(c * DN_CHUNK, DN_CHUNK)
    return q_ref[pl.ds(r0, DN_CHUNK), :], k_ref[pl.ds(r0, DN_CHUNK), :], v_ref[pl.ds(r0, DN_CHUNK), :]


def _dn_prep_compute(q, k, v, gb):
    n = 2 * DN_CHUNK
    top, incl, strict, incl_t, eye = _dn_masks()
    g_row = jnp.broadcast_to(gb[0:1, :], (SUBLANES, n))
    beta_rows = jnp.broadcast_to(gb[1:2, :], (n, n))
    kst = jnp.concatenate([k, k], axis=0)
    qst = jnp.concatenate([q, q], axis=0)
    vst = jnp.concatenate([v, v], axis=0)
    kst_b = kst.astype(BF16)

    gc_row = _split_dot(g_row, jnp.where(incl_t, 1.0, 0.0).astype(BF16))[0:1, :]
    kk = _dot_nt(kst_b, kst_b)
    qk = _dot_nt(qst.astype(BF16), kst_b)
    yield
    c2 = jnp.broadcast_to(gc_row, (n, n))
    c1 = c2.T
    beta_c = beta_rows.T
    tot = jnp.where(top, c1[DN_CHUNK - 1:DN_CHUNK, :], c1[DN_CHUNK:DN_CHUNK + 1, :])
    decay = jnp.exp(jnp.where(incl, c1 - c2, -jnp.inf))
    e_gc = jnp.exp(c1)
    e_rest = jnp.exp(tot - c1)
    scale = HEAD_DIM ** -0.5

    m = jnp.where(strict, kk * beta_c * decay, 0.0)
    ri = lax.broadcasted_iota(jnp.int32, (n, n), 0)
    ci = lax.broadcasted_iota(jnp.int32, (n, n), 1)
    blk = DN_INV_BLOCK
    dg = jnp.where((ri // blk) == (ci // blk), m, 0.0)
    t = jnp.where(eye, 1.0, 0.0) - dg
    dp = _bdot(dg, dg)
    yield
    for it in range(blk.bit_length() - 2):
        t = t + _bdot(t, dp)
        if it + 1 < blk.bit_length() - 2:
            dp = _bdot(dp, dp)
        yield
    while blk < DN_CHUNK:
        off = jnp.where(((ri // (2 * blk)) == (ci // (2 * blk))) & ((ri // blk) != (ci // blk)), m, 0.0)
        ta = _bdot(t, off)
        yield
        t = t - _bdot(ta, t)
        yield
        blk *= 2

    kb = kst * beta_c
    rhs = jnp.concatenate([vst * beta_c, kb * e_gc], axis=1)
    uw = _bdot(t, rhs)
    yield
    d = HEAD_DIM
    u_b = uw[:, 0:d].astype(BF16)
    w_b = uw[:, d:2 * d].astype(BF16)
    kd = (kst * e_rest).astype(BF16)
    qkm = (jnp.where(incl, qk * decay, 0.0) * scale).astype(BF16)
    ri2 = lax.broadcasted_iota(jnp.int32, (n, 2 * d), 0)
    ci2 = lax.broadcasted_iota(jnp.int32, (n, 2 * d), 1)
    own_dir = (ri2 // DN_CHUNK) == (ci2 // d)
    zero = jnp.zeros((), BF16)
    wu_bd = jnp.concatenate([jnp.where(own_dir, jnp.concatenate([w_b, w_b], axis=1), zero),
                             jnp.where(own_dir, jnp.concatenate([u_b, u_b], axis=1), zero)], axis=1)
    kw = _dot_tn(kd, wu_bd)
    qw = _dot(qkm, jnp.concatenate([w_b, u_b], axis=1))
    yield
    return ((-kw[:, 0:2 * d]).astype(BF16),
            kw[:, 2 * d:4 * d],
            (qst * e_gc * scale - qw[:, 0:d]).astype(BF16),
            qw[:, d:2 * d],
            jnp.exp(tot)[DN_CHUNK - 4:DN_CHUNK + 4, :])


def _dn_prep_group(c0, group, in_refs, out_refs):
    *qkv_refs, gt_ref = in_refs
    gates = _dn_gate_rows(c0, group, gt_ref)
    gens = [_dn_prep_compute(*_dn_prep_load(c0 + j, *qkv_refs), gates[j]) for j in range(group)]
    results = [None] * group
    while any(r is None for r in results):
        for j, gen in enumerate(gens):
            try:
                next(gen)
            except StopIteration as done:
                results[j] = done.value
    for j, res in enumerate(results):
        for ref, val in zip(out_refs, res):
            ref[c0 + j] = val


def _dn_scan_step(cf, cb, a_ref, b_ref, q_ref, o0_ref, ar_ref, s_ref, of_ref, ob_ref):
    h = DN_CHUNK
    d = HEAD_DIM
    s = s_ref[...]
    s_b = s.astype(BF16)
    zero = jnp.zeros((d, d), BF16)
    s_bd = jnp.concatenate([jnp.concatenate([s_b[:, 0:d], zero], axis=1),
                            jnp.concatenate([zero, s_b[:, d:2 * d]], axis=1)], axis=0)
    a_sel = jnp.concatenate([a_ref[cf, :, 0:d], a_ref[cb, :, d:2 * d]], axis=1)
    zq = jnp.zeros((h, d), BF16)
    q_rows = jnp.concatenate([jnp.concatenate([q_ref[cf, 0:h, :], zq], axis=1),
                              jnp.concatenate([zq, q_ref[cb, h:2 * h, :]], axis=1)], axis=0)
    r = _dot(jnp.concatenate([a_sel, q_rows], axis=0), s_bd)
    if of_ref is not None:
        of_ref[pl.ds(pl.multiple_of(cf * h, h), h), :] = r[d:d + h, 0:d] + o0_ref[cf, 0:h, :]
        ob_ref[pl.ds(pl.multiple_of(cb * h, h), h), :] = r[d + h:d + 2 * h, d:2 * d] + o0_ref[cb, h:2 * h, :]
    b_sel = jnp.concatenate([b_ref[cf, :, 0:d], b_ref[cb, :, d:2 * d]], axis=1)
    a = jnp.concatenate([jnp.broadcast_to(ar_ref[cf, 0:1, :], (d, d)),
                         jnp.broadcast_to(ar_ref[cb, 4:5, :], (d, d))], axis=1)
    s_ref[...] = s * a + r[0:d, :] + b_sel


def _dn_kernel(n_lat, n_ctx,
               qx_ref, kx_ref, vx_ref, qc_ref, kc_ref, vc_ref, wq_ref, wk_ref, wv_ref,
               gbx_ref, gbc_ref, z_ref, ng_ref, y_ref,
               pad_ref, q_s, k_s, v_s, qc_s, kc_s, vc_s,
               a_x, b_x, qp_x, o0_x, ar_x, a_c, b_c, qp_c, o0_c, ar_c,
               s_ref, of_ref, ob_ref):
    nc_x = n_lat // DN_CHUNK
    nc_c = n_ctx // DN_CHUNK

    _conv_silu(qc_ref, wq_ref, pad_ref, qc_s, n_ctx, True)
    _conv_silu(kc_ref, wk_ref, pad_ref, kc_s, n_ctx, True)
    _conv_silu(vc_ref, wv_ref, pad_ref, vc_s, n_ctx, False)
    _conv_silu(qx_ref, wq_ref, pad_ref, q_s, n_lat, True)
    _conv_silu(kx_ref, wk_ref, pad_ref, k_s, n_lat, True)
    _conv_silu(vx_ref, wv_ref, pad_ref, v_s, n_lat, False)
    grp_c = math.gcd(nc_c, DN_PREP_GROUP)
    grp_x = math.gcd(nc_x, DN_PREP_GROUP)
    assert grp_c % 2 == 0 and grp_x % 2 == 0, "a group's tokens must cover whole 128-lane tiles"
    assert nc_x % 2 == 0

    def prep_c(i, carry):
        _dn_prep_group(i * grp_c, grp_c, (qc_s, kc_s, vc_s, gbc_ref), (a_c, b_c, qp_c, o0_c, ar_c))
        return carry

    def prep_x(i, carry):
        _dn_prep_group(i * grp_x, grp_x, (q_s, k_s, v_s, gbx_ref), (a_x, b_x, qp_x, o0_x, ar_x))
        return carry

    lax.fori_loop(0, nc_c // grp_c, prep_c, 0)
    lax.fori_loop(0, nc_x // grp_x, prep_x, 0)

    s_ref[...] = jnp.zeros(s_ref.shape, F32)

    def scan_c(i, carry):
        _dn_scan_step(i, nc_c - 1 - i, a_c, b_c, qp_c, o0_c, ar_c, s_ref, None, None)
        return carry

    def scan_x(i, carry):
        _dn_scan_step(i, nc_x - 1 - i, a_x, b_x, qp_x, o0_x, ar_x, s_ref, of_ref, ob_ref)
        return carry

    def finish(c):
        start = c * DN_CHUNK
        rows = pl.ds(start if isinstance(c, int) else pl.multiple_of(start, DN_CHUNK), DN_CHUNK)
        o = of_ref[rows, :] + ob_ref[rows, :]
        ms = jnp.mean(o * o, axis=-1, keepdims=True)
        y = (o * lax.rsqrt(ms + EPS)) * ng_ref[...] * _silu(z_ref[0, rows, :].astype(F32))
        y_ref[0, rows, :] = y.astype(y_ref.dtype)

    def scan_x_finish(i, carry):
        finish(i - 1)
        finish(nc_x - i)
        scan_x(i, carry)
        return carry

    half = nc_x // 2
    lax.fori_loop(0, nc_c, scan_c, 0)
    lax.fori_loop(0, half + 1, scan_x, 0)
    lax.fori_loop(half + 1, nc_x, scan_x_finish, 0)
    finish(nc_x - 1)
    finish(0)


def _deltanet(qkv_x, qkv_c, conv_w, gt_x, gt_c, z, norm_g):
    b, n_lat, _ = qkv_x.shape
    n_ctx = qkv_c.shape[1]
    gbx = gt_x.reshape(b, 2, 2, N_HEADS, 1, n_lat)
    gbc = gt_c.reshape(b, 2, 2, N_HEADS, 1, n_ctx)
    hh = N_HEADS
    d = HEAD_DIM
    nc_x, nc_c = n_lat // DN_CHUNK, n_ctx // DN_CHUNK
    tok = lambda n, off: pl.BlockSpec((1, n, d), lambda i, j: (i, 0, off + j))
    cw = lambda off: pl.BlockSpec((DN_CONV, d), lambda i, j: (0, off + j))
    gspec = lambda n: pl.BlockSpec((1, 2, 2, 1, 1, n), lambda i, j: (i, 0, 0, j, 0, 0))
    chunk_scratch = lambda nc: [
        pltpu.VMEM((nc, d, 2 * d), BF16),
        pltpu.VMEM((nc, d, 2 * d), F32),
        pltpu.VMEM((nc, 2 * DN_CHUNK, d), BF16),
        pltpu.VMEM((nc, 2 * DN_CHUNK, d), F32),
        pltpu.VMEM((nc, SUBLANES, 2 * DN_CHUNK), F32),
    ]
    return pl.pallas_call(
        functools.partial(_dn_kernel, n_lat, n_ctx),
        grid=(b, hh),
        in_specs=[tok(n_lat, 0), tok(n_lat, hh), tok(n_lat, 2 * hh),
                  tok(n_ctx, 0), tok(n_ctx, hh), tok(n_ctx, 2 * hh),
                  cw(0), cw(hh), cw(2 * hh),
                  gspec(n_lat), gspec(n_ctx),
                  tok(n_lat, 0), _const_spec((1, d))],
        out_specs=tok(n_lat, 0),
        out_shape=jax.ShapeDtypeStruct((b, n_lat, hh * d), BF16),
        scratch_shapes=[pltpu.VMEM((n_lat + 2 * CONV_PAD, d), F32)]
        + [pltpu.VMEM((n_lat, d), F32)] * 3 + [pltpu.VMEM((n_ctx, d), F32)] * 3
        + chunk_scratch(nc_x) + chunk_scratch(nc_c)
        + [pltpu.VMEM((d, 2 * d), F32), pltpu.VMEM((n_lat, d), F32), pltpu.VMEM((n_lat, d), F32)],
        compiler_params=_cparams(("parallel", "parallel")),
        name="deltanet",
    )(qkv_x, qkv_x, qkv_x, qkv_c, qkv_c, qkv_c, conv_w, conv_w, conv_w, gbx, gbc, z, norm_g)


ATTN_ONES_ROWS = 32


def _rope(x, cos, sin):
    lane = lax.broadcasted_iota(jnp.int32, x.shape, 1)
    quarter = MLA_ROPE // 4
    rot = jnp.where((lane // quarter) % 2 == 0,
                    -pltpu.roll(x, LANES - quarter, axis=1), pltpu.roll(x, quarter, axis=1))
    return x * cos + rot * sin


def _rms(x, n):
    return lax.rsqrt(jnp.sum(x * x, axis=-1, keepdims=True) / n + EPS)


def _mla_proj_kernel(with_q, cq_ref, small_ref, gqa_ref, gkva_ref, wq_ref, wk_ref, wvt_ref, gq_ref, gk_ref,
                     cos_ref, sin_ref, *out_refs):
    hd = HEAD_DIM
    hh = N_HEADS
    cos, sin = cos_ref[...], sin_ref[...]
    small = small_ref[0]
    ckv = small[:, 0:MLA_KV_RANK]
    kpe = small[:, MLA_KV_RANK:MLA_KV_RANK + LANES]
    lane = lax.broadcasted_iota(jnp.int32, kpe.shape, 1)
    kpe = jnp.where(lane < MLA_ROPE, kpe, 0.0)
    kpe_ssq = jnp.sum(kpe * kpe, axis=-1, keepdims=True)
    ckvn = (ckv * _rms(ckv, MLA_KV_RANK) * gkva_ref[...]).astype(BF16)
    kf = _dot(ckvn, wk_ref[...])
    vt = _dot_nt(wvt_ref[...], ckvn)
    if with_q:
        q_ref, k_ref, v_ref = out_refs
        cq = cq_ref[0].astype(F32)
        cqn = (cq * _rms(cq, MLA_Q_RANK) * gqa_ref[...]).astype(BF16)
        qf = _dot(cqn, wq_ref[...])
        q_scale = MLA_QK_DIM ** -0.5 * math.log2(math.e) * ATTN_QK_BALANCE
    else:
        k_ref, v_ref = out_refs
    gq, gk = gq_ref[...], gk_ref[...]
    kpe_rot = _rope(kpe * gk[:, hd:2 * hd], cos, sin)
    for h in range(hh):
        if with_q:
            qn = qf[:, h * hd:(h + 1) * hd]
            qp = qf[:, (hh + h) * hd:(hh + h + 1) * hd]
            ssq = jnp.sum(qn * qn + qp * qp, axis=-1, keepdims=True)
            r = lax.rsqrt(ssq / MLA_QK_DIM + EPS)
            q_ref[0, h, :, 0:hd] = (qn * r * gq[:, 0:hd] * q_scale).astype(q_ref.dtype)
            qr = qf[:, (2 * hh + h) * hd:(2 * hh + h + 1) * hd]
            q_pe = qp * (gq[:, hd:2 * hd] * cos) + qr * (gq[:, 2 * hd:3 * hd] * sin)
            q_ref[0, h, :, hd:2 * hd] = (q_pe * (r * q_scale)).astype(q_ref.dtype)
        kn = kf[:, h * hd:(h + 1) * hd]
        r = lax.rsqrt((jnp.sum(kn * kn, axis=-1, keepdims=True) + kpe_ssq) / MLA_QK_DIM + EPS)
        rk = r * (1.0 / ATTN_QK_BALANCE)
        k_ref[0, h, :, 0:hd] = (kn * rk * gk[:, 0:hd]).astype(k_ref.dtype)
        k_ref[0, h, :, hd:2 * hd] = (kpe_rot * rk).astype(k_ref.dtype)
        v_ref[0, h, 0:hd, :] = vt[h * hd:(h + 1) * hd, :].astype(v_ref.dtype)
        v_ref[0, h, hd:hd + ATTN_ONES_ROWS, :] = jnp.ones((ATTN_ONES_ROWS, vt.shape[1]), v_ref.dtype)


def _mla_projections(cq, small, gqa, gkva, wq, wk, wvt, gq, gk, cos, sin, with_q, tl):
    b, n, _ = small.shape
    hh, hd = N_HEADS, HEAD_DIM
    row = lambda w: pl.BlockSpec((1, tl, w), lambda i, j: (i, j, 0))
    head = lambda w: pl.BlockSpec((1, hh, tl, w), lambda i, j: (i, 0, j, 0))
    tab = pl.BlockSpec((tl, LANES), lambda i, j: (j, 0))
    hv = hd + ATTN_ONES_ROWS
    out_specs = [head(2 * hd), pl.BlockSpec((1, hh, hv, tl), lambda i, j: (i, 0, 0, j))]
    out_shape = [jax.ShapeDtypeStruct((b, hh, n, 2 * hd), F8), jax.ShapeDtypeStruct((b, hh, hv, n), F8)]
    if with_q:
        out_specs = [head(2 * hd)] + out_specs
        out_shape = [jax.ShapeDtypeStruct((b, hh, n, 2 * hd), F8)] + out_shape
    return pl.pallas_call(
        functools.partial(_mla_proj_kernel, with_q),
        grid=(b, n // tl),
        in_specs=[row(cq.shape[-1]), row(small.shape[-1]), _const_spec(gqa.shape), _const_spec(gkva.shape),
                  _const_spec(wq.shape), _const_spec(wk.shape), _const_spec(wvt.shape), _const_spec(gq.shape),
                  _const_spec(gk.shape),
                  tab, tab],
        out_specs=out_specs,
        out_shape=out_shape,
        compiler_params=_cparams(("parallel", "parallel")),
        name="mla_projections_q" if with_q else "mla_projections_kv",
    )(cq, small, gqa, gkva, wq, wk, wvt, gq, gk, cos, sin)


ATTN_SUB_COLS = 256
ATTN_LOOKAHEAD = 3
ATTN_QK_BALANCE = 4.0
ATTN_P_SHIFT = 8.0


def _attn_kernel(q_ref, kc_ref, vct_ref, kx_ref, vxt_ref, o_ref):
    tq = q_ref.shape[2]
    n_sub = tq // ATTN_SUB_COLS

    def scores(i):
        q = q_ref[0, 0, i * ATTN_SUB_COLS:(i + 1) * ATTN_SUB_COLS, :]
        return _dot_nt(kc_ref[0, 0], q), _dot_nt(kx_ref[0, 0], q)

    pending = [scores(i) for i in range(min(ATTN_LOOKAHEAD, n_sub))]
    for i in range(n_sub):
        if i + ATTN_LOOKAHEAD < n_sub:
            pending.append(scores(i + ATTN_LOOKAHEAD))
        s_c, s_x = pending.pop(0)
        s_c, s_x = s_c.astype(BF16), s_x.astype(BF16)
        m = jnp.maximum(jnp.max(s_c, axis=0, keepdims=True), jnp.max(s_x, axis=0, keepdims=True)) - ATTN_P_SHIFT
        p_c = jnp.exp2(s_c - m).astype(F8)
        p_x = jnp.exp2(s_x - m).astype(F8)
        o_t = _dot(vct_ref[0, 0], p_c) + _dot(vxt_ref[0, 0], p_x)
        hd = o_ref.shape[-1]
        l = o_t[hd:hd + 1, :]
        o_ref[0, i * ATTN_SUB_COLS:(i + 1) * ATTN_SUB_COLS, :] = (o_t[0:hd, :] / l).T.astype(o_ref.dtype)


def _attention(q, k_c, vt_c, k_x, vt_x, tq):
    b, hh, n, dq = q.shape
    n_ctx = k_c.shape[2]
    hv = vt_x.shape[2]
    hd = hv - ATTN_ONES_ROWS
    keys = lambda n_: pl.BlockSpec((1, 1, n_, dq), lambda i, j, t: (i, j, 0, 0))
    vals = lambda n_: pl.BlockSpec((1, 1, hv, n_), lambda i, j, t: (i, j, 0, 0))
    return pl.pallas_call(
        _attn_kernel,
        grid=(b, hh, n // tq),
        in_specs=[pl.BlockSpec((1, 1, tq, dq), lambda i, j, t: (i, j, t, 0)),
                  keys(n_ctx), vals(n_ctx), keys(n), vals(n)],
        out_specs=pl.BlockSpec((1, tq, hd), lambda i, j, t: (i, t, j)),
        out_shape=jax.ShapeDtypeStruct((b, n, hh * hd), BF16),
        compiler_params=_cparams(("parallel", "parallel", "parallel")),
        name="attention",
    )(q, k_c, vt_c, k_x, vt_x)


def _merge_kernel(ya_ref, yb_ref, gate_ref, x_ref, g1_ref, sh2_ref, sc2_ref, n2g_ref,
                  wa_ref, wb_ref, wo_ref, rwt_ref, xmid_ref, h2_ref, aff_ref):
    d = x_ref.shape[-1]
    ga = _sigmoid(gate_ref[0, :, 0:d].astype(F32))
    gb = _sigmoid(gate_ref[0, :, d:2 * d].astype(F32))
    mix = ga * _dot(ya_ref[0], wa_ref[...]) + gb * _dot(yb_ref[0], wb_ref[...])
    mix = _dot(mix.astype(BF16), wo_ref[...])
    xm = x_ref[0] + g1_ref[0] * mix
    xmid_ref[0] = xm
    h2 = _modulated_norm(xm, n2g_ref[...], sh2_ref[0], sc2_ref[0])
    _to_tile_major(h2_ref, h2, lead=(0,))
    rw_hi, rw_lo = _hi_lo(rwt_ref[...])
    h_hi, h_lo = _hi_lo(h2)
    logits = _dot_nt(rw_hi, h_hi) + (_dot_nt(rw_hi, h_lo) + _dot_nt(rw_lo, h_hi))
    mx = jnp.max(logits, axis=0, keepdims=True)
    ex = jnp.exp(logits - mx)
    aff_ref[0] = ex / jnp.sum(ex, axis=0, keepdims=True)


def _merge(ya, yb, gate, x, g1, sh2, sc2, n2g, wa, wb, wo, rwt, tl):
    b, n, d = x.shape
    ne = rwt.shape[0]
    row = lambda w: pl.BlockSpec((1, tl, w), lambda i, j: (i, j, 0))
    vec = pl.BlockSpec((1, 1, d), lambda i, j: (i, 0, 0))
    return pl.pallas_call(
        _merge_kernel,
        grid=(b, n // tl),
        in_specs=[row(d), row(d), row(2 * d), row(d), vec, vec, vec, _const_spec((1, d)),
                  _const_spec(wa.shape), _const_spec(wb.shape), _const_spec(wo.shape), _const_spec(rwt.shape)],
        out_specs=[row(d), pl.BlockSpec((1, tl * SUBLANES, LANES), lambda i, j: (i, j, 0)),
                   pl.BlockSpec((1, ne, tl), lambda i, j: (i, 0, j))],
        out_shape=[jax.ShapeDtypeStruct((b, n, d), F32), jax.ShapeDtypeStruct((b, n * SUBLANES, LANES), F32),
                   jax.ShapeDtypeStruct((b, ne, n), F32)],
        compiler_params=_cparams(("parallel", "parallel")),
        name="merge_router",
    )(ya, yb, gate, x, g1, sh2, sc2, n2g, wa, wb, wo, rwt)


TOPK_RANK_LANES = 128


def _lane_cumsum(x_bf16, tri_bf16, out_ref, block_ref=None):
    ne, n = x_bf16.shape
    carry = jnp.zeros((ne, 1), F32)
    for j in range(n // LANES):
        blk = _dot(x_bf16[:, j * LANES:(j + 1) * LANES], tri_bf16) + carry
        out_ref[:, j * LANES:(j + 1) * LANES] = blk
        if block_ref is not None:
            for ex in range(ne):
                block_ref[ex, j:j + 1, :] = blk[ex:ex + 1, :]
        carry = blk[:, LANES - 1:LANES]


def _topk_kernel(cap, aff_ref, idx_ref, gate_ref, cum_ref, csb_ref, afb_ref):
    aff = aff_ref[0]
    ne, n = aff.shape
    nb = n // LANES
    bits = pltpu.bitcast(aff, jnp.int32)

    def search(i, t):
        cand = t | (jnp.int32(1) << (30 - i))
        cnt = jnp.sum(jnp.where(bits >= cand, 1.0, 0.0), axis=-1, keepdims=True)
        return jnp.where(cnt >= cap, cand, t)

    thr = lax.fori_loop(0, 31, search, jnp.zeros((ne, 1), jnp.int32))
    gt = bits > thr
    eq = bits == thr
    need = cap - jnp.sum(jnp.where(gt, 1.0, 0.0), axis=-1, keepdims=True)
    ri = lax.broadcasted_iota(jnp.int32, (LANES, LANES), 0)
    ci = lax.broadcasted_iota(jnp.int32, (LANES, LANES), 1)
    tri = jnp.where(ri <= ci, 1.0, 0.0).astype(BF16)
    _lane_cumsum(jnp.where(eq, 1.0, 0.0).astype(BF16), tri, cum_ref)
    sel = gt | (eq & (cum_ref[...] <= need))
    affsel = jnp.where(sel, aff, 0.0)
    for j in range(nb):
        for ex in range(ne):
            afb_ref[ex, j:j + 1, :] = affsel[ex:ex + 1, j * LANES:(j + 1) * LANES]
    _lane_cumsum(jnp.where(sel, 1.0, 0.0).astype(BF16), tri, cum_ref, csb_ref)

    group = min(TOPK_RANK_LANES, cap)
    blk_id = lax.broadcasted_iota(jnp.int32, (nb, group), 0).astype(F32)

    def per_expert(e, carry):
        idx_acc, gate_acc = carry
        cs = csb_ref[e]
        af = afb_ref[e]
        cs_hi = jnp.floor(cs * (1.0 / 16.0))
        cs_parts = (cs_hi.astype(BF16), (cs - 16.0 * cs_hi).astype(BF16))
        a1 = af.astype(BF16)
        a2 = (af - a1.astype(F32)).astype(BF16)
        a3 = (af - a1.astype(F32) - a2.astype(F32)).astype(BF16)
        block_end = cs[:, LANES - 1:LANES]
        ranks = [(lax.broadcasted_iota(jnp.int32, (1, group), 1) + r0).astype(F32) for r0 in range(0, cap, group)]
        jsel = [jnp.sum(jnp.where(block_end <= r, 1.0, 0.0), axis=0, keepdims=True) for r in ranks]
        onehot = [jnp.where(blk_id == j, 1.0, 0.0).astype(BF16) for j in jsel]
        rows_hi = [_dot_tn(cs_parts[0], oh) for oh in onehot]
        rows_lo = [_dot_tn(cs_parts[1], oh) for oh in onehot]
        rows_a = [_dot_tn(a1, oh) + (_dot_tn(a2, oh) + _dot_tn(a3, oh)) for oh in onehot]
        pos, gat = [], []
        for r, j, hi, lo, ra in zip(ranks, jsel, rows_hi, rows_lo, rows_a):
            cnt_row = 16.0 * hi + lo
            pos.append(LANES * j + jnp.sum(jnp.where(cnt_row <= r, 1.0, 0.0), axis=0, keepdims=True))
            gat.append(jnp.sum(jnp.where(cnt_row == r + 1.0, ra, 0.0), axis=0, keepdims=True))
        pos = jnp.concatenate(pos, axis=1).astype(jnp.int32)
        gat = jnp.concatenate(gat, axis=1)
        mine = lax.broadcasted_iota(jnp.int32, (ne, cap), 0) == e
        return jnp.where(mine, pos, idx_acc), jnp.where(mine, gat, gate_acc)

    idx, gate = lax.fori_loop(0, ne, per_expert,
                              (jnp.zeros((ne, cap), jnp.int32), jnp.zeros((ne, cap), F32)))
    idx_ref[0] = idx
    gate_ref[0] = gate


def _expert_topk(aff_t, cap):
    b, ne, n = aff_t.shape
    nb = n // LANES
    return pl.pallas_call(
        functools.partial(_topk_kernel, cap),
        grid=(b,),
        in_specs=[pl.BlockSpec((1, ne, n), lambda i: (i, 0, 0))],
        out_specs=[pl.BlockSpec((1, ne, cap), lambda i: (i, 0, 0))] * 2,
        out_shape=[jax.ShapeDtypeStruct((b, ne, cap), jnp.int32), jax.ShapeDtypeStruct((b, ne, cap), F32)],
        scratch_shapes=[pltpu.VMEM((ne, n), F32), pltpu.VMEM((ne, nb, LANES), F32), pltpu.VMEM((ne, nb, LANES), F32)],
        compiler_params=_cparams(("parallel",)),
        name="expert_topk",
    )(aff_t)


MOE_FF_TILE = 768
MOE_UNROLL = 8


def _moe_kernel(cap, idx_ref, gate_ref, h_ref, wg_ref, wu_ref, wd_ref, o_ref, xt_ref, x_ref, y_ref, yt_ref):
    e = pl.program_id(1)
    f = pl.program_id(2)
    nf = pl.num_programs(2)

    @pl.when((e == 0) & (f == 0))
    def _():
        o_ref[...] = jnp.zeros(o_ref.shape, o_ref.dtype)

    @pl.when(f == 0)
    def _():
        def gather(i, carry):
            for u in range(MOE_UNROLL):
                r = i * MOE_UNROLL + u
                xt_ref[_tile(r), :] = h_ref[0, _tile(idx_ref[0, 0, 0, r]), :]
            return carry
        lax.fori_loop(0, cap // MOE_UNROLL, gather, 0)
        x_ref[...] = _from_tile_major(xt_ref, cap).astype(x_ref.dtype)

    x = x_ref[...]
    hid = (_silu(_dot(x, wg_ref[0])) * _dot(x, wu_ref[0])).astype(BF16)
    part = _dot(hid, wd_ref[0])

    @pl.when(f == 0)
    def _():
        y_ref[...] = part

    @pl.when(f > 0)
    def _():
        y_ref[...] += part

    @pl.when(f == nf - 1)
    def _():
        _to_tile_major(yt_ref, y_ref[...])

        def scatter(i, carry):
            rows = [i * MOE_UNROLL + u for u in range(MOE_UNROLL)]
            toks = [idx_ref[0, 0, 0, r] for r in rows]
            new = [o_ref[0, _tile(t), :] + gate_ref[0, 0, 0, r] * yt_ref[_tile(r), :] for r, t in zip(rows, toks)]
            for t, val in zip(toks, new):
                o_ref[0, _tile(t), :] = val
            return carry
        lax.fori_loop(0, cap // MOE_UNROLL, scatter, 0)


def _moe(idx, gate, h2t, wg, wu, wd):
    b, rows, _ = h2t.shape
    ne, d, ffp = wg.shape
    cap = idx.shape[-1]
    tf = MOE_FF_TILE
    smem = lambda: pl.BlockSpec((1, 1, 1, cap), lambda i, j, k: (i, j, 0, 0), memory_space=pltpu.SMEM)
    act = lambda: pl.BlockSpec((1, rows, LANES), lambda i, j, k: (i, 0, 0), pipeline_mode=pl.Buffered(1))
    return pl.pallas_call(
        functools.partial(_moe_kernel, cap),
        grid=(b, ne, ffp // tf),
        in_specs=[smem(), smem(), act(),
                  pl.BlockSpec((1, d, tf), lambda i, j, k: (j, 0, k)),
                  pl.BlockSpec((1, d, tf), lambda i, j, k: (j, 0, k)),
                  pl.BlockSpec((1, tf, d), lambda i, j, k: (j, k, 0))],
        out_specs=act(),
        out_shape=jax.ShapeDtypeStruct((b, rows, LANES), F32),
        scratch_shapes=[pltpu.VMEM((cap * SUBLANES, LANES), F32), pltpu.VMEM((cap, d), BF16),
                        pltpu.VMEM((cap, d), F32), pltpu.VMEM((cap * SUBLANES, LANES), F32)],
        compiler_params=_cparams(("parallel", "arbitrary", "arbitrary")),
        name="expert_ffn",
    )(idx[:, :, None, :], gate[:, :, None, :], h2t, wg, wu, wd)


def _final_kernel(xm_ref, moe_ref, g_ref, o_ref):
    o_ref[0] = xm_ref[0] + g_ref[0] * _from_tile_major(moe_ref, xm_ref.shape[1], lead=(0,))


def _final(xmid, moe_t, g2, tl):
    b, n, d = xmid.shape
    row = pl.BlockSpec((1, tl, d), lambda i, j: (i, j, 0))
    return pl.pallas_call(
        _final_kernel,
        grid=(b, n // tl),
        in_specs=[row, pl.BlockSpec((1, tl * SUBLANES, LANES), lambda i, j: (i, j, 0)),
                  pl.BlockSpec((1, 1, d), lambda i, j: (i, 0, 0))],
        out_specs=row,
        out_shape=jax.ShapeDtypeStruct((b, n, d), F32),
        compiler_params=_cparams(("parallel", "parallel")),
        name="final_residual",
    )(xmid, moe_t, g2)


def _rope_tables(n_lat):
    rows = n_lat // GRID_W
    row = jnp.repeat(jnp.arange(rows), GRID_W).astype(F32)
    col = jnp.broadcast_to(jnp.arange(GRID_W), (rows, GRID_W)).reshape(-1).astype(F32)
    n_freq = MLA_ROPE // 4
    inv_freq = ROPE_BASE ** (-jnp.arange(n_freq, dtype=F32) / n_freq)
    ang_r = row[:, None] * inv_freq
    ang_c = col[:, None] * inv_freq
    ang = jnp.concatenate([ang_r, ang_r, ang_c, ang_c], axis=-1)
    zeros = jnp.zeros((n_lat, LANES - MLA_ROPE), F32)
    return jnp.concatenate([jnp.cos(ang), zeros], axis=-1), jnp.concatenate([jnp.sin(ang), zeros], axis=-1)


def _rotate_half(v, signed=True):
    x1, x2, x3, x4 = jnp.split(v, 4, axis=-1)
    sgn = -1.0 if signed else 1.0
    return jnp.concatenate([sgn * x2, x1, sgn * x4, x3], axis=-1)


def _pad_lanes(v, width):
    return jnp.concatenate([v, jnp.zeros(v.shape[:-1] + (width - v.shape[-1],), v.dtype)], axis=-1)


def kernel(x, c, ctx, c_ctx, ada_w, ada_b, norm1_g, norm2_g, w_in, conv_w, a_log, dt_bias, dn_norm_g, w_out_a, q_a_norm_g, w_uq, kv_a_norm_g, w_ukv, q_norm_g, k_norm_g, w_out_b, w_o, router_w, w_gate, w_up, w_down):
    assert ada_w.shape[0] == 1, "single-layer block"
    b, n_lat, d = x.shape
    n_ctx = ctx.shape[1]
    hh, hd = N_HEADS, HEAD_DIM
    qkv_dim = 3 * hh * hd
    nb = 2 * hh

    n_cond = -(-(b + 1) // SUBLANES) * SUBLANES
    cond = jnp.concatenate([c, c_ctx[None], jnp.zeros((n_cond - b - 1, d), F32)], axis=0)
    mod = _modulation(cond, ada_w[0], ada_b[0])
    mods = [mod[:, i * d:(i + 1) * d] for i in range(6)]
    lat = lambda m: m[:b, None, :]
    cvec = lambda m: jnp.broadcast_to(m[b][None, None, :], (b, 1, d))

    o_z = qkv_dim
    o_alpha = o_z + hh * hd
    o_beta = o_alpha + nb
    o_cq = o_beta + nb
    o_ckv = o_cq + MLA_Q_RANK
    o_kr = o_ckv + MLA_KV_RANK
    o_gate = o_kr + MLA_ROPE
    w = w_in[0]
    w_big = jnp.concatenate([w[:, :o_alpha], w[:, o_gate:], w[:, o_cq:o_ckv]], axis=1).astype(BF16)
    small_w = _pad_lanes(w[:, o_ckv:o_gate], MLA_KV_RANK + LANES).astype(BF16)
    w_ab_t = jnp.transpose(w[:, o_alpha:o_cq]).astype(BF16)
    gate_pad = jnp.zeros((nb, 1), F32)
    alog = jnp.concatenate([a_log[0].reshape(nb, 1), gate_pad], axis=0)
    dtb = jnp.concatenate([dt_bias[0].reshape(nb, 1), gate_pad], axis=0)
    widths = (qkv_dim, hh * hd, N_BRANCHES * d, MLA_Q_RANK)
    g1 = norm1_g[0][None]
    qkv_x, z_x, gt_x, cq_x, sm_x, dg_x = _in_projection(x, lat(mods[0]), lat(mods[1]), g1, w_big, small_w, w_ab_t,
                                                        alog, dtb, widths, 512)
    qkv_c, _, _, cq_c, sm_c, dg_c = _in_projection(ctx, cvec(mods[0]), cvec(mods[1]), g1, w_big, small_w, w_ab_t,
                                                   alog, dtb, widths, min(n_ctx, 512))

    y_a = _deltanet(qkv_x, qkv_c, conv_w[0], dg_x, dg_c, z_x, dn_norm_g[0][None])

    wq = w_uq[0].reshape(MLA_Q_RANK, hh, MLA_QK_DIM)
    wq = jnp.concatenate([wq[:, :, :hd].reshape(MLA_Q_RANK, hh * hd),
                          _pad_lanes(wq[:, :, hd:], hd).reshape(MLA_Q_RANK, hh * hd),
                          _pad_lanes(_rotate_half(wq[:, :, hd:]), hd).reshape(MLA_Q_RANK, hh * hd)],
                         axis=1).astype(BF16)
    wkv = w_ukv[0].reshape(MLA_KV_RANK, hh, 2 * hd)
    wk = wkv[:, :, :hd].reshape(MLA_KV_RANK, hh * hd).astype(BF16)
    wvt = jnp.transpose(wkv[:, :, hd:].reshape(MLA_KV_RANK, hh * hd)).astype(BF16)
    gq = jnp.concatenate([_pad_lanes(q_norm_g[0], 2 * hd),
                          _pad_lanes(_rotate_half(q_norm_g[0][hd:], signed=False), hd)])[None]
    gk = _pad_lanes(k_norm_g[0], 2 * hd)[None]
    gqa, gkva = q_a_norm_g[0][None], kv_a_norm_g[0][None]
    cos, sin = _rope_tables(n_lat)
    cos_c = _pad_lanes(jnp.ones((n_ctx, MLA_ROPE), F32), LANES)
    sin_c = jnp.zeros((n_ctx, LANES), F32)
    q_x, k_x, v_x = _mla_projections(cq_x, sm_x, gqa, gkva, wq, wk, wvt, gq, gk, cos, sin, True, 512)
    k_c, v_c = _mla_projections(cq_c, sm_c, gqa, gkva, wq, wk, wvt, gq, gk, cos_c, sin_c, False, min(n_ctx, 512))
    y_b = _attention(q_x, k_c, v_c, k_x, v_x, min(n_lat, 1024))

    rwt = jnp.transpose(router_w[0])
    x_mid, h2, aff_t = _merge(y_a, y_b, gt_x, x, lat(mods[2]), lat(mods[3]), lat(mods[4]), norm2_g[0][None],
                              w_out_a[0].astype(BF16), w_out_b[0].astype(BF16), w_o[0].astype(BF16), rwt, 512)

    cap = EC_CAPACITY * n_lat // N_EXPERTS
    idx, gate = _expert_topk(aff_t, cap)
    ff = w_gate.shape[-1]
    ffp = -(-ff // MOE_FF_TILE) * MOE_FF_TILE
    wg = _pad_lanes(w_gate[0], ffp).astype(BF16)
    wu = _pad_lanes(w_up[0], ffp).astype(BF16)
    wd = jnp.concatenate([w_down[0], jnp.zeros((N_EXPERTS, ffp - ff, d), F32)], axis=1).astype(BF16)
    moe = _moe(idx, gate, h2, wg, wu, wd)
    return _final(x_mid, moe, lat(mods[5]), 512)
```

```python
import functools
import math

import jax
import jax.numpy as jnp
from jax import lax
from jax.experimental import pallas as pl
from jax.experimental.pallas import tpu as pltpu

F32 = jnp.float32
BF16 = jnp.bfloat16
F8 = jnp.float8_e4m3fn
EPS = 1e-6

N_HEADS = 8
HEAD_DIM = 128
DN_CONV = 5
DN_CHUNK = 64
MLA_Q_RANK = 384
MLA_KV_RANK = 256
MLA_ROPE = 64
MLA_QK_DIM = HEAD_DIM + MLA_ROPE
GRID_W = 64
ROPE_BASE = 10000.0
N_EXPERTS = 16
EC_CAPACITY = 2
N_BRANCHES = 2

LANES = 128
SUBLANES = 8
VMEM_LIMIT_BYTES = 56 * 1024 * 1024

HI = lax.Precision.HIGHEST


def _cparams(sem):
    return pltpu.CompilerParams(dimension_semantics=sem, vmem_limit_bytes=VMEM_LIMIT_BYTES)


def _dot(a, b, precision=None):
    return jnp.dot(a, b, preferred_element_type=F32, precision=precision)


def _dot_nt(a, b, precision=None):
    return lax.dot_general(a, b, (((1,), (1,)), ((), ())), preferred_element_type=F32, precision=precision)


def _dot_tn(a, b):
    return lax.dot_general(a, b, (((0,), (0,)), ((), ())), preferred_element_type=F32)


def _sigmoid(x):
    return 1.0 / (1.0 + jnp.exp(-x))


def _silu(x):
    return x * _sigmoid(x)


def _const_spec(shape):
    nd = len(shape)
    return pl.BlockSpec(shape, lambda *_: (0,) * nd, pipeline_mode=pl.Buffered(1))


def _to_tile_major(ref, x, lead=()):
    rows = x.shape[0]
    for s_ in range(SUBLANES):
        ref[lead + (pl.ds(s_, rows, stride=SUBLANES), slice(None))] = x[:, s_ * LANES:(s_ + 1) * LANES]


def _from_tile_major(ref, rows, lead=()):
    return jnp.concatenate([ref[lead + (pl.ds(s_, rows, stride=SUBLANES), slice(None))] for s_ in range(SUBLANES)],
                           axis=1)


def _tile(i):
    return pl.ds(pl.multiple_of(i * SUBLANES, SUBLANES), SUBLANES)


def _mod_kernel(c_ref, w_ref, b_ref, o_ref):
    c = c_ref[...]
    o_ref[...] = _dot(_silu(c), w_ref[...], precision=HI) + b_ref[...]


def _modulation(cond, ada_w, ada_b):
    n, d = cond.shape
    n_out = ada_w.shape[1]
    tn = d
    return pl.pallas_call(
        _mod_kernel,
        grid=(n_out // tn,),
        in_specs=[pl.BlockSpec((n, d), lambda j: (0, 0)),
                  pl.BlockSpec((d, tn), lambda j: (0, j)),
                  pl.BlockSpec((1, tn), lambda j: (0, j))],
        out_specs=pl.BlockSpec((n, tn), lambda j: (0, j)),
        out_shape=jax.ShapeDtypeStruct((n, n_out), F32),
        compiler_params=_cparams(("parallel",)),
        name="modulation",
    )(cond, ada_w, ada_b.reshape(1, n_out))


def _modulated_norm(x, g, shift, scale):
    ms = jnp.mean(x * x, axis=-1, keepdims=True)
    return (x * lax.rsqrt(ms + EPS)) * g * (1.0 + scale) + shift


def _inproj_kernel(widths, x_ref, shift_ref, scale_ref, g_ref, wb_ref, ws_ref, wab_ref, alog_ref, dtb_ref,
                   *out_refs):
    big_refs, small_ref, gates_ref = out_refs[:-2], out_refs[-2], out_refs[-1]
    h = _modulated_norm(x_ref[0], g_ref[...], shift_ref[0], scale_ref[0]).astype(BF16)
    off = 0
    for o_ref, width in zip(big_refs, widths):
        for c0 in range(0, width, 1024):
            c1 = min(c0 + 1024, width)
            o_ref[0, :, c0:c1] = _dot(h, wb_ref[:, off + c0:off + c1]).astype(o_ref.dtype)
        off += width
    small_ref[0] = _dot(h, ws_ref[...])
    ab = _dot_nt(wab_ref[...], h)
    row_id = lax.broadcasted_iota(jnp.int32, ab.shape, 0)
    xs = ab + dtb_ref[...]
    softplus = jnp.maximum(xs, 0.0) + jnp.log1p(jnp.exp(-jnp.abs(xs)))
    gates_ref[0] = jnp.where(row_id < 2 * N_HEADS, -jnp.exp(alog_ref[...]) * softplus, _sigmoid(ab))


def _in_projection(x, shift, scale, g, w_big, w_small, w_ab_t, alog, dtb, widths, tl):
    b, n, d = x.shape
    row = lambda w: pl.BlockSpec((1, tl, w), lambda i, j: (i, j, 0))
    vec = pl.BlockSpec((1, 1, d), lambda i, j: (i, 0, 0))
    n_gate = w_ab_t.shape[0]
    outs = [jax.ShapeDtypeStruct((b, n, w), BF16) for w in widths]
    outs.append(jax.ShapeDtypeStruct((b, n, w_small.shape[1]), F32))
    outs.append(jax.ShapeDtypeStruct((b, n_gate, n), F32))
    return pl.pallas_call(
        functools.partial(_inproj_kernel, widths),
        grid=(b, n // tl),
        in_specs=[row(d), vec, vec, _const_spec((1, d)), _const_spec(w_big.shape), _const_spec(w_small.shape),
                  _const_spec(w_ab_t.shape), _const_spec(alog.shape), _const_spec(dtb.shape)],
        out_specs=[row(w) for w in widths] + [row(w_small.shape[1]),
                                              pl.BlockSpec((1, n_gate, tl), lambda i, j: (i, 0, j))],
        out_shape=outs,
        compiler_params=_cparams(("parallel", "parallel")),
        name="in_projection",
    )(x, shift, scale, g, w_big, w_small, w_ab_t, alog, dtb)


CONV_ROWS = 512
CONV_PAD = 8


def _conv_fill(raw_ref, pad_ref, n_tok):
    zeros = jnp.zeros((CONV_PAD, LANES), F32)
    pad_ref[0:CONV_PAD, :] = zeros
    pad_ref[CONV_PAD + n_tok:2 * CONV_PAD + n_tok, :] = zeros
    pad_ref[CONV_PAD:CONV_PAD + n_tok, :] = raw_ref[0].astype(F32)


def _conv_block(pad_ref, w, r0, rows, normalize):
    half = DN_CONV // 2
    acc = None
    for j in range(DN_CONV):
        term = pad_ref[pl.ds(r0 + (CONV_PAD + j - half), rows), :] * w[j:j + 1, :]
        acc = term if acc is None else acc + term
    y = _silu(acc)
    if normalize:
        y = y * lax.rsqrt(jnp.sum(y * y, axis=-1, keepdims=True) + EPS)
    return y


def _conv_silu(raw_ref, w_ref, pad_ref, out_ref, n_tok, normalize):
    _conv_fill(raw_ref, pad_ref, n_tok)
    w = w_ref[...]
    rb = min(CONV_ROWS, n_tok)
    for r0 in range(0, n_tok, rb):
        out_ref[r0:r0 + rb, :] = _conv_block(pad_ref, w, r0, rb, normalize)


def _dn_masks():
    n = 2 * DN_CHUNK
    ri = lax.broadcasted_iota(jnp.int32, (n, n), 0)
    ci = lax.broadcasted_iota(jnp.int32, (n, n), 1)
    top = ri < DN_CHUNK
    same = (ri // DN_CHUNK) == (ci // DN_CHUNK)
    sgn = jnp.where(top, 1, -1)
    delta = (ri - ci) * sgn
    incl = same & (delta >= 0)
    strict = same & (delta > 0)
    incl_t = same & (delta <= 0)
    eye = ri == ci
    return top, incl, strict, incl_t, eye


DN_INV_BLOCK = 8
DN_PREP_GROUP = 16


def _hi_lo(a_f32):
    hi = a_f32.astype(BF16)
    return hi, (a_f32 - hi.astype(F32)).astype(BF16)


def _split_dot(a_f32, b_bf16):
    hi, lo = _hi_lo(a_f32)
    return _dot(hi, b_bf16) + _dot(lo, b_bf16)


def _bdot(a_f32, b_f32):
    return _dot(a_f32.astype(BF16), b_f32.astype(BF16))


def _dn_gate_rows(c0, group, gt_ref):
    n = 2 * DN_CHUNK
    t0 = pl.multiple_of(c0 * DN_CHUNK, n)
    lane = lax.broadcasted_iota(jnp.int32, (1, n), 1)
    rows = [[gt_ref[0, kind, dr, 0, :, pl.ds(t0, group * DN_CHUNK)] for dr in range(2)] for kind in range(2)]
    out = []
    for j in range(group):
        p, odd = divmod(j, 2)
        both = []
        for kind in range(2):
            fwd = rows[kind][0][:, p * n:(p + 1) * n]
            bwd = rows[kind][1][:, p * n:(p + 1) * n]
            if odd:
                both.append(jnp.where(lane < DN_CHUNK, pltpu.roll(fwd, DN_CHUNK, axis=1), bwd))
            else:
                both.append(jnp.where(lane < DN_CHUNK, fwd, pltpu.roll(bwd, DN_CHUNK, axis=1)))
        out.append(jnp.concatenate(both, axis=0))
    return out


def _dn_prep_load(c, q_ref, k_ref, v_ref):
    r0 = pl.multiple_of(c * DN_CHUNK, DN_CHUNK)
    return q_ref[pl.ds(r0, DN_CHUNK), :], k_ref[pl.ds(r0, DN_CHUNK), :], v_ref[pl.ds(r0, DN_CHUNK), :]


def _dn_prep_compute(q, k, v, gb):
    n = 2 * DN_CHUNK
    top, incl, strict, incl_t, eye = _dn_masks()
    g_row = jnp.broadcast_to(gb[0:1, :], (SUBLANES, n))
    beta_rows = jnp.broadcast_to(gb[1:2, :], (n, n))
    kst = jnp.concatenate([k, k], axis=0)
    qst = jnp.concatenate([q, q], axis=0)
    vst = jnp.concatenate([v, v], axis=0)
    kst_b = kst.astype(BF16)

    gc_row = _split_dot(g_row, jnp.where(incl_t, 1.0, 0.0).astype(BF16))[0:1, :]
    kk = _dot_nt(kst_b, kst_b)
    qk = _dot_nt(qst.astype(BF16), kst_b)
    yield
    c2 = jnp.broadcast_to(gc_row, (n, n))
    c1 = c2.T
    beta_c = beta_rows.T
    tot = jnp.where(top, c1[DN_CHUNK - 1:DN_CHUNK, :], c1[DN_CHUNK:DN_CHUNK + 1, :])
    decay = jnp.exp(jnp.where(incl, c1 - c2, -jnp.inf))
    e_gc = jnp.exp(c1)
    e_rest = jnp.exp(tot - c1)
    scale = HEAD_DIM ** -0.5

    m = jnp.where(strict, kk * beta_c * decay, 0.0)
    ri = lax.broadcasted_iota(jnp.int32, (n, n), 0)
    ci = lax.broadcasted_iota(jnp.int32, (n, n), 1)
    blk = DN_INV_BLOCK
    dg = jnp.where((ri // blk) == (ci // blk), m, 0.0)
    t = jnp.where(eye, 1.0, 0.0) - dg
    dp = _bdot(dg, dg)
    yield
    for it in range(blk.bit_length() - 2):
        t = t + _bdot(t, dp)
        if it + 1 < blk.bit_length() - 2:
            dp = _bdot(dp, dp)
        yield
    while blk < DN_CHUNK:
        off = jnp.where(((ri // (2 * blk)) == (ci // (2 * blk))) & ((ri // blk) != (ci // blk)), m, 0.0)
        ta = _bdot(t, off)
        yield
        t = t - _bdot(ta, t)
        yield
        blk *= 2

    kb = kst * beta_c
    rhs = jnp.concatenate([vst * beta_c, kb * e_gc], axis=1)
    uw = _bdot(t, rhs)
    yield
    d = HEAD_DIM
    u_b = uw[:, 0:d].astype(BF16)
    w_b = uw[:, d:2 * d].astype(BF16)
    kd = (kst * e_rest).astype(BF16)
    qkm = (jnp.where(incl, qk * decay, 0.0) * scale).astype(BF16)
    ri2 = lax.broadcasted_iota(jnp.int32, (n, 2 * d), 0)
    ci2 = lax.broadcasted_iota(jnp.int32, (n, 2 * d), 1)
    own_dir = (ri2 // DN_CHUNK) == (ci2 // d)
    zero = jnp.zeros((), BF16)
    wu_bd = jnp.concatenate([jnp.where(own_dir, jnp.concatenate([w_b, w_b], axis=1), zero),
                             jnp.where(own_dir, jnp.concatenate([u_b, u_b], axis=1), zero)], axis=1)
    kw = _dot_tn(kd, wu_bd)
    qw = _dot(qkm, jnp.concatenate([w_b, u_b], axis=1))
    yield
    return ((-kw[:, 0:2 * d]).astype(BF16),
            kw[:, 2 * d:4 * d],
            (qst * e_gc * scale - qw[:, 0:d]).astype(BF16),
            qw[:, d:2 * d],
            jnp.exp(tot)[DN_CHUNK - 4:DN_CHUNK + 4, :])


def _dn_prep_group(c0, group, in_refs, out_refs):
    *qkv_refs, gt_ref = in_refs
    gates = _dn_gate_rows(c0, group, gt_ref)
    gens = [_dn_prep_compute(*_dn_prep_load(c0 + j, *qkv_refs), gates[j]) for j in range(group)]
    results = [None] * group
    while any(r is None for r in results):
        for j, gen in enumerate(gens):
            try:
                next(gen)
            except StopIteration as done:
                results[j] = done.value
    for j, res in enumerate(results):
        for ref, val in zip(out_refs, res):
            ref[c0 + j] = val


def _dn_scan_step(cf, cb, a_ref, b_ref, q_ref, o0_ref, ar_ref, s_ref, of_ref, ob_ref):
    h = DN_CHUNK
    d = HEAD_DIM
    s = s_ref[...]
    s_b = s.astype(BF16)
    r_f = _dot(jnp.concatenate([a_ref[cf, :, 0:d], q_ref[cf, 0:h, :]], axis=0), s_b[:, 0:d])
    r_b = _dot(jnp.concatenate([a_ref[cb, :, d:2 * d], q_ref[cb, h:2 * h, :]], axis=0), s_b[:, d:2 * d])
    if of_ref is not None:
        of_ref[pl.ds(pl.multiple_of(cf * h, h), h), :] = r_f[d:d + h, :] + o0_ref[cf, 0:h, :]
        ob_ref[pl.ds(pl.multiple_of(cb * h, h), h), :] = r_b[d:d + h, :] + o0_ref[cb, h:2 * h, :]
    b_sel = jnp.concatenate([b_ref[cf, :, 0:d], b_ref[cb, :, d:2 * d]], axis=1)
    a = jnp.concatenate([jnp.broadcast_to(ar_ref[cf, 0:1, :], (d, d)),
                         jnp.broadcast_to(ar_ref[cb, 4:5, :], (d, d))], axis=1)
    s_ref[...] = s * a + jnp.concatenate([r_f[0:d, :], r_b[0:d, :]], axis=1) + b_sel


def _dn_kernel(n_lat, n_ctx,
               qx_ref, kx_ref, vx_ref, qc_ref, kc_ref, vc_ref, wq_ref, wk_ref, wv_ref,
               gbx_ref, gbc_ref, z_ref, ng_ref, y_ref,
               pad_ref, q_s, k_s, v_s, qc_s, kc_s, vc_s,
               a_x, b_x, qp_x, o0_x, ar_x, a_c, b_c, qp_c, o0_c, ar_c,
               s_ref, of_ref, ob_ref):
    nc_x = n_lat // DN_CHUNK
    nc_c = n_ctx // DN_CHUNK

    _conv_silu(qc_ref, wq_ref, pad_ref, qc_s, n_ctx, True)
    _conv_silu(kc_ref, wk_ref, pad_ref, kc_s, n_ctx, True)
    _conv_silu(vc_ref, wv_ref, pad_ref, vc_s, n_ctx, False)
    _conv_silu(qx_ref, wq_ref, pad_ref, q_s, n_lat, True)
    _conv_silu(kx_ref, wk_ref, pad_ref, k_s, n_lat, True)
    _conv_silu(vx_ref, wv_ref, pad_ref, v_s, n_lat, False)
    grp_c = math.gcd(nc_c, DN_PREP_GROUP)
    grp_x = math.gcd(nc_x, DN_PREP_GROUP)
    assert grp_c % 2 == 0 and grp_x % 2 == 0, "a group's tokens must cover whole 128-lane tiles"
    assert nc_x % 2 == 0

    def prep_c(i, carry):
        _dn_prep_group(i * grp_c, grp_c, (qc_s, kc_s, vc_s, gbc_ref), (a_c, b_c, qp_c, o0_c, ar_c))
        return carry

    def prep_x(i, carry):
        _dn_prep_group(i * grp_x, grp_x, (q_s, k_s, v_s, gbx_ref), (a_x, b_x, qp_x, o0_x, ar_x))
        return carry

    lax.fori_loop(0, nc_c // grp_c, prep_c, 0)
    lax.fori_loop(0, nc_x // grp_x, prep_x, 0)

    s_ref[...] = jnp.zeros(s_ref.shape, F32)

    def scan_c(i, carry):
        _dn_scan_step(i, nc_c - 1 - i, a_c, b_c, qp_c, o0_c, ar_c, s_ref, None, None)
        return carry

    def scan_x(i, carry):
        _dn_scan_step(i, nc_x - 1 - i, a_x, b_x, qp_x, o0_x, ar_x, s_ref, of_ref, ob_ref)
        return carry

    def finish(c):
        start = c * DN_CHUNK
        rows = pl.ds(start if isinstance(c, int) else pl.multiple_of(start, DN_CHUNK), DN_CHUNK)
        o = of_ref[rows, :] + ob_ref[rows, :]
        ms = jnp.mean(o * o, axis=-1, keepdims=True)
        y = (o * lax.rsqrt(ms + EPS)) * ng_ref[...] * _silu(z_ref[0, rows, :].astype(F32))
        y_ref[0, rows, :] = y.astype(y_ref.dtype)

    def scan_x_finish(i, carry):
        finish(i - 1)
        finish(nc_x - i)
        scan_x(i, carry)
        return carry

    half = nc_x // 2
    lax.fori_loop(0, nc_c, scan_c, 0)
    lax.fori_loop(0, half + 1, scan_x, 0)
    lax.fori_loop(half + 1, nc_x, scan_x_finish, 0)
    finish(nc_x - 1)
    finish(0)


def _deltanet(qkv_x, qkv_c, conv_w, gt_x, gt_c, z, norm_g):
    b, n_lat, _ = qkv_x.shape
    n_ctx = qkv_c.shape[1]
    gbx = gt_x.reshape(b, 2, 2, N_HEADS, 1, n_lat)
    gbc = gt_c.reshape(b, 2, 2, N_HEADS, 1, n_ctx)
    hh = N_HEADS
    d = HEAD_DIM
    nc_x, nc_c = n_lat // DN_CHUNK, n_ctx // DN_CHUNK
    tok = lambda n, off: pl.BlockSpec((1, n, d), lambda i, j: (i, 0, off + j))
    cw = lambda off: pl.BlockSpec((DN_CONV, d), lambda i, j: (0, off + j))
    gspec = lambda n: pl.BlockSpec((1, 2, 2, 1, 1, n), lambda i, j: (i, 0, 0, j, 0, 0))
    chunk_scratch = lambda nc: [
        pltpu.VMEM((nc, d, 2 * d), BF16),
        pltpu.VMEM((nc, d, 2 * d), F32),
        pltpu.VMEM((nc, 2 * DN_CHUNK, d), BF16),
        pltpu.VMEM((nc, 2 * DN_CHUNK, d), F32),
        pltpu.VMEM((nc, SUBLANES, 2 * DN_CHUNK), F32),
    ]
    return pl.pallas_call(
        functools.partial(_dn_kernel, n_lat, n_ctx),
        grid=(b, hh),
        in_specs=[tok(n_lat, 0), tok(n_lat, hh), tok(n_lat, 2 * hh),
                  tok(n_ctx, 0), tok(n_ctx, hh), tok(n_ctx, 2 * hh),
                  cw(0), cw(hh), cw(2 * hh),
                  gspec(n_lat), gspec(n_ctx),
                  tok(n_lat, 0), _const_spec((1, d))],
        out_specs=tok(n_lat, 0),
        out_shape=jax.ShapeDtypeStruct((b, n_lat, hh * d), BF16),
        scratch_shapes=[pltpu.VMEM((n_lat + 2 * CONV_PAD, d), F32)]
        + [pltpu.VMEM((n_lat, d), F32)] * 3 + [pltpu.VMEM((n_ctx, d), F32)] * 3
        + chunk_scratch(nc_x) + chunk_scratch(nc_c)
        + [pltpu.VMEM((d, 2 * d), F32), pltpu.VMEM((n_lat, d), F32), pltpu.VMEM((n_lat, d), F32)],
        compiler_params=_cparams(("parallel", "parallel")),
        name="deltanet",
    )(qkv_x, qkv_x, qkv_x, qkv_c, qkv_c, qkv_c, conv_w, conv_w, conv_w, gbx, gbc, z, norm_g)


ATTN_ONES_ROWS = 32


def _rope(x, cos, sin):
    lane = lax.broadcasted_iota(jnp.int32, x.shape, 1)
    quarter = MLA_ROPE // 4
    rot = jnp.where((lane // quarter) % 2 == 0,
                    -pltpu.roll(x, LANES - quarter, axis=1), pltpu.roll(x, quarter, axis=1))
    return x * cos + rot * sin


def _rms(x, n):
    return lax.rsqrt(jnp.sum(x * x, axis=-1, keepdims=True) / n + EPS)


def _mla_proj_kernel(with_q, cq_ref, small_ref, gqa_ref, gkva_ref, wq_ref, wk_ref, wvt_ref, gq_ref, gk_ref,
                     cos_ref, sin_ref, *out_refs):
    hd = HEAD_DIM
    hh = N_HEADS
    cos, sin = cos_ref[...], sin_ref[...]
    small = small_ref[0]
    ckv = small[:, 0:MLA_KV_RANK]
    kpe = small[:, MLA_KV_RANK:MLA_KV_RANK + LANES]
    lane = lax.broadcasted_iota(jnp.int32, kpe.shape, 1)
    kpe = jnp.where(lane < MLA_ROPE, kpe, 0.0)
    kpe_ssq = jnp.sum(kpe * kpe, axis=-1, keepdims=True)
    ckvn = (ckv * _rms(ckv, MLA_KV_RANK) * gkva_ref[...]).astype(BF16)
    kf = _dot(ckvn, wk_ref[...])
    vt = _dot_nt(wvt_ref[...], ckvn)
    if with_q:
        q_ref, k_ref, v_ref = out_refs
        cq = cq_ref[0].astype(F32)
        cqn = (cq * _rms(cq, MLA_Q_RANK) * gqa_ref[...]).astype(BF16)
        qf = _dot(cqn, wq_ref[...])
        q_scale = MLA_QK_DIM ** -0.5 * math.log2(math.e) * ATTN_QK_BALANCE
    else:
        k_ref, v_ref = out_refs
    gq, gk = gq_ref[...], gk_ref[...]
    kpe_rot = _rope(kpe * gk[:, hd:2 * hd], cos, sin)
    for h in range(hh):
        if with_q:
            qn = qf[:, h * hd:(h + 1) * hd]
            qp = qf[:, (hh + h) * hd:(hh + h + 1) * hd]
            ssq = jnp.sum(qn * qn + qp * qp, axis=-1, keepdims=True)
            r = lax.rsqrt(ssq / MLA_QK_DIM + EPS)
            q_ref[0, h, :, 0:hd] = (qn * r * gq[:, 0:hd] * q_scale).astype(q_ref.dtype)
            qr = qf[:, (2 * hh + h) * hd:(2 * hh + h + 1) * hd]
            q_pe = qp * (gq[:, hd:2 * hd] * cos) + qr * (gq[:, 2 * hd:3 * hd] * sin)
            q_ref[0, h, :, hd:2 * hd] = (q_pe * (r * q_scale)).astype(q_ref.dtype)
        kn = kf[:, h * hd:(h + 1) * hd]
        r = lax.rsqrt((jnp.sum(kn * kn, axis=-1, keepdims=True) + kpe_ssq) / MLA_QK_DIM + EPS)
        rk = r * (1.0 / ATTN_QK_BALANCE)
        k_ref[0, h, :, 0:hd] = (kn * rk * gk[:, 0:hd]).astype(k_ref.dtype)
        k_ref[0, h, :, hd:2 * hd] = (kpe_rot * rk).astype(k_ref.dtype)
        v_ref[0, h, 0:hd, :] = vt[h * hd:(h + 1) * hd, :].astype(v_ref.dtype)
        v_ref[0, h, hd:hd + ATTN_ONES_ROWS, :] = jnp.ones((ATTN_ONES_ROWS, vt.shape[1]), v_ref.dtype)


def _mla_projections(cq, small, gqa, gkva, wq, wk, wvt, gq, gk, cos, sin, with_q, tl):
    b, n, _ = small.shape
    hh, hd = N_HEADS, HEAD_DIM
    row = lambda w: pl.BlockSpec((1, tl, w), lambda i, j: (i, j, 0))
    head = lambda w: pl.BlockSpec((1, hh, tl, w), lambda i, j: (i, 0, j, 0))
    tab = pl.BlockSpec((tl, LANES), lambda i, j: (j, 0))
    hv = hd + ATTN_ONES_ROWS
    out_specs = [head(2 * hd), pl.BlockSpec((1, hh, hv, tl), lambda i, j: (i, 0, 0, j))]
    out_shape = [jax.ShapeDtypeStruct((b, hh, n, 2 * hd), F8), jax.ShapeDtypeStruct((b, hh, hv, n), F8)]
    if with_q:
        out_specs = [head(2 * hd)] + out_specs
        out_shape = [jax.ShapeDtypeStruct((b, hh, n, 2 * hd), F8)] + out_shape
    return pl.pallas_call(
        functools.partial(_mla_proj_kernel, with_q),
        grid=(b, n // tl),
        in_specs=[row(cq.shape[-1]), row(small.shape[-1]), _const_spec(gqa.shape), _const_spec(gkva.shape),
                  _const_spec(wq.shape), _const_spec(wk.shape), _const_spec(wvt.shape), _const_spec(gq.shape),
                  _const_spec(gk.shape),
                  tab, tab],
        out_specs=out_specs,
        out_shape=out_shape,
        compiler_params=_cparams(("parallel", "parallel")),
        name="mla_projections_q" if with_q else "mla_projections_kv",
    )(cq, small, gqa, gkva, wq, wk, wvt, gq, gk, cos, sin)


ATTN_SUB_COLS = 256
ATTN_LOOKAHEAD = 3
ATTN_QK_BALANCE = 4.0
ATTN_P_SHIFT = 8.0


def _attn_kernel(q_ref, kc_ref, vct_ref, kx_ref, vxt_ref, o_ref):
    tq = q_ref.shape[2]
    n_sub = tq // ATTN_SUB_COLS

    def scores(i):
        q = q_ref[0, 0, i * ATTN_SUB_COLS:(i + 1) * ATTN_SUB_COLS, :]
        return _dot_nt(kc_ref[0, 0], q), _dot_nt(kx_ref[0, 0], q)

    pending = [scores(i) for i in range(min(ATTN_LOOKAHEAD, n_sub))]
    for i in range(n_sub):
        if i + ATTN_LOOKAHEAD < n_sub:
            pending.append(scores(i + ATTN_LOOKAHEAD))
        s_c, s_x = pending.pop(0)
        s_c, s_x = s_c.astype(BF16), s_x.astype(BF16)
        m = jnp.maximum(jnp.max(s_c, axis=0, keepdims=True), jnp.max(s_x, axis=0, keepdims=True)) - ATTN_P_SHIFT
        p_c = jnp.exp2(s_c - m).astype(F8)
        p_x = jnp.exp2(s_x - m).astype(F8)
        o_t = _dot(vct_ref[0, 0], p_c) + _dot(vxt_ref[0, 0], p_x)
        hd = o_ref.shape[-1]
        l = o_t[hd:hd + 1, :]
        o_ref[0, i * ATTN_SUB_COLS:(i + 1) * ATTN_SUB_COLS, :] = (o_t[0:hd, :] / l).T.astype(o_ref.dtype)


def _attention(q, k_c, vt_c, k_x, vt_x, tq):
    b, hh, n, dq = q.shape
    n_ctx = k_c.shape[2]
    hv = vt_x.shape[2]
    hd = hv - ATTN_ONES_ROWS
    keys = lambda n_: pl.BlockSpec((1, 1, n_, dq), lambda i, j, t: (i, j, 0, 0))
    vals = lambda n_: pl.BlockSpec((1, 1, hv, n_), lambda i, j, t: (i, j, 0, 0))
    return pl.pallas_call(
        _attn_kernel,
        grid=(b, hh, n // tq),
        in_specs=[pl.BlockSpec((1, 1, tq, dq), lambda i, j, t: (i, j, t, 0)),
                  keys(n_ctx), vals(n_ctx), keys(n), vals(n)],
        out_specs=pl.BlockSpec((1, tq, hd), lambda i, j, t: (i, t, j)),
        out_shape=jax.ShapeDtypeStruct((b, n, hh * hd), BF16),
        compiler_params=_cparams(("parallel", "parallel", "parallel")),
        name="attention",
    )(q, k_c, vt_c, k_x, vt_x)


def _merge_kernel(ya_ref, yb_ref, gate_ref, x_ref, g1_ref, sh2_ref, sc2_ref, n2g_ref,
                  wa_ref, wb_ref, wo_ref, rwt_ref, xmid_ref, h2_ref, aff_ref):
    d = x_ref.shape[-1]
    ga = _sigmoid(gate_ref[0, :, 0:d].astype(F32))
    gb = _sigmoid(gate_ref[0, :, d:2 * d].astype(F32))
    mix = ga * _dot(ya_ref[0], wa_ref[...]) + gb * _dot(yb_ref[0], wb_ref[...])
    mix = _dot(mix.astype(BF16), wo_ref[...])
    xm = x_ref[0] + g1_ref[0] * mix
    xmid_ref[0] = xm
    h2 = _modulated_norm(xm, n2g_ref[...], sh2_ref[0], sc2_ref[0])
    _to_tile_major(h2_ref, h2, lead=(0,))
    rw_hi, rw_lo = _hi_lo(rwt_ref[...])
    h_hi, h_lo = _hi_lo(h2)
    logits = _dot_nt(rw_hi, h_hi) + (_dot_nt(rw_hi, h_lo) + _dot_nt(rw_lo, h_hi))
    mx = jnp.max(logits, axis=0, keepdims=True)
    ex = jnp.exp(logits - mx)
    aff_ref[0] = ex / jnp.sum(ex, axis=0, keepdims=True)


def _merge(ya, yb, gate, x, g1, sh2, sc2, n2g, wa, wb, wo, rwt, tl):
    b, n, d = x.shape
    ne = rwt.shape[0]
    row = lambda w: pl.BlockSpec((1, tl, w), lambda i, j: (i, j, 0))
    vec = pl.BlockSpec((1, 1, d), lambda i, j: (i, 0, 0))
    return pl.pallas_call(
        _merge_kernel,
        grid=(b, n // tl),
        in_specs=[row(d), row(d), row(2 * d), row(d), vec, vec, vec, _const_spec((1, d)),
                  _const_spec(wa.shape), _const_spec(wb.shape), _const_spec(wo.shape), _const_spec(rwt.shape)],
        out_specs=[row(d), pl.BlockSpec((1, tl * SUBLANES, LANES), lambda i, j: (i, j, 0)),
                   pl.BlockSpec((1, ne, tl), lambda i, j: (i, 0, j))],
        out_shape=[jax.ShapeDtypeStruct((b, n, d), F32), jax.ShapeDtypeStruct((b, n * SUBLANES, LANES), F32),
                   jax.ShapeDtypeStruct((b, ne, n), F32)],
        compiler_params=_cparams(("parallel", "parallel")),
        name="merge_router",
    )(ya, yb, gate, x, g1, sh2, sc2, n2g, wa, wb, wo, rwt)


TOPK_RANK_LANES = 128


def _lane_cumsum(x_bf16, tri_bf16, out_ref, block_ref=None):
    ne, n = x_bf16.shape
    carry = jnp.zeros((ne, 1), F32)
    for j in range(n // LANES):
        blk = _dot(x_bf16[:, j * LANES:(j + 1) * LANES], tri_bf16) + carry
        out_ref[:, j * LANES:(j + 1) * LANES] = blk
        if block_ref is not None:
            for ex in range(ne):
                block_ref[ex, j:j + 1, :] = blk[ex:ex + 1, :]
        carry = blk[:, LANES - 1:LANES]


def _topk_kernel(cap, aff_ref, idx_ref, gate_ref, cum_ref, csb_ref, afb_ref):
    aff = aff_ref[0]
    ne, n = aff.shape
    nb = n // LANES
    bits = pltpu.bitcast(aff, jnp.int32)

    def search(i, t):
        cand = t | (jnp.int32(1) << (30 - i))
        cnt = jnp.sum(jnp.where(bits >= cand, 1.0, 0.0), axis=-1, keepdims=True)
        return jnp.where(cnt >= cap, cand, t)

    thr = lax.fori_loop(0, 31, search, jnp.zeros((ne, 1), jnp.int32))
    gt = bits > thr
    eq = bits == thr
    need = cap - jnp.sum(jnp.where(gt, 1.0, 0.0), axis=-1, keepdims=True)
    ri = lax.broadcasted_iota(jnp.int32, (LANES, LANES), 0)
    ci = lax.broadcasted_iota(jnp.int32, (LANES, LANES), 1)
    tri = jnp.where(ri <= ci, 1.0, 0.0).astype(BF16)
    _lane_cumsum(jnp.where(eq, 1.0, 0.0).astype(BF16), tri, cum_ref)
    sel = gt | (eq & (cum_ref[...] <= need))
    affsel = jnp.where(sel, aff, 0.0)
    for j in range(nb):
        for ex in range(ne):
            afb_ref[ex, j:j + 1, :] = affsel[ex:ex + 1, j * LANES:(j + 1) * LANES]
    _lane_cumsum(jnp.where(sel, 1.0, 0.0).astype(BF16), tri, cum_ref, csb_ref)

    group = min(TOPK_RANK_LANES, cap)
    blk_id = lax.broadcasted_iota(jnp.int32, (nb, group), 0).astype(F32)

    def per_expert(e, carry):
        idx_acc, gate_acc = carry
        cs = csb_ref[e]
        af = afb_ref[e]
        cs_hi = jnp.floor(cs * (1.0 / 16.0))
        cs_parts = (cs_hi.astype(BF16), (cs - 16.0 * cs_hi).astype(BF16))
        a1 = af.astype(BF16)
        a2 = (af - a1.astype(F32)).astype(BF16)
        a3 = (af - a1.astype(F32) - a2.astype(F32)).astype(BF16)
        block_end = cs[:, LANES - 1:LANES]
        ranks = [(lax.broadcasted_iota(jnp.int32, (1, group), 1) + r0).astype(F32) for r0 in range(0, cap, group)]
        jsel = [jnp.sum(jnp.where(block_end <= r, 1.0, 0.0), axis=0, keepdims=True) for r in ranks]
        onehot = [jnp.where(blk_id == j, 1.0, 0.0).astype(BF16) for j in jsel]
        rows_hi = [_dot_tn(cs_parts[0], oh) for oh in onehot]
        rows_lo = [_dot_tn(cs_parts[1], oh) for oh in onehot]
        rows_a = [_dot_tn(a1, oh) + (_dot_tn(a2, oh) + _dot_tn(a3, oh)) for oh in onehot]
        pos, gat = [], []
        for r, j, hi, lo, ra in zip(ranks, jsel, rows_hi, rows_lo, rows_a):
            cnt_row = 16.0 * hi + lo
            pos.append(LANES * j + jnp.sum(jnp.where(cnt_row <= r, 1.0, 0.0), axis=0, keepdims=True))
            gat.append(jnp.sum(jnp.where(cnt_row == r + 1.0, ra, 0.0), axis=0, keepdims=True))
        pos = jnp.concatenate(pos, axis=1).astype(jnp.int32)
        gat = jnp.concatenate(gat, axis=1)
        mine = lax.broadcasted_iota(jnp.int32, (ne, cap), 0) == e
        return jnp.where(mine, pos, idx_acc), jnp.where(mine, gat, gate_acc)

    idx, gate = lax.fori_loop(0, ne, per_expert,
                              (jnp.zeros((ne, cap), jnp.int32), jnp.zeros((ne, cap), F32)))
    idx_ref[0] = idx
    gate_ref[0] = gate


def _expert_topk(aff_t, cap):
    b, ne, n = aff_t.shape
    nb = n // LANES
    return pl.pallas_call(
        functools.partial(_topk_kernel, cap),
        grid=(b,),
        in_specs=[pl.BlockSpec((1, ne, n), lambda i: (i, 0, 0))],
        out_specs=[pl.BlockSpec((1, ne, cap), lambda i: (i, 0, 0))] * 2,
        out_shape=[jax.ShapeDtypeStruct((b, ne, cap), jnp.int32), jax.ShapeDtypeStruct((b, ne, cap), F32)],
        scratch_shapes=[pltpu.VMEM((ne, n), F32), pltpu.VMEM((ne, nb, LANES), F32), pltpu.VMEM((ne, nb, LANES), F32)],
        compiler_params=_cparams(("parallel",)),
        name="expert_topk",
    )(aff_t)


MOE_FF_TILE = 768
MOE_UNROLL = 8


def _moe_kernel(cap, idx_ref, gate_ref, h_ref, wg_ref, wu_ref, wd_ref, o_ref, xt_ref, x_ref, y_ref, yt_ref):
    e = pl.program_id(1)
    f = pl.program_id(2)
    nf = pl.num_programs(2)

    @pl.when((e == 0) & (f == 0))
    def _():
        o_ref[...] = jnp.zeros(o_ref.shape, o_ref.dtype)

    @pl.when(f == 0)
    def _():
        def gather(i, carry):
            for u in range(MOE_UNROLL):
                r = i * MOE_UNROLL + u
                xt_ref[_tile(r), :] = h_ref[0, _tile(idx_ref[0, 0, 0, r]), :]
            return carry
        lax.fori_loop(0, cap // MOE_UNROLL, gather, 0)
        x_ref[...] = _from_tile_major(xt_ref, cap).astype(x_ref.dtype)

    x = x_ref[...]
    hid = (_silu(_dot(x, wg_ref[0])) * _dot(x, wu_ref[0])).astype(BF16)
    part = _dot(hid, wd_ref[0])

    @pl.when(f == 0)
    def _():
        y_ref[...] = part

    @pl.when(f > 0)
    def _():
        y_ref[...] += part

    @pl.when(f == nf - 1)
    def _():
        _to_tile_major(yt_ref, y_ref[...])

        def scatter(i, carry):
            rows = [i * MOE_UNROLL + u for u in range(MOE_UNROLL)]
            toks = [idx_ref[0, 0, 0, r] for r in rows]
            new = [o_ref[0, _tile(t), :] + gate_ref[0, 0, 0, r] * yt_ref[_tile(r), :] for r, t in zip(rows, toks)]
            for t, val in zip(toks, new):
                o_ref[0, _tile(t), :] = val
            return carry
        lax.fori_loop(0, cap // MOE_UNROLL, scatter, 0)


def _moe(idx, gate, h2t, wg, wu, wd):
    b, rows, _ = h2t.shape
    ne, d, ffp = wg.shape
    cap = idx.shape[-1]
    tf = MOE_FF_TILE
    smem = lambda: pl.BlockSpec((1, 1, 1, cap), lambda i, j, k: (i, j, 0, 0), memory_space=pltpu.SMEM)
    act = lambda: pl.BlockSpec((1, rows, LANES), lambda i, j, k: (i, 0, 0), pipeline_mode=pl.Buffered(1))
    return pl.pallas_call(
        functools.partial(_moe_kernel, cap),
        grid=(b, ne, ffp // tf),
        in_specs=[smem(), smem(), act(),
                  pl.BlockSpec((1, d, tf), lambda i, j, k: (j, 0, k)),
                  pl.BlockSpec((1, d, tf), lambda i, j, k: (j, 0, k)),
                  pl.BlockSpec((1, tf, d), lambda i, j, k: (j, k, 0))],
        out_specs=act(),
        out_shape=jax.ShapeDtypeStruct((b, rows, LANES), F32),
        scratch_shapes=[pltpu.VMEM((cap * SUBLANES, LANES), F32), pltpu.VMEM((cap, d), BF16),
                        pltpu.VMEM((cap, d), F32), pltpu.VMEM((cap * SUBLANES, LANES), F32)],
        compiler_params=_cparams(("parallel", "arbitrary", "arbitrary")),
        name="expert_ffn",
    )(idx[:, :, None, :], gate[:, :, None, :], h2t, wg, wu, wd)


def _final_kernel(xm_ref, moe_ref, g_ref, o_ref):
    o_ref[0] = xm_ref[0] + g_ref[0] * _from_tile_major(moe_ref, xm_ref.shape[1], lead=(0,))


def _final(xmid, moe_t, g2, tl):
    b, n, d = xmid.shape
    row = pl.BlockSpec((1, tl, d), lambda i, j: (i, j, 0))
    return pl.pallas_call(
        _final_kernel,
        grid=(b, n // tl),
        in_specs=[row, pl.BlockSpec((1, tl * SUBLANES, LANES), lambda i, j: (i, j, 0)),
                  pl.BlockSpec((1, 1, d), lambda i, j: (i, 0, 0))],
        out_specs=row,
        out_shape=jax.ShapeDtypeStruct((b, n, d), F32),
        compiler_params=_cparams(("parallel", "parallel")),
        name="final_residual",
    )(xmid, moe_t, g2)


def _rope_tables(n_lat):
    rows = n_lat // GRID_W
    row = jnp.repeat(jnp.arange(rows), GRID_W).astype(F32)
    col = jnp.broadcast_to(jnp.arange(GRID_W), (rows, GRID_W)).reshape(-1).astype(F32)
    n_freq = MLA_ROPE // 4
    inv_freq = ROPE_BASE ** (-jnp.arange(n_freq, dtype=F32) / n_freq)
    ang_r = row[:, None] * inv_freq
    ang_c = col[:, None] * inv_freq
    ang = jnp.concatenate([ang_r, ang_r, ang_c, ang_c], axis=-1)
    zeros = jnp.zeros((n_lat, LANES - MLA_ROPE), F32)
    return jnp.concatenate([jnp.cos(ang), zeros], axis=-1), jnp.concatenate([jnp.sin(ang), zeros], axis=-1)


def _rotate_half(v, signed=True):
    x1, x2, x3, x4 = jnp.split(v, 4, axis=-1)
    sgn = -1.0 if signed else 1.0
    return jnp.concatenate([sgn * x2, x1, sgn * x4, x3], axis=-1)


def _pad_lanes(v, width):
    return jnp.concatenate([v, jnp.zeros(v.shape[:-1] + (width - v.shape[-1],), v.dtype)], axis=-1)


def kernel(x, c, ctx, c_ctx, ada_w, ada_b, norm1_g, norm2_g, w_in, conv_w, a_log, dt_bias, dn_norm_g, w_out_a, q_a_norm_g, w_uq, kv_a_norm_g, w_ukv, q_norm_g, k_norm_g, w_out_b, w_o, router_w, w_gate, w_up, w_down):
    assert ada_w.shape[0] == 1, "single-layer block"
    b, n_lat, d = x.shape
    n_ctx = ctx.shape[1]
    hh, hd = N_HEADS, HEAD_DIM
    qkv_dim = 3 * hh * hd
    nb = 2 * hh

    n_cond = -(-(b + 1) // SUBLANES) * SUBLANES
    cond = jnp.concatenate([c, c_ctx[None], jnp.zeros((n_cond - b - 1, d), F32)], axis=0)
    mod = _modulation(cond, ada_w[0], ada_b[0])
    mods = [mod[:, i * d:(i + 1) * d] for i in range(6)]
    lat = lambda m: m[:b, None, :]
    cvec = lambda m: jnp.broadcast_to(m[b][None, None, :], (b, 1, d))

    o_z = qkv_dim
    o_alpha = o_z + hh * hd
    o_beta = o_alpha + nb
    o_cq = o_beta + nb
    o_ckv = o_cq + MLA_Q_RANK
    o_kr = o_ckv + MLA_KV_RANK
    o_gate = o_kr + MLA_ROPE
    w = w_in[0]
    w_big = jnp.concatenate([w[:, :o_alpha], w[:, o_gate:], w[:, o_cq:o_ckv]], axis=1).astype(BF16)
    small_w = _pad_lanes(w[:, o_ckv:o_gate], MLA_KV_RANK + LANES).astype(BF16)
    w_ab_t = jnp.transpose(w[:, o_alpha:o_cq]).astype(BF16)
    gate_pad = jnp.zeros((nb, 1), F32)
    alog = jnp.concatenate([a_log[0].reshape(nb, 1), gate_pad], axis=0)
    dtb = jnp.concatenate([dt_bias[0].reshape(nb, 1), gate_pad], axis=0)
    widths = (qkv_dim, hh * hd, N_BRANCHES * d, MLA_Q_RANK)
    g1 = norm1_g[0][None]
    qkv_x, z_x, gt_x, cq_x, sm_x, dg_x = _in_projection(x, lat(mods[0]), lat(mods[1]), g1, w_big, small_w, w_ab_t,
                                                        alog, dtb, widths, 512)
    qkv_c, _, _, cq_c, sm_c, dg_c = _in_projection(ctx, cvec(mods[0]), cvec(mods[1]), g1, w_big, small_w, w_ab_t,
                                                   alog, dtb, widths, min(n_ctx, 512))

    y_a = _deltanet(qkv_x, qkv_c, conv_w[0], dg_x, dg_c, z_x, dn_norm_g[0][None])

    wq = w_uq[0].reshape(MLA_Q_RANK, hh, MLA_QK_DIM)
    wq = jnp.concatenate([wq[:, :, :hd].reshape(MLA_Q_RANK, hh * hd),
                          _pad_lanes(wq[:, :, hd:], hd).reshape(MLA_Q_RANK, hh * hd),
                          _pad_lanes(_rotate_half(wq[:, :, hd:]), hd).reshape(MLA_Q_RANK, hh * hd)],
                         axis=1).astype(BF16)
    wkv = w_ukv[0].reshape(MLA_KV_RANK, hh, 2 * hd)
    wk = wkv[:, :, :hd].reshape(MLA_KV_RANK, hh * hd).astype(BF16)
    wvt = jnp.transpose(wkv[:, :, hd:].reshape(MLA_KV_RANK, hh * hd)).astype(BF16)
    gq = jnp.concatenate([_pad_lanes(q_norm_g[0], 2 * hd),
                          _pad_lanes(_rotate_half(q_norm_g[0][hd:], signed=False), hd)])[None]
    gk = _pad_lanes(k_norm_g[0], 2 * hd)[None]
    gqa, gkva = q_a_norm_g[0][None], kv_a_norm_g[0][None]
    cos, sin = _rope_tables(n_lat)
    cos_c = _pad_lanes(jnp.ones((n_ctx, MLA_ROPE), F32), LANES)
    sin_c = jnp.zeros((n_ctx, LANES), F32)
    q_x, k_x, v_x = _mla_projections(cq_x, sm_x, gqa, gkva, wq, wk, wvt, gq, gk, cos, sin, True, 512)
    k_c, v_c = _mla_projections(cq_c, sm_c, gqa, gkva, wq, wk, wvt, gq, gk, cos_c, sin_c, False, min(n_ctx, 512))
    y_b = _attention(q_x, k_c, v_c, k_x, v_x, min(n_lat, 4096))

    rwt = jnp.transpose(router_w[0])
    x_mid, h2, aff_t = _merge(y_a, y_b, gt_x, x, lat(mods[2]), lat(mods[3]), lat(mods[4]), norm2_g[0][None],
                              w_out_a[0].astype(BF16), w_out_b[0].astype(BF16), w_o[0].astype(BF16), rwt, 512)

    cap = EC_CAPACITY * n_lat // N_EXPERTS
    idx, gate = _expert_topk(aff_t, cap)
    ff = w_gate.shape[-1]
    ffp = -(-ff // MOE_FF_TILE) * MOE_FF_TILE
    wg = _pad_lanes(w_gate[0], ffp).astype(BF16)
    wu = _pad_lanes(w_up[0], ffp).astype(BF16)
    wd = jnp.concatenate([w_down[0], jnp.zeros((N_EXPERTS, ffp - ff, d), F32)], axis=1).astype(BF16)
    moe = _moe(idx, gate, h2, wg, wu, wd)
    return _final(x_mid, moe, lat(mods[5]), 512)
```

```python
import functools
import math

import jax
import jax.numpy as jnp
from jax import lax
from jax.experimental import pallas as pl
from jax.experimental.pallas import tpu as pltpu

F32 = jnp.float32
BF16 = jnp.bfloat16
F8 = jnp.float8_e4m3fn
EPS = 1e-6

N_HEADS = 8
HEAD_DIM = 128
DN_CONV = 5
DN_CHUNK = 64
MLA_Q_RANK = 384
MLA_KV_RANK = 256
MLA_ROPE = 64
MLA_QK_DIM = HEAD_DIM + MLA_ROPE
GRID_W = 64
ROPE_BASE = 10000.0
N_EXPERTS = 16
EC_CAPACITY = 2
N_BRANCHES = 2

LANES = 128
SUBLANES = 8
VMEM_LIMIT_BYTES = 56 * 1024 * 1024

ROW_TILE = 512
ATTN_Q_TILE = 4096

HI = lax.Precision.HIGHEST


def _cparams(sem):
    return pltpu.CompilerParams(dimension_semantics=sem, vmem_limit_bytes=VMEM_LIMIT_BYTES)


def _dot(a, b, precision=None):
    return jnp.dot(a, b, preferred_element_type=F32, precision=precision)


def _dot_nt(a, b, precision=None):
    return lax.dot_general(a, b, (((1,), (1,)), ((), ())), preferred_element_type=F32, precision=precision)


def _dot_tn(a, b):
    return lax.dot_general(a, b, (((0,), (0,)), ((), ())), preferred_element_type=F32)


def _sigmoid(x):
    return 1.0 / (1.0 + jnp.exp(-x))


def _silu(x):
    return x * _sigmoid(x)


def _const_spec(shape):
    nd = len(shape)
    return pl.BlockSpec(shape, lambda *_: (0,) * nd, pipeline_mode=pl.Buffered(1))


def _to_tile_major(ref, x, lead=()):
    rows = x.shape[0]
    for s_ in range(SUBLANES):
        ref[lead + (pl.ds(s_, rows, stride=SUBLANES), slice(None))] = x[:, s_ * LANES:(s_ + 1) * LANES]


def _from_tile_major(ref, rows, lead=()):
    return jnp.concatenate([ref[lead + (pl.ds(s_, rows, stride=SUBLANES), slice(None))] for s_ in range(SUBLANES)],
                           axis=1)


def _tile(i):
    return pl.ds(pl.multiple_of(i * SUBLANES, SUBLANES), SUBLANES)


def _mod_kernel(c_ref, w_ref, b_ref, o_ref):
    c = c_ref[...]
    o_ref[...] = _dot(_silu(c), w_ref[...], precision=HI) + b_ref[...]


def _modulation(cond, ada_w, ada_b):
    n, d = cond.shape
    n_out = ada_w.shape[1]
    tn = d
    return pl.pallas_call(
        _mod_kernel,
        grid=(n_out // tn,),
        in_specs=[pl.BlockSpec((n, d), lambda j: (0, 0)),
                  pl.BlockSpec((d, tn), lambda j: (0, j)),
                  pl.BlockSpec((1, tn), lambda j: (0, j))],
        out_specs=pl.BlockSpec((n, tn), lambda j: (0, j)),
        out_shape=jax.ShapeDtypeStruct((n, n_out), F32),
        compiler_params=_cparams(("parallel",)),
        name="modulation",
    )(cond, ada_w, ada_b.reshape(1, n_out))


def _modulated_norm(x, g, shift, scale):
    ms = jnp.mean(x * x, axis=-1, keepdims=True)
    return (x * lax.rsqrt(ms + EPS)) * g * (1.0 + scale) + shift


def _inproj_kernel(widths, x_ref, shift_ref, scale_ref, g_ref, wb_ref, ws_ref, wab_ref, alog_ref, dtb_ref,
                   *out_refs):
    big_refs, small_ref, gates_ref = out_refs[:-2], out_refs[-2], out_refs[-1]
    h = _modulated_norm(x_ref[0], g_ref[...], shift_ref[0], scale_ref[0]).astype(BF16)
    off = 0
    for o_ref, width in zip(big_refs, widths):
        for c0 in range(0, width, 1024):
            c1 = min(c0 + 1024, width)
            o_ref[0, :, c0:c1] = _dot(h, wb_ref[:, off + c0:off + c1]).astype(o_ref.dtype)
        off += width
    small_ref[0] = _dot(h, ws_ref[...])
    ab = _dot_nt(wab_ref[...], h)
    row_id = lax.broadcasted_iota(jnp.int32, ab.shape, 0)
    xs = ab + dtb_ref[...]
    softplus = jnp.maximum(xs, 0.0) + jnp.log1p(jnp.exp(-jnp.abs(xs)))
    gates_ref[0] = jnp.where(row_id < 2 * N_HEADS, -jnp.exp(alog_ref[...]) * softplus, _sigmoid(ab))


def _in_projection(x, shift, scale, g, w_big, w_small, w_ab_t, alog, dtb, widths, tl):
    b, n, d = x.shape
    row = lambda w: pl.BlockSpec((1, tl, w), lambda i, j: (i, j, 0))
    vec = pl.BlockSpec((1, 1, d), lambda i, j: (i, 0, 0))
    n_gate = w_ab_t.shape[0]
    outs = [jax.ShapeDtypeStruct((b, n, w), BF16) for w in widths]
    outs.append(jax.ShapeDtypeStruct((b, n, w_small.shape[1]), F32))
    outs.append(jax.ShapeDtypeStruct((b, n_gate, n), F32))
    return pl.pallas_call(
        functools.partial(_inproj_kernel, widths),
        grid=(b, n // tl),
        in_specs=[row(d), vec, vec, _const_spec((1, d)), _const_spec(w_big.shape), _const_spec(w_small.shape),
                  _const_spec(w_ab_t.shape), _const_spec(alog.shape), _const_spec(dtb.shape)],
        out_specs=[row(w) for w in widths] + [row(w_small.shape[1]),
                                              pl.BlockSpec((1, n_gate, tl), lambda i, j: (i, 0, j))],
        out_shape=outs,
        compiler_params=_cparams(("parallel", "parallel")),
        name="in_projection",
    )(x, shift, scale, g, w_big, w_small, w_ab_t, alog, dtb)


CONV_ROWS = 512
CONV_PAD = 8


def _conv_fill(raw_ref, pad_ref, n_tok):
    zeros = jnp.zeros((CONV_PAD, LANES), F32)
    pad_ref[0:CONV_PAD, :] = zeros
    pad_ref[CONV_PAD + n_tok:2 * CONV_PAD + n_tok, :] = zeros
    pad_ref[CONV_PAD:CONV_PAD + n_tok, :] = raw_ref[0].astype(F32)


def _conv_block(pad_ref, w, r0, rows, normalize):
    half = DN_CONV // 2
    acc = None
    for j in range(DN_CONV):
        term = pad_ref[pl.ds(r0 + (CONV_PAD + j - half), rows), :] * w[j:j + 1, :]
        acc = term if acc is None else acc + term
    y = _silu(acc)
    if normalize:
        y = y * lax.rsqrt(jnp.sum(y * y, axis=-1, keepdims=True) + EPS)
    return y


def _conv_silu(raw_ref, w_ref, pad_ref, out_ref, n_tok, normalize):
    _conv_fill(raw_ref, pad_ref, n_tok)
    w = w_ref[...]
    rb = min(CONV_ROWS, n_tok)
    for r0 in range(0, n_tok, rb):
        out_ref[r0:r0 + rb, :] = _conv_block(pad_ref, w, r0, rb, normalize)


def _dn_masks():
    n = 2 * DN_CHUNK
    ri = lax.broadcasted_iota(jnp.int32, (n, n), 0)
    ci = lax.broadcasted_iota(jnp.int32, (n, n), 1)
    top = ri < DN_CHUNK
    same = (ri // DN_CHUNK) == (ci // DN_CHUNK)
    sgn = jnp.where(top, 1, -1)
    delta = (ri - ci) * sgn
    incl = same & (delta >= 0)
    strict = same & (delta > 0)
    incl_t = same & (delta <= 0)
    eye = ri == ci
    return top, incl, strict, incl_t, eye


DN_INV_BLOCK = 8
DN_PREP_GROUP = 16
DN_SCAN_UNROLL = 3


def _hi_lo(a_f32):
    hi = a_f32.astype(BF16)
    return hi, (a_f32 - hi.astype(F32)).astype(BF16)


def _split_dot(a_f32, b_bf16):
    hi, lo = _hi_lo(a_f32)
    return _dot(hi, b_bf16) + _dot(lo, b_bf16)


def _bdot(a_f32, b_f32):
    return _dot(a_f32.astype(BF16), b_f32.astype(BF16))


def _dn_gate_rows(c0, group, gt_ref):
    n = 2 * DN_CHUNK
    t0 = pl.multiple_of(c0 * DN_CHUNK, n)
    lane = lax.broadcasted_iota(jnp.int32, (1, n), 1)
    rows = [[gt_ref[0, kind, dr, 0, :, pl.ds(t0, group * DN_CHUNK)] for dr in range(2)] for kind in range(2)]
    out = []
    for j in range(group):
        p, odd = divmod(j, 2)
        both = []
        for kind in range(2):
            fwd = rows[kind][0][:, p * n:(p + 1) * n]
            bwd = rows[kind][1][:, p * n:(p + 1) * n]
            if odd:
                both.append(jnp.where(lane < DN_CHUNK, pltpu.roll(fwd, DN_CHUNK, axis=1), bwd))
            else:
                both.append(jnp.where(lane < DN_CHUNK, fwd, pltpu.roll(bwd, DN_CHUNK, axis=1)))
        out.append(jnp.concatenate(both, axis=0))
    return out


def _dn_prep_load(c, q_ref, k_ref, v_ref):
    r0 = pl.multiple_of(c * DN_CHUNK, DN_CHUNK)
    return q_ref[pl.ds(r0, DN_CHUNK), :], k_ref[pl.ds(r0, DN_CHUNK), :], v_ref[pl.ds(r0, DN_CHUNK), :]


def _dn_prep_compute(q, k, v, gb):
    n = 2 * DN_CHUNK
    top, incl, strict, incl_t, eye = _dn_masks()
    g_row = jnp.broadcast_to(gb[0:1, :], (SUBLANES, n))
    beta_rows = jnp.broadcast_to(gb[1:2, :], (n, n))
    kst = jnp.concatenate([k, k], axis=0)
    qst = jnp.concatenate([q, q], axis=0)
    vst = jnp.concatenate([v, v], axis=0)
    kst_b = kst.astype(BF16)

    gc_row = _split_dot(g_row, jnp.where(incl_t, 1.0, 0.0).astype(BF16))[0:1, :]
    kk = _dot_nt(kst_b, kst_b)
    qk = _dot_nt(qst.astype(BF16), kst_b)
    yield
    c2 = jnp.broadcast_to(gc_row, (n, n))
    c1 = c2.T
    beta_c = beta_rows.T
    tot = jnp.where(top, c1[DN_CHUNK - 1:DN_CHUNK, :], c1[DN_CHUNK:DN_CHUNK + 1, :])
    decay = jnp.exp(jnp.where(incl, c1 - c2, -jnp.inf))
    e_gc = jnp.exp(c1)
    e_rest = jnp.exp(tot - c1)
    scale = HEAD_DIM ** -0.5

    m = jnp.where(strict, kk * beta_c * decay, 0.0)
    ri = lax.broadcasted_iota(jnp.int32, (n, n), 0)
    ci = lax.broadcasted_iota(jnp.int32, (n, n), 1)
    blk = DN_INV_BLOCK
    dg = jnp.where((ri // blk) == (ci // blk), m, 0.0)
    t = jnp.where(eye, 1.0, 0.0) - dg
    dp = _bdot(dg, dg)
    yield
    for it in range(blk.bit_length() - 2):
        t = t + _bdot(t, dp)
        if it + 1 < blk.bit_length() - 2:
            dp = _bdot(dp, dp)
        yield
    while blk < DN_CHUNK:
        off = jnp.where(((ri // (2 * blk)) == (ci // (2 * blk))) & ((ri // blk) != (ci // blk)), m, 0.0)
        ta = _bdot(t, off)
        yield
        t = t - _bdot(ta, t)
        yield
        blk *= 2

    kb = kst * beta_c
    rhs = jnp.concatenate([vst * beta_c, kb * e_gc], axis=1)
    uw = _bdot(t, rhs)
    yield
    d = HEAD_DIM
    u_b = uw[:, 0:d].astype(BF16)
    w_b = uw[:, d:2 * d].astype(BF16)
    kd = (kst * e_rest).astype(BF16)
    qkm = (jnp.where(incl, qk * decay, 0.0) * scale).astype(BF16)
    ri2 = lax.broadcasted_iota(jnp.int32, (n, 2 * d), 0)
    ci2 = lax.broadcasted_iota(jnp.int32, (n, 2 * d), 1)
    own_dir = (ri2 // DN_CHUNK) == (ci2 // d)
    zero = jnp.zeros((), BF16)
    wu_bd = jnp.concatenate([jnp.where(own_dir, jnp.concatenate([w_b, w_b], axis=1), zero),
                             jnp.where(own_dir, jnp.concatenate([u_b, u_b], axis=1), zero)], axis=1)
    kw = _dot_tn(kd, wu_bd)
    qw = _dot(qkm, jnp.concatenate([w_b, u_b], axis=1))
    yield
    return ((-kw[:, 0:2 * d]).astype(BF16),
            kw[:, 2 * d:4 * d],
            (qst * e_gc * scale - qw[:, 0:d]).astype(BF16),
            qw[:, d:2 * d],
            jnp.exp(tot)[DN_CHUNK - 4:DN_CHUNK + 4, :])


def _dn_prep_group(c0, group, in_refs, out_refs):
    *qkv_refs, gt_ref = in_refs
    gates = _dn_gate_rows(c0, group, gt_ref)
    gens = [_dn_prep_compute(*_dn_prep_load(c0 + j, *qkv_refs), gates[j]) for j in range(group)]
    results = [None] * group
    while any(r is None for r in results):
        for j, gen in enumerate(gens):
            try:
                next(gen)
            except StopIteration as done:
                results[j] = done.value
    for j, res in enumerate(results):
        for ref, val in zip(out_refs, res):
            ref[c0 + j] = val


def _dn_scan_step(cf, cb, a_ref, b_ref, q_ref, o0_ref, ar_ref, s_ref, of_ref, ob_ref):
    h = DN_CHUNK
    d = HEAD_DIM
    s = s_ref[...]
    s_b = s.astype(BF16)
    r_f = _dot(jnp.concatenate([a_ref[cf, :, 0:d], q_ref[cf, 0:h, :]], axis=0), s_b[:, 0:d])
    r_b = _dot(jnp.concatenate([a_ref[cb, :, d:2 * d], q_ref[cb, h:2 * h, :]], axis=0), s_b[:, d:2 * d])
    if of_ref is not None:
        of_ref[pl.ds(pl.multiple_of(cf * h, h), h), :] = r_f[d:d + h, :] + o0_ref[cf, 0:h, :]
        ob_ref[pl.ds(pl.multiple_of(cb * h, h), h), :] = r_b[d:d + h, :] + o0_ref[cb, h:2 * h, :]
    b_sel = jnp.concatenate([b_ref[cf, :, 0:d], b_ref[cb, :, d:2 * d]], axis=1)
    a = jnp.concatenate([jnp.broadcast_to(ar_ref[cf, 0:1, :], (d, d)),
                         jnp.broadcast_to(ar_ref[cb, 4:5, :], (d, d))], axis=1)
    s_ref[...] = s * a + jnp.concatenate([r_f[0:d, :], r_b[0:d, :]], axis=1) + b_sel


def _dn_kernel(n_lat, n_ctx,
               qx_ref, kx_ref, vx_ref, qc_ref, kc_ref, vc_ref, wq_ref, wk_ref, wv_ref,
               gbx_ref, gbc_ref, z_ref, ng_ref, y_ref,
               pad_ref, q_s, k_s, v_s, qc_s, kc_s, vc_s,
               a_x, b_x, qp_x, o0_x, ar_x, a_c, b_c, qp_c, o0_c, ar_c,
               s_ref, of_ref, ob_ref):
    nc_x = n_lat // DN_CHUNK
    nc_c = n_ctx // DN_CHUNK

    _conv_silu(qc_ref, wq_ref, pad_ref, qc_s, n_ctx, True)
    _conv_silu(kc_ref, wk_ref, pad_ref, kc_s, n_ctx, True)
    _conv_silu(vc_ref, wv_ref, pad_ref, vc_s, n_ctx, False)
    _conv_silu(qx_ref, wq_ref, pad_ref, q_s, n_lat, True)
    _conv_silu(kx_ref, wk_ref, pad_ref, k_s, n_lat, True)
    _conv_silu(vx_ref, wv_ref, pad_ref, v_s, n_lat, False)
    grp_c = math.gcd(nc_c, DN_PREP_GROUP)
    grp_x = math.gcd(nc_x, DN_PREP_GROUP)
    assert grp_c % 2 == 0 and grp_x % 2 == 0, "a group's tokens must cover whole 128-lane tiles"
    assert nc_x % 2 == 0

    def prep_c(i, carry):
        _dn_prep_group(i * grp_c, grp_c, (qc_s, kc_s, vc_s, gbc_ref), (a_c, b_c, qp_c, o0_c, ar_c))
        return carry

    def prep_x(i, carry):
        _dn_prep_group(i * grp_x, grp_x, (q_s, k_s, v_s, gbx_ref), (a_x, b_x, qp_x, o0_x, ar_x))
        return carry

    lax.fori_loop(0, nc_c // grp_c, prep_c, 0)
    lax.fori_loop(0, nc_x // grp_x, prep_x, 0)

    s_ref[...] = jnp.zeros(s_ref.shape, F32)

    def scan_c(i, carry):
        _dn_scan_step(i, nc_c - 1 - i, a_c, b_c, qp_c, o0_c, ar_c, s_ref, None, None)
        return carry

    def scan_x(i, carry):
        _dn_scan_step(i, nc_x - 1 - i, a_x, b_x, qp_x, o0_x, ar_x, s_ref, of_ref, ob_ref)
        return carry

    def finish(c):
        start = c * DN_CHUNK
        rows = pl.ds(start if isinstance(c, int) else pl.multiple_of(start, DN_CHUNK), DN_CHUNK)
        o = of_ref[rows, :] + ob_ref[rows, :]
        ms = jnp.mean(o * o, axis=-1, keepdims=True)
        y = (o * lax.rsqrt(ms + EPS)) * ng_ref[...] * _silu(z_ref[0, rows, :].astype(F32))
        y_ref[0, rows, :] = y.astype(y_ref.dtype)

    def scan_x_finish(i, carry):
        finish(i - 1)
        finish(nc_x - i)
        scan_x(i, carry)
        return carry

    half = nc_x // 2
    lax.fori_loop(0, nc_c, scan_c, 0)
    lax.fori_loop(0, half + 1, scan_x, 0, unroll=DN_SCAN_UNROLL if (half + 1) % DN_SCAN_UNROLL == 0 else 1)
    lax.fori_loop(half + 1, nc_x, scan_x_finish, 0)
    finish(nc_x - 1)
    finish(0)


def _deltanet(qkv_x, qkv_c, conv_w, gt_x, gt_c, z, norm_g):
    b, n_lat, _ = qkv_x.shape
    n_ctx = qkv_c.shape[1]
    gbx = gt_x.reshape(b, 2, 2, N_HEADS, 1, n_lat)
    gbc = gt_c.reshape(b, 2, 2, N_HEADS, 1, n_ctx)
    hh = N_HEADS
    d = HEAD_DIM
    nc_x, nc_c = n_lat // DN_CHUNK, n_ctx // DN_CHUNK
    tok = lambda n, off: pl.BlockSpec((1, n, d), lambda i, j: (i, 0, off + j))
    cw = lambda off: pl.BlockSpec((DN_CONV, d), lambda i, j: (0, off + j))
    gspec = lambda n: pl.BlockSpec((1, 2, 2, 1, 1, n), lambda i, j: (i, 0, 0, j, 0, 0))
    chunk_scratch = lambda nc: [
        pltpu.VMEM((nc, d, 2 * d), BF16),
        pltpu.VMEM((nc, d, 2 * d), F32),
        pltpu.VMEM((nc, 2 * DN_CHUNK, d), BF16),
        pltpu.VMEM((nc, 2 * DN_CHUNK, d), F32),
        pltpu.VMEM((nc, SUBLANES, 2 * DN_CHUNK), F32),
    ]
    return pl.pallas_call(
        functools.partial(_dn_kernel, n_lat, n_ctx),
        grid=(b, hh),
        in_specs=[tok(n_lat, 0), tok(n_lat, hh), tok(n_lat, 2 * hh),
                  tok(n_ctx, 0), tok(n_ctx, hh), tok(n_ctx, 2 * hh),
                  cw(0), cw(hh), cw(2 * hh),
                  gspec(n_lat), gspec(n_ctx),
                  tok(n_lat, 0), _const_spec((1, d))],
        out_specs=tok(n_lat, 0),
        out_shape=jax.ShapeDtypeStruct((b, n_lat, hh * d), BF16),
        scratch_shapes=[pltpu.VMEM((n_lat + 2 * CONV_PAD, d), F32)]
        + [pltpu.VMEM((n_lat, d), F32)] * 3 + [pltpu.VMEM((n_ctx, d), F32)] * 3
        + chunk_scratch(nc_x) + chunk_scratch(nc_c)
        + [pltpu.VMEM((d, 2 * d), F32), pltpu.VMEM((n_lat, d), F32), pltpu.VMEM((n_lat, d), F32)],
        compiler_params=_cparams(("parallel", "parallel")),
        name="deltanet",
    )(qkv_x, qkv_x, qkv_x, qkv_c, qkv_c, qkv_c, conv_w, conv_w, conv_w, gbx, gbc, z, norm_g)


ATTN_ONES_ROWS = 32


def _rope(x, cos, sin):
    lane = lax.broadcasted_iota(jnp.int32, x.shape, 1)
    quarter = MLA_ROPE // 4
    rot = jnp.where((lane // quarter) % 2 == 0,
                    -pltpu.roll(x, LANES - quarter, axis=1), pltpu.roll(x, quarter, axis=1))
    return x * cos + rot * sin


def _rms(x, n):
    return lax.rsqrt(jnp.sum(x * x, axis=-1, keepdims=True) / n + EPS)


def _mla_proj_kernel(with_q, cq_ref, small_ref, gqa_ref, gkva_ref, wq_ref, wk_ref, wvt_ref, gq_ref, gk_ref,
                     cos_ref, sin_ref, *out_refs):
    hd = HEAD_DIM
    hh = N_HEADS
    cos, sin = cos_ref[...], sin_ref[...]
    small = small_ref[0]
    ckv = small[:, 0:MLA_KV_RANK]
    kpe = small[:, MLA_KV_RANK:MLA_KV_RANK + LANES]
    lane = lax.broadcasted_iota(jnp.int32, kpe.shape, 1)
    kpe = jnp.where(lane < MLA_ROPE, kpe, 0.0)
    kpe_ssq = jnp.sum(kpe * kpe, axis=-1, keepdims=True)
    ckvn = (ckv * _rms(ckv, MLA_KV_RANK) * gkva_ref[...]).astype(BF16)
    kf = _dot(ckvn, wk_ref[...])
    vt = _dot_nt(wvt_ref[...], ckvn)
    if with_q:
        q_ref, k_ref, v_ref = out_refs
        cq = cq_ref[0].astype(F32)
        cqn = (cq * _rms(cq, MLA_Q_RANK) * gqa_ref[...]).astype(BF16)
        qf = _dot(cqn, wq_ref[...])
        q_scale = MLA_QK_DIM ** -0.5 * math.log2(math.e) * ATTN_QK_BALANCE
    else:
        k_ref, v_ref = out_refs
    gq, gk = gq_ref[...], gk_ref[...]
    kpe_rot = _rope(kpe * gk[:, hd:2 * hd], cos, sin)
    for h in range(hh):
        if with_q:
            qn = qf[:, h * hd:(h + 1) * hd]
            qp = qf[:, (hh + h) * hd:(hh + h + 1) * hd]
            ssq = jnp.sum(qn * qn + qp * qp, axis=-1, keepdims=True)
            r = lax.rsqrt(ssq / MLA_QK_DIM + EPS)
            q_ref[0, h, :, 0:hd] = (qn * r * gq[:, 0:hd] * q_scale).astype(q_ref.dtype)
            qr = qf[:, (2 * hh + h) * hd:(2 * hh + h + 1) * hd]
            q_pe = qp * (gq[:, hd:2 * hd] * cos) + qr * (gq[:, 2 * hd:3 * hd] * sin)
            q_ref[0, h, :, hd:2 * hd] = (q_pe * (r * q_scale)).astype(q_ref.dtype)
        kn = kf[:, h * hd:(h + 1) * hd]
        r = lax.rsqrt((jnp.sum(kn * kn, axis=-1, keepdims=True) + kpe_ssq) / MLA_QK_DIM + EPS)
        rk = r * (1.0 / ATTN_QK_BALANCE)
        k_ref[0, h, :, 0:hd] = (kn * rk * gk[:, 0:hd]).astype(k_ref.dtype)
        k_ref[0, h, :, hd:2 * hd] = (kpe_rot * rk).astype(k_ref.dtype)
        v_ref[0, h, 0:hd, :] = vt[h * hd:(h + 1) * hd, :].astype(v_ref.dtype)
        v_ref[0, h, hd:hd + ATTN_ONES_ROWS, :] = jnp.ones((ATTN_ONES_ROWS, vt.shape[1]), v_ref.dtype)


def _mla_projections(cq, small, gqa, gkva, wq, wk, wvt, gq, gk, cos, sin, with_q, tl):
    b, n, _ = small.shape
    hh, hd = N_HEADS, HEAD_DIM
    row = lambda w: pl.BlockSpec((1, tl, w), lambda i, j: (i, j, 0))
    head = lambda w: pl.BlockSpec((1, hh, tl, w), lambda i, j: (i, 0, j, 0))
    tab = pl.BlockSpec((tl, LANES), lambda i, j: (j, 0))
    hv = hd + ATTN_ONES_ROWS
    out_specs = [head(2 * hd), pl.BlockSpec((1, hh, hv, tl), lambda i, j: (i, 0, 0, j))]
    out_shape = [jax.ShapeDtypeStruct((b, hh, n, 2 * hd), F8), jax.ShapeDtypeStruct((b, hh, hv, n), F8)]
    if with_q:
        out_specs = [head(2 * hd)] + out_specs
        out_shape = [jax.ShapeDtypeStruct((b, hh, n, 2 * hd), F8)] + out_shape
    return pl.pallas_call(
        functools.partial(_mla_proj_kernel, with_q),
        grid=(b, n // tl),
        in_specs=[row(cq.shape[-1]), row(small.shape[-1]), _const_spec(gqa.shape), _const_spec(gkva.shape),
                  _const_spec(wq.shape), _const_spec(wk.shape), _const_spec(wvt.shape), _const_spec(gq.shape),
                  _const_spec(gk.shape),
                  tab, tab],
        out_specs=out_specs,
        out_shape=out_shape,
        compiler_params=_cparams(("parallel", "parallel")),
        name="mla_projections_q" if with_q else "mla_projections_kv",
    )(cq, small, gqa, gkva, wq, wk, wvt, gq, gk, cos, sin)


ATTN_SUB_COLS = 256
ATTN_LOOKAHEAD = 3
ATTN_QK_BALANCE = 4.0
ATTN_P_SHIFT = 8.0


def _attn_kernel(q_ref, kc_ref, vct_ref, kx_ref, vxt_ref, o_ref):
    tq = q_ref.shape[2]
    n_sub = tq // ATTN_SUB_COLS

    def scores(i):
        q = q_ref[0, 0, i * ATTN_SUB_COLS:(i + 1) * ATTN_SUB_COLS, :]
        return _dot_nt(kc_ref[0, 0], q), _dot_nt(kx_ref[0, 0], q)

    pending = [scores(i) for i in range(min(ATTN_LOOKAHEAD, n_sub))]
    for i in range(n_sub):
        if i + ATTN_LOOKAHEAD < n_sub:
            pending.append(scores(i + ATTN_LOOKAHEAD))
        s_c, s_x = pending.pop(0)
        s_c, s_x = s_c.astype(BF16), s_x.astype(BF16)
        m = jnp.maximum(jnp.max(s_c, axis=0, keepdims=True), jnp.max(s_x, axis=0, keepdims=True)) - ATTN_P_SHIFT
        p_c = jnp.exp2(s_c - m).astype(F8)
        p_x = jnp.exp2(s_x - m).astype(F8)
        o_t = _dot(vct_ref[0, 0], p_c) + _dot(vxt_ref[0, 0], p_x)
        hd = o_ref.shape[-1]
        l = o_t[hd:hd + 1, :]
        o_ref[0, i * ATTN_SUB_COLS:(i + 1) * ATTN_SUB_COLS, :] = (o_t[0:hd, :] / l).T.astype(o_ref.dtype)


def _attention(q, k_c, vt_c, k_x, vt_x, tq):
    b, hh, n, dq = q.shape
    n_ctx = k_c.shape[2]
    hv = vt_x.shape[2]
    hd = hv - ATTN_ONES_ROWS
    keys = lambda n_: pl.BlockSpec((1, 1, n_, dq), lambda i, j, t: (i, j, 0, 0))
    vals = lambda n_: pl.BlockSpec((1, 1, hv, n_), lambda i, j, t: (i, j, 0, 0))
    return pl.pallas_call(
        _attn_kernel,
        grid=(b, hh, n // tq),
        in_specs=[pl.BlockSpec((1, 1, tq, dq), lambda i, j, t: (i, j, t, 0)),
                  keys(n_ctx), vals(n_ctx), keys(n), vals(n)],
        out_specs=pl.BlockSpec((1, tq, hd), lambda i, j, t: (i, t, j)),
        out_shape=jax.ShapeDtypeStruct((b, n, hh * hd), BF16),
        compiler_params=_cparams(("parallel", "parallel", "parallel")),
        name="attention",
    )(q, k_c, vt_c, k_x, vt_x)


def _merge_kernel(ya_ref, yb_ref, gate_ref, x_ref, g1_ref, sh2_ref, sc2_ref, n2g_ref,
                  wa_ref, wb_ref, wo_ref, rwt_ref, xmid_ref, h2_ref, aff_ref):
    d = x_ref.shape[-1]
    ga = _sigmoid(gate_ref[0, :, 0:d].astype(F32))
    gb = _sigmoid(gate_ref[0, :, d:2 * d].astype(F32))
    mix = ga * _dot(ya_ref[0], wa_ref[...]) + gb * _dot(yb_ref[0], wb_ref[...])
    mix = _dot(mix.astype(BF16), wo_ref[...])
    xm = x_ref[0] + g1_ref[0] * mix
    xmid_ref[0] = xm
    h2 = _modulated_norm(xm, n2g_ref[...], sh2_ref[0], sc2_ref[0])
    _to_tile_major(h2_ref, h2, lead=(0,))
    rw_hi, rw_lo = _hi_lo(rwt_ref[...])
    h_hi, h_lo = _hi_lo(h2)
    logits = _dot_nt(rw_hi, h_hi) + (_dot_nt(rw_hi, h_lo) + _dot_nt(rw_lo, h_hi))
    mx = jnp.max(logits, axis=0, keepdims=True)
    ex = jnp.exp(logits - mx)
    aff_ref[0] = ex / jnp.sum(ex, axis=0, keepdims=True)


def _merge(ya, yb, gate, x, g1, sh2, sc2, n2g, wa, wb, wo, rwt, tl):
    b, n, d = x.shape
    ne = rwt.shape[0]
    row = lambda w: pl.BlockSpec((1, tl, w), lambda i, j: (i, j, 0))
    vec = pl.BlockSpec((1, 1, d), lambda i, j: (i, 0, 0))
    return pl.pallas_call(
        _merge_kernel,
        grid=(b, n // tl),
        in_specs=[row(d), row(d), row(2 * d), row(d), vec, vec, vec, _const_spec((1, d)),
                  _const_spec(wa.shape), _const_spec(wb.shape), _const_spec(wo.shape), _const_spec(rwt.shape)],
        out_specs=[row(d), pl.BlockSpec((1, tl * SUBLANES, LANES), lambda i, j: (i, j, 0)),
                   pl.BlockSpec((1, ne, tl), lambda i, j: (i, 0, j))],
        out_shape=[jax.ShapeDtypeStruct((b, n, d), F32), jax.ShapeDtypeStruct((b, n * SUBLANES, LANES), F32),
                   jax.ShapeDtypeStruct((b, ne, n), F32)],
        compiler_params=_cparams(("parallel", "parallel")),
        name="merge_router",
    )(ya, yb, gate, x, g1, sh2, sc2, n2g, wa, wb, wo, rwt)


TOPK_RANK_LANES = 128


def _lane_cumsum(x_bf16, tri_bf16, out_ref, block_ref=None):
    ne, n = x_bf16.shape
    carry = jnp.zeros((ne, 1), F32)
    for j in range(n // LANES):
        blk = _dot(x_bf16[:, j * LANES:(j + 1) * LANES], tri_bf16) + carry
        out_ref[:, j * LANES:(j + 1) * LANES] = blk
        if block_ref is not None:
            for ex in range(ne):
                block_ref[ex, j:j + 1, :] = blk[ex:ex + 1, :]
        carry = blk[:, LANES - 1:LANES]


def _topk_kernel(cap, aff_ref, idx_ref, gate_ref, cum_ref, csb_ref, afb_ref):
    aff = aff_ref[0]
    ne, n = aff.shape
    nb = n // LANES
    bits = pltpu.bitcast(aff, jnp.int32)

    def search(i, t):
        cand = t | (jnp.int32(1) << (30 - i))
        cnt = jnp.sum(jnp.where(bits >= cand, 1.0, 0.0), axis=-1, keepdims=True)
        return jnp.where(cnt >= cap, cand, t)

    thr = lax.fori_loop(0, 31, search, jnp.zeros((ne, 1), jnp.int32))
    gt = bits > thr
    eq = bits == thr
    need = cap - jnp.sum(jnp.where(gt, 1.0, 0.0), axis=-1, keepdims=True)
    ri = lax.broadcasted_iota(jnp.int32, (LANES, LANES), 0)
    ci = lax.broadcasted_iota(jnp.int32, (LANES, LANES), 1)
    tri = jnp.where(ri <= ci, 1.0, 0.0).astype(BF16)
    _lane_cumsum(jnp.where(eq, 1.0, 0.0).astype(BF16), tri, cum_ref)
    sel = gt | (eq & (cum_ref[...] <= need))
    affsel = jnp.where(sel, aff, 0.0)
    for j in range(nb):
        for ex in range(ne):
            afb_ref[ex, j:j + 1, :] = affsel[ex:ex + 1, j * LANES:(j + 1) * LANES]
    _lane_cumsum(jnp.where(sel, 1.0, 0.0).astype(BF16), tri, cum_ref, csb_ref)

    group = min(TOPK_RANK_LANES, cap)
    blk_id = lax.broadcasted_iota(jnp.int32, (nb, group), 0).astype(F32)

    def per_expert(e, carry):
        idx_acc, gate_acc = carry
        cs = csb_ref[e]
        af = afb_ref[e]
        cs_hi = jnp.floor(cs * (1.0 / 16.0))
        cs_parts = (cs_hi.astype(BF16), (cs - 16.0 * cs_hi).astype(BF16))
        a1 = af.astype(BF16)
        a2 = (af - a1.astype(F32)).astype(BF16)
        a3 = (af - a1.astype(F32) - a2.astype(F32)).astype(BF16)
        block_end = cs[:, LANES - 1:LANES]
        ranks = [(lax.broadcasted_iota(jnp.int32, (1, group), 1) + r0).astype(F32) for r0 in range(0, cap, group)]
        jsel = [jnp.sum(jnp.where(block_end <= r, 1.0, 0.0), axis=0, keepdims=True) for r in ranks]
        onehot = [jnp.where(blk_id == j, 1.0, 0.0).astype(BF16) for j in jsel]
        rows_hi = [_dot_tn(cs_parts[0], oh) for oh in onehot]
        rows_lo = [_dot_tn(cs_parts[1], oh) for oh in onehot]
        rows_a = [_dot_tn(a1, oh) + (_dot_tn(a2, oh) + _dot_tn(a3, oh)) for oh in onehot]
        pos, gat = [], []
        for r, j, hi, lo, ra in zip(ranks, jsel, rows_hi, rows_lo, rows_a):
            cnt_row = 16.0 * hi + lo
            pos.append(LANES * j + jnp.sum(jnp.where(cnt_row <= r, 1.0, 0.0), axis=0, keepdims=True))
            gat.append(jnp.sum(jnp.where(cnt_row == r + 1.0, ra, 0.0), axis=0, keepdims=True))
        pos = jnp.concatenate(pos, axis=1).astype(jnp.int32)
        gat = jnp.concatenate(gat, axis=1)
        mine = lax.broadcasted_iota(jnp.int32, (ne, cap), 0) == e
        return jnp.where(mine, pos, idx_acc), jnp.where(mine, gat, gate_acc)

    idx, gate = lax.fori_loop(0, ne, per_expert,
                              (jnp.zeros((ne, cap), jnp.int32), jnp.zeros((ne, cap), F32)))
    idx_ref[0] = idx
    gate_ref[0] = gate


def _expert_topk(aff_t, cap):
    b, ne, n = aff_t.shape
    nb = n // LANES
    return pl.pallas_call(
        functools.partial(_topk_kernel, cap),
        grid=(b,),
        in_specs=[pl.BlockSpec((1, ne, n), lambda i: (i, 0, 0))],
        out_specs=[pl.BlockSpec((1, ne, cap), lambda i: (i, 0, 0))] * 2,
        out_shape=[jax.ShapeDtypeStruct((b, ne, cap), jnp.int32), jax.ShapeDtypeStruct((b, ne, cap), F32)],
        scratch_shapes=[pltpu.VMEM((ne, n), F32), pltpu.VMEM((ne, nb, LANES), F32), pltpu.VMEM((ne, nb, LANES), F32)],
        compiler_params=_cparams(("parallel",)),
        name="expert_topk",
    )(aff_t)


MOE_FF_TILE = 768
MOE_UNROLL = 8


def _moe_kernel(cap, idx_ref, gate_ref, h_ref, wg_ref, wu_ref, wd_ref, o_ref, xt_ref, x_ref, y_ref, yt_ref):
    e = pl.program_id(1)
    f = pl.program_id(2)
    nf = pl.num_programs(2)

    @pl.when((e == 0) & (f == 0))
    def _():
        o_ref[...] = jnp.zeros(o_ref.shape, o_ref.dtype)

    @pl.when(f == 0)
    def _():
        def gather(i, carry):
            for u in range(MOE_UNROLL):
                r = i * MOE_UNROLL + u
                xt_ref[_tile(r), :] = h_ref[0, _tile(idx_ref[0, 0, 0, r]), :]
            return carry
        lax.fori_loop(0, cap // MOE_UNROLL, gather, 0)
        x_ref[...] = _from_tile_major(xt_ref, cap).astype(x_ref.dtype)

    x = x_ref[...]
    hid = (_silu(_dot(x, wg_ref[0])) * _dot(x, wu_ref[0])).astype(BF16)
    part = _dot(hid, wd_ref[0])

    @pl.when(f == 0)
    def _():
        y_ref[...] = part

    @pl.when(f > 0)
    def _():
        y_ref[...] += part

    @pl.when(f == nf - 1)
    def _():
        _to_tile_major(yt_ref, y_ref[...])

        def scatter(i, carry):
            rows = [i * MOE_UNROLL + u for u in range(MOE_UNROLL)]
            toks = [idx_ref[0, 0, 0, r] for r in rows]
            new = [o_ref[0, _tile(t), :] + gate_ref[0, 0, 0, r] * yt_ref[_tile(r), :] for r, t in zip(rows, toks)]
            for t, val in zip(toks, new):
                o_ref[0, _tile(t), :] = val
            return carry
        lax.fori_loop(0, cap // MOE_UNROLL, scatter, 0)


def _moe(idx, gate, h2t, wg, wu, wd):
    b, rows, _ = h2t.shape
    ne, d, ffp = wg.shape
    cap = idx.shape[-1]
    tf = MOE_FF_TILE
    smem = lambda: pl.BlockSpec((1, 1, 1, cap), lambda i, j, k: (i, j, 0, 0), memory_space=pltpu.SMEM)
    act = lambda: pl.BlockSpec((1, rows, LANES), lambda i, j, k: (i, 0, 0), pipeline_mode=pl.Buffered(1))
    return pl.pallas_call(
        functools.partial(_moe_kernel, cap),
        grid=(b, ne, ffp // tf),
        in_specs=[smem(), smem(), act(),
                  pl.BlockSpec((1, d, tf), lambda i, j, k: (j, 0, k)),
                  pl.BlockSpec((1, d, tf), lambda i, j, k: (j, 0, k)),
                  pl.BlockSpec((1, tf, d), lambda i, j, k: (j, k, 0))],
        out_specs=act(),
        out_shape=jax.ShapeDtypeStruct((b, rows, LANES), F32),
        scratch_shapes=[pltpu.VMEM((cap * SUBLANES, LANES), F32), pltpu.VMEM((cap, d), BF16),
                        pltpu.VMEM((cap, d), F32), pltpu.VMEM((cap * SUBLANES, LANES), F32)],
        compiler_params=_cparams(("parallel", "arbitrary", "arbitrary")),
        name="expert_ffn",
    )(idx[:, :, None, :], gate[:, :, None, :], h2t, wg, wu, wd)


def _final_kernel(xm_ref, moe_ref, g_ref, o_ref):
    o_ref[0] = xm_ref[0] + g_ref[0] * _from_tile_major(moe_ref, xm_ref.shape[1], lead=(0,))


def _final(xmid, moe_t, g2, tl):
    b, n, d = xmid.shape
    row = pl.BlockSpec((1, tl, d), lambda i, j: (i, j, 0))
    return pl.pallas_call(
        _final_kernel,
        grid=(b, n // tl),
        in_specs=[row, pl.BlockSpec((1, tl * SUBLANES, LANES), lambda i, j: (i, j, 0)),
                  pl.BlockSpec((1, 1, d), lambda i, j: (i, 0, 0))],
        out_specs=row,
        out_shape=jax.ShapeDtypeStruct((b, n, d), F32),
        compiler_params=_cparams(("parallel", "parallel")),
        name="final_residual",
    )(xmid, moe_t, g2)


def _rope_tables(n_lat):
    rows = n_lat // GRID_W
    row = jnp.repeat(jnp.arange(rows), GRID_W).astype(F32)
    col = jnp.broadcast_to(jnp.arange(GRID_W), (rows, GRID_W)).reshape(-1).astype(F32)
    n_freq = MLA_ROPE // 4
    inv_freq = ROPE_BASE ** (-jnp.arange(n_freq, dtype=F32) / n_freq)
    ang_r = row[:, None] * inv_freq
    ang_c = col[:, None] * inv_freq
    ang = jnp.concatenate([ang_r, ang_r, ang_c, ang_c], axis=-1)
    zeros = jnp.zeros((n_lat, LANES - MLA_ROPE), F32)
    return jnp.concatenate([jnp.cos(ang), zeros], axis=-1), jnp.concatenate([jnp.sin(ang), zeros], axis=-1)


def _rotate_half(v, signed=True):
    x1, x2, x3, x4 = jnp.split(v, 4, axis=-1)
    sgn = -1.0 if signed else 1.0
    return jnp.concatenate([sgn * x2, x1, sgn * x4, x3], axis=-1)


def _pad_lanes(v, width):
    return jnp.concatenate([v, jnp.zeros(v.shape[:-1] + (width - v.shape[-1],), v.dtype)], axis=-1)


def kernel(x, c, ctx, c_ctx, ada_w, ada_b, norm1_g, norm2_g, w_in, conv_w, a_log, dt_bias, dn_norm_g, w_out_a, q_a_norm_g, w_uq, kv_a_norm_g, w_ukv, q_norm_g, k_norm_g, w_out_b, w_o, router_w, w_gate, w_up, w_down):
    assert ada_w.shape[0] == 1, "single-layer block"
    b, n_lat, d = x.shape
    n_ctx = ctx.shape[1]
    hh, hd = N_HEADS, HEAD_DIM
    qkv_dim = 3 * hh * hd
    nb = 2 * hh

    n_cond = -(-(b + 1) // SUBLANES) * SUBLANES
    cond = jnp.concatenate([c, c_ctx[None], jnp.zeros((n_cond - b - 1, d), F32)], axis=0)
    mod = _modulation(cond, ada_w[0], ada_b[0])
    mods = [mod[:, i * d:(i + 1) * d] for i in range(6)]
    lat = lambda m: m[:b, None, :]
    cvec = lambda m: jnp.broadcast_to(m[b][None, None, :], (b, 1, d))

    o_z = qkv_dim
    o_alpha = o_z + hh * hd
    o_beta = o_alpha + nb
    o_cq = o_beta + nb
    o_ckv = o_cq + MLA_Q_RANK
    o_kr = o_ckv + MLA_KV_RANK
    o_gate = o_kr + MLA_ROPE
    w = w_in[0]
    w_big = jnp.concatenate([w[:, :o_alpha], w[:, o_gate:], w[:, o_cq:o_ckv]], axis=1).astype(BF16)
    small_w = _pad_lanes(w[:, o_ckv:o_gate], MLA_KV_RANK + LANES).astype(BF16)
    w_ab_t = jnp.transpose(w[:, o_alpha:o_cq]).astype(BF16)
    gate_pad = jnp.zeros((nb, 1), F32)
    alog = jnp.concatenate([a_log[0].reshape(nb, 1), gate_pad], axis=0)
    dtb = jnp.concatenate([dt_bias[0].reshape(nb, 1), gate_pad], axis=0)
    widths = (qkv_dim, hh * hd, N_BRANCHES * d, MLA_Q_RANK)
    g1 = norm1_g[0][None]
    qkv_x, z_x, gt_x, cq_x, sm_x, dg_x = _in_projection(x, lat(mods[0]), lat(mods[1]), g1, w_big, small_w, w_ab_t,
                                                        alog, dtb, widths, min(n_lat, ROW_TILE))
    qkv_c, _, _, cq_c, sm_c, dg_c = _in_projection(ctx, cvec(mods[0]), cvec(mods[1]), g1, w_big, small_w, w_ab_t,
                                                   alog, dtb, widths, min(n_ctx, ROW_TILE))

    y_a = _deltanet(qkv_x, qkv_c, conv_w[0], dg_x, dg_c, z_x, dn_norm_g[0][None])

    wq = w_uq[0].reshape(MLA_Q_RANK, hh, MLA_QK_DIM)
    wq = jnp.concatenate([wq[:, :, :hd].reshape(MLA_Q_RANK, hh * hd),
                          _pad_lanes(wq[:, :, hd:], hd).reshape(MLA_Q_RANK, hh * hd),
                          _pad_lanes(_rotate_half(wq[:, :, hd:]), hd).reshape(MLA_Q_RANK, hh * hd)],
                         axis=1).astype(BF16)
    wkv = w_ukv[0].reshape(MLA_KV_RANK, hh, 2 * hd)
    wk = wkv[:, :, :hd].reshape(MLA_KV_RANK, hh * hd).astype(BF16)
    wvt = jnp.transpose(wkv[:, :, hd:].reshape(MLA_KV_RANK, hh * hd)).astype(BF16)
    gq = jnp.concatenate([_pad_lanes(q_norm_g[0], 2 * hd),
                          _pad_lanes(_rotate_half(q_norm_g[0][hd:], signed=False), hd)])[None]
    gk = _pad_lanes(k_norm_g[0], 2 * hd)[None]
    gqa, gkva = q_a_norm_g[0][None], kv_a_norm_g[0][None]
    cos, sin = _rope_tables(n_lat)
    cos_c = _pad_lanes(jnp.ones((n_ctx, MLA_ROPE), F32), LANES)
    sin_c = jnp.zeros((n_ctx, LANES), F32)
    q_x, k_x, v_x = _mla_projections(cq_x, sm_x, gqa, gkva, wq, wk, wvt, gq, gk, cos, sin, True, min(n_lat, ROW_TILE))
    k_c, v_c = _mla_projections(cq_c, sm_c, gqa, gkva, wq, wk, wvt, gq, gk, cos_c, sin_c, False,
                                min(n_ctx, ROW_TILE))
    y_b = _attention(q_x, k_c, v_c, k_x, v_x, min(n_lat, ATTN_Q_TILE))

    rwt = jnp.transpose(router_w[0])
    x_mid, h2, aff_t = _merge(y_a, y_b, gt_x, x, lat(mods[2]), lat(mods[3]), lat(mods[4]), norm2_g[0][None],
                              w_out_a[0].astype(BF16), w_out_b[0].astype(BF16), w_o[0].astype(BF16), rwt,
                              min(n_lat, ROW_TILE))

    cap = EC_CAPACITY * n_lat // N_EXPERTS
    idx, gate = _expert_topk(aff_t, cap)
    ff = w_gate.shape[-1]
    ffp = -(-ff // MOE_FF_TILE) * MOE_FF_TILE
    wg = _pad_lanes(w_gate[0], ffp).astype(BF16)
    wu = _pad_lanes(w_up[0], ffp).astype(BF16)
    wd = jnp.concatenate([w_down[0], jnp.zeros((N_EXPERTS, ffp - ff, d), F32)], axis=1).astype(BF16)
    moe = _moe(idx, gate, h2, wg, wu, wd)
    return _final(x_mid, moe, lat(mods[5]), min(n_lat, ROW_TILE))
```

```python
import functools
import math

import jax
import jax.numpy as jnp
from jax import lax
from jax.experimental import pallas as pl
from jax.experimental.pallas import tpu as pltpu

F32 = jnp.float32
BF16 = jnp.bfloat16
F8 = jnp.float8_e4m3fn
EPS = 1e-6

N_HEADS = 8
HEAD_DIM = 128
DN_CONV = 5
DN_CHUNK = 64
MLA_Q_RANK = 384
MLA_KV_RANK = 256
MLA_ROPE = 64
MLA_QK_DIM = HEAD_DIM + MLA_ROPE
GRID_W = 64
ROPE_BASE = 10000.0
N_EXPERTS = 16
EC_CAPACITY = 2
N_BRANCHES = 2

LANES = 128
SUBLANES = 8
VMEM_LIMIT_BYTES = 56 * 1024 * 1024

ROW_TILE = 512
ATTN_Q_TILE = 4096

HI = lax.Precision.HIGHEST


def _cparams(sem):
    return pltpu.CompilerParams(dimension_semantics=sem, vmem_limit_bytes=VMEM_LIMIT_BYTES)


def _dot(a, b, precision=None):
    return jnp.dot(a, b, preferred_element_type=F32, precision=precision)


def _dot_nt(a, b, precision=None):
    return lax.dot_general(a, b, (((1,), (1,)), ((), ())), preferred_element_type=F32, precision=precision)


def _dot_tn(a, b):
    return lax.dot_general(a, b, (((0,), (0,)), ((), ())), preferred_element_type=F32)


def _sigmoid(x):
    return 1.0 / (1.0 + jnp.exp(-x))


def _silu(x):
    return x * _sigmoid(x)


def _const_spec(shape):
    nd = len(shape)
    return pl.BlockSpec(shape, lambda *_: (0,) * nd, pipeline_mode=pl.Buffered(1))


def _to_tile_major(ref, x, lead=()):
    rows = x.shape[0]
    for s_ in range(SUBLANES):
        ref[lead + (pl.ds(s_, rows, stride=SUBLANES), slice(None))] = x[:, s_ * LANES:(s_ + 1) * LANES]


def _from_tile_major(ref, rows, lead=()):
    return jnp.concatenate([ref[lead + (pl.ds(s_, rows, stride=SUBLANES), slice(None))] for s_ in range(SUBLANES)],
                           axis=1)


def _tile(i):
    return pl.ds(pl.multiple_of(i * SUBLANES, SUBLANES), SUBLANES)


def _mod_kernel(c_ref, w_ref, b_ref, o_ref):
    c = c_ref[...]
    o_ref[...] = _dot(_silu(c), w_ref[...], precision=HI) + b_ref[...]


def _modulation(cond, ada_w, ada_b):
    n, d = cond.shape
    n_out = ada_w.shape[1]
    tn = d
    return pl.pallas_call(
        _mod_kernel,
        grid=(n_out // tn,),
        in_specs=[pl.BlockSpec((n, d), lambda j: (0, 0)),
                  pl.BlockSpec((d, tn), lambda j: (0, j)),
                  pl.BlockSpec((1, tn), lambda j: (0, j))],
        out_specs=pl.BlockSpec((n, tn), lambda j: (0, j)),
        out_shape=jax.ShapeDtypeStruct((n, n_out), F32),
        compiler_params=_cparams(("parallel",)),
        name="modulation",
    )(cond, ada_w, ada_b.reshape(1, n_out))


def _modulated_norm(x, g, shift, scale):
    ms = jnp.mean(x * x, axis=-1, keepdims=True)
    return (x * lax.rsqrt(ms + EPS)) * g * (1.0 + scale) + shift


def _inproj_kernel(widths, x_ref, shift_ref, scale_ref, g_ref, wb_ref, ws_ref, wab_ref, alog_ref, dtb_ref,
                   *out_refs):
    big_refs, small_ref, gates_ref = out_refs[:-2], out_refs[-2], out_refs[-1]
    h = _modulated_norm(x_ref[0], g_ref[...], shift_ref[0], scale_ref[0]).astype(BF16)
    off = 0
    for o_ref, width in zip(big_refs, widths):
        for c0 in range(0, width, 1024):
            c1 = min(c0 + 1024, width)
            o_ref[0, :, c0:c1] = _dot(h, wb_ref[:, off + c0:off + c1]).astype(o_ref.dtype)
        off += width
    small_ref[0] = _dot(h, ws_ref[...])
    ab = _dot_nt(wab_ref[...], h)
    row_id = lax.broadcasted_iota(jnp.int32, ab.shape, 0)
    xs = ab + dtb_ref[...]
    softplus = jnp.maximum(xs, 0.0) + jnp.log1p(jnp.exp(-jnp.abs(xs)))
    gates_ref[0] = jnp.where(row_id < 2 * N_HEADS, -jnp.exp(alog_ref[...]) * softplus, _sigmoid(ab))


def _in_projection(x, shift, scale, g, w_big, w_small, w_ab_t, alog, dtb, widths, tl):
    b, n, d = x.shape
    row = lambda w: pl.BlockSpec((1, tl, w), lambda i, j: (i, j, 0))
    vec = pl.BlockSpec((1, 1, d), lambda i, j: (i, 0, 0))
    n_gate = w_ab_t.shape[0]
    outs = [jax.ShapeDtypeStruct((b, n, w), BF16) for w in widths]
    outs.append(jax.ShapeDtypeStruct((b, n, w_small.shape[1]), F32))
    outs.append(jax.ShapeDtypeStruct((b, n_gate, n), F32))
    return pl.pallas_call(
        functools.partial(_inproj_kernel, widths),
        grid=(b, n // tl),
        in_specs=[row(d), vec, vec, _const_spec((1, d)), _const_spec(w_big.shape), _const_spec(w_small.shape),
                  _const_spec(w_ab_t.shape), _const_spec(alog.shape), _const_spec(dtb.shape)],
        out_specs=[row(w) for w in widths] + [row(w_small.shape[1]),
                                              pl.BlockSpec((1, n_gate, tl), lambda i, j: (i, 0, j))],
        out_shape=outs,
        compiler_params=_cparams(("parallel", "parallel")),
        name="in_projection",
    )(x, shift, scale, g, w_big, w_small, w_ab_t, alog, dtb)


CONV_ROWS = 512
CONV_PAD = 8


def _conv_fill(raw_ref, pad_ref, n_tok):
    zeros = jnp.zeros((CONV_PAD, LANES), F32)
    pad_ref[0:CONV_PAD, :] = zeros
    pad_ref[CONV_PAD + n_tok:2 * CONV_PAD + n_tok, :] = zeros
    pad_ref[CONV_PAD:CONV_PAD + n_tok, :] = raw_ref[0].astype(F32)


def _conv_block(pad_ref, w, r0, rows, normalize):
    half = DN_CONV // 2
    acc = None
    for j in range(DN_CONV):
        term = pad_ref[pl.ds(r0 + (CONV_PAD + j - half), rows), :] * w[j:j + 1, :]
        acc = term if acc is None else acc + term
    y = _silu(acc)
    if normalize:
        y = y * lax.rsqrt(jnp.sum(y * y, axis=-1, keepdims=True) + EPS)
    return y


def _conv_silu(raw_ref, w_ref, pad_ref, out_ref, n_tok, normalize):
    _conv_fill(raw_ref, pad_ref, n_tok)
    w = w_ref[...]
    rb = min(CONV_ROWS, n_tok)
    for r0 in range(0, n_tok, rb):
        out_ref[r0:r0 + rb, :] = _conv_block(pad_ref, w, r0, rb, normalize)


def _dn_masks():
    n = 2 * DN_CHUNK
    ri = lax.broadcasted_iota(jnp.int32, (n, n), 0)
    ci = lax.broadcasted_iota(jnp.int32, (n, n), 1)
    top = ri < DN_CHUNK
    same = (ri // DN_CHUNK) == (ci // DN_CHUNK)
    sgn = jnp.where(top, 1, -1)
    delta = (ri - ci) * sgn
    incl = same & (delta >= 0)
    strict = same & (delta > 0)
    incl_t = same & (delta <= 0)
    eye = ri == ci
    return top, incl, strict, incl_t, eye


DN_INV_BLOCK = 8
DN_PREP_GROUP = 16
DN_SCAN_UNROLL = 3


def _hi_lo(a_f32):
    hi = a_f32.astype(BF16)
    return hi, (a_f32 - hi.astype(F32)).astype(BF16)


def _split_dot(a_f32, b_bf16):
    hi, lo = _hi_lo(a_f32)
    return _dot(hi, b_bf16) + _dot(lo, b_bf16)


def _bdot(a_f32, b_f32):
    return _dot(a_f32.astype(BF16), b_f32.astype(BF16))


def _dn_gate_rows(c0, group, gt_ref):
    n = 2 * DN_CHUNK
    t0 = pl.multiple_of(c0 * DN_CHUNK, n)
    lane = lax.broadcasted_iota(jnp.int32, (1, n), 1)
    rows = [[gt_ref[0, kind, dr, 0, :, pl.ds(t0, group * DN_CHUNK)] for dr in range(2)] for kind in range(2)]
    out = []
    for j in range(group):
        p, odd = divmod(j, 2)
        both = []
        for kind in range(2):
            fwd = rows[kind][0][:, p * n:(p + 1) * n]
            bwd = rows[kind][1][:, p * n:(p + 1) * n]
            if odd:
                both.append(jnp.where(lane < DN_CHUNK, pltpu.roll(fwd, DN_CHUNK, axis=1), bwd))
            else:
                both.append(jnp.where(lane < DN_CHUNK, fwd, pltpu.roll(bwd, DN_CHUNK, axis=1)))
        out.append(jnp.concatenate(both, axis=0))
    return out


def _dn_prep_load(c, q_ref, k_ref, v_ref):
    r0 = pl.multiple_of(c * DN_CHUNK, DN_CHUNK)
    return q_ref[pl.ds(r0, DN_CHUNK), :], k_ref[pl.ds(r0, DN_CHUNK), :], v_ref[pl.ds(r0, DN_CHUNK), :]


def _dn_prep_compute(q, k, v, gb):
    n = 2 * DN_CHUNK
    top, incl, strict, incl_t, eye = _dn_masks()
    g_row = jnp.broadcast_to(gb[0:1, :], (SUBLANES, n))
    beta_rows = jnp.broadcast_to(gb[1:2, :], (n, n))
    kst = jnp.concatenate([k, k], axis=0)
    qst = jnp.concatenate([q, q], axis=0)
    vst = jnp.concatenate([v, v], axis=0)
    kst_b = kst.astype(BF16)

    gc_row = _split_dot(g_row, jnp.where(incl_t, 1.0, 0.0).astype(BF16))[0:1, :]
    kk = _dot_nt(kst_b, kst_b)
    qk = _dot_nt(qst.astype(BF16), kst_b)
    yield
    c2 = jnp.broadcast_to(gc_row, (n, n))
    c1 = c2.T
    beta_c = beta_rows.T
    tot = jnp.where(top, c1[DN_CHUNK - 1:DN_CHUNK, :], c1[DN_CHUNK:DN_CHUNK + 1, :])
    decay = jnp.exp(jnp.where(incl, c1 - c2, -jnp.inf))
    e_gc = jnp.exp(c1)
    e_rest = jnp.exp(tot - c1)
    scale = HEAD_DIM ** -0.5

    m = jnp.where(strict, kk * beta_c * decay, 0.0)
    ri = lax.broadcasted_iota(jnp.int32, (n, n), 0)
    ci = lax.broadcasted_iota(jnp.int32, (n, n), 1)
    blk = DN_INV_BLOCK
    dg = jnp.where((ri // blk) == (ci // blk), m, 0.0)
    t = jnp.where(eye, 1.0, 0.0) - dg
    dp = _bdot(dg, dg)
    yield
    for it in range(blk.bit_length() - 2):
        t = t + _bdot(t, dp)
        if it + 1 < blk.bit_length() - 2:
            dp = _bdot(dp, dp)
        yield
    while blk < DN_CHUNK:
        off = jnp.where(((ri // (2 * blk)) == (ci // (2 * blk))) & ((ri // blk) != (ci // blk)), m, 0.0)
        ta = _bdot(t, off)
        yield
        t = t - _bdot(ta, t)
        yield
        blk *= 2

    kb = kst * beta_c
    rhs = jnp.concatenate([vst * beta_c, kb * e_gc], axis=1)
    uw = _bdot(t, rhs)
    yield
    d = HEAD_DIM
    u_b = uw[:, 0:d].astype(BF16)
    w_b = uw[:, d:2 * d].astype(BF16)
    kd = (kst * e_rest).astype(BF16)
    qkm = (jnp.where(incl, qk * decay, 0.0) * scale).astype(BF16)
    ri2 = lax.broadcasted_iota(jnp.int32, (n, 2 * d), 0)
    ci2 = lax.broadcasted_iota(jnp.int32, (n, 2 * d), 1)
    own_dir = (ri2 // DN_CHUNK) == (ci2 // d)
    zero = jnp.zeros((), BF16)
    wu_bd = jnp.concatenate([jnp.where(own_dir, jnp.concatenate([w_b, w_b], axis=1), zero),
                             jnp.where(own_dir, jnp.concatenate([u_b, u_b], axis=1), zero)], axis=1)
    kw = _dot_tn(kd, wu_bd)
    qw = _dot(qkm, jnp.concatenate([w_b, u_b], axis=1))
    yield
    return ((-kw[:, 0:2 * d]).astype(BF16),
            kw[:, 2 * d:4 * d],
            (qst * e_gc * scale - qw[:, 0:d]).astype(BF16),
            qw[:, d:2 * d],
            jnp.exp(tot)[DN_CHUNK - 4:DN_CHUNK + 4, :])


def _dn_prep_group(c0, group, in_refs, out_refs):
    *qkv_refs, gt_ref = in_refs
    gates = _dn_gate_rows(c0, group, gt_ref)
    gens = [_dn_prep_compute(*_dn_prep_load(c0 + j, *qkv_refs), gates[j]) for j in range(group)]
    results = [None] * group
    while any(r is None for r in results):
        for j, gen in enumerate(gens):
            try:
                next(gen)
            except StopIteration as done:
                results[j] = done.value
    for j, res in enumerate(results):
        for ref, val in zip(out_refs, res):
            ref[c0 + j] = val


def _dn_scan_step(cf, cb, a_ref, b_ref, q_ref, o0_ref, ar_ref, s_ref, of_ref, ob_ref):
    h = DN_CHUNK
    d = HEAD_DIM
    s = s_ref[...]
    s_b = s.astype(BF16)
    r_f = _dot(jnp.concatenate([a_ref[cf, :, 0:d], q_ref[cf, 0:h, :]], axis=0), s_b[:, 0:d])
    r_b = _dot(jnp.concatenate([a_ref[cb, :, d:2 * d], q_ref[cb, h:2 * h, :]], axis=0), s_b[:, d:2 * d])
    if of_ref is not None:
        of_ref[pl.ds(pl.multiple_of(cf * h, h), h), :] = r_f[d:d + h, :] + o0_ref[cf, 0:h, :]
        ob_ref[pl.ds(pl.multiple_of(cb * h, h), h), :] = r_b[d:d + h, :] + o0_ref[cb, h:2 * h, :]
    b_sel = jnp.concatenate([b_ref[cf, :, 0:d], b_ref[cb, :, d:2 * d]], axis=1)
    a = jnp.concatenate([jnp.broadcast_to(ar_ref[cf, 0:1, :], (d, d)),
                         jnp.broadcast_to(ar_ref[cb, 4:5, :], (d, d))], axis=1)
    s_ref[...] = s * a + jnp.concatenate([r_f[0:d, :], r_b[0:d, :]], axis=1) + b_sel


def _dn_kernel(n_lat, n_ctx,
               qx_ref, kx_ref, vx_ref, qc_ref, kc_ref, vc_ref, wq_ref, wk_ref, wv_ref,
               gbx_ref, gbc_ref, z_ref, ng_ref, y_ref,
               pad_ref, q_s, k_s, v_s, qc_s, kc_s, vc_s,
               a_x, b_x, qp_x, o0_x, ar_x, a_c, b_c, qp_c, o0_c, ar_c,
               s_ref, of_ref, ob_ref):
    nc_x = n_lat // DN_CHUNK
    nc_c = n_ctx // DN_CHUNK

    _conv_silu(qc_ref, wq_ref, pad_ref, qc_s, n_ctx, True)
    _conv_silu(kc_ref, wk_ref, pad_ref, kc_s, n_ctx, True)
    _conv_silu(vc_ref, wv_ref, pad_ref, vc_s, n_ctx, False)
    _conv_silu(qx_ref, wq_ref, pad_ref, q_s, n_lat, True)
    _conv_silu(kx_ref, wk_ref, pad_ref, k_s, n_lat, True)
    _conv_silu(vx_ref, wv_ref, pad_ref, v_s, n_lat, False)
    grp_c = math.gcd(nc_c, DN_PREP_GROUP)
    grp_x = math.gcd(nc_x, DN_PREP_GROUP)
    assert grp_c % 2 == 0 and grp_x % 2 == 0, "a group's tokens must cover whole 128-lane tiles"
    assert nc_x % 2 == 0

    def prep_c(i, carry):
        _dn_prep_group(i * grp_c, grp_c, (qc_s, kc_s, vc_s, gbc_ref), (a_c, b_c, qp_c, o0_c, ar_c))
        return carry

    def prep_x(i, carry):
        _dn_prep_group(i * grp_x, grp_x, (q_s, k_s, v_s, gbx_ref), (a_x, b_x, qp_x, o0_x, ar_x))
        return carry

    lax.fori_loop(0, nc_c // grp_c, prep_c, 0)
    lax.fori_loop(0, nc_x // grp_x, prep_x, 0)

    s_ref[...] = jnp.zeros(s_ref.shape, F32)

    def scan_c(i, carry):
        _dn_scan_step(i, nc_c - 1 - i, a_c, b_c, qp_c, o0_c, ar_c, s_ref, None, None)
        return carry

    def scan_x(i, carry):
        _dn_scan_step(i, nc_x - 1 - i, a_x, b_x, qp_x, o0_x, ar_x, s_ref, of_ref, ob_ref)
        return carry

    def finish(c):
        start = c * DN_CHUNK
        rows = pl.ds(start if isinstance(c, int) else pl.multiple_of(start, DN_CHUNK), DN_CHUNK)
        o = of_ref[rows, :] + ob_ref[rows, :]
        ms = jnp.mean(o * o, axis=-1, keepdims=True)
        y = (o * lax.rsqrt(ms + EPS)) * ng_ref[...] * _silu(z_ref[0, rows, :].astype(F32))
        y_ref[0, rows, :] = y.astype(y_ref.dtype)

    def scan_x_finish(i, carry):
        finish(i - 1)
        finish(nc_x - i)
        scan_x(i, carry)
        return carry

    half = nc_x // 2
    lax.fori_loop(0, nc_c, scan_c, 0)
    lax.fori_loop(0, half + 1, scan_x, 0, unroll=DN_SCAN_UNROLL if (half + 1) % DN_SCAN_UNROLL == 0 else 1)
    lax.fori_loop(half + 1, nc_x, scan_x_finish, 0)
    finish(nc_x - 1)
    finish(0)


def _deltanet(qkv_x, qkv_c, conv_w, gt_x, gt_c, z, norm_g):
    b, n_lat, _ = qkv_x.shape
    n_ctx = qkv_c.shape[1]
    gbx = gt_x.reshape(b, 2, 2, N_HEADS, 1, n_lat)
    gbc = gt_c.reshape(b, 2, 2, N_HEADS, 1, n_ctx)
    hh = N_HEADS
    d = HEAD_DIM
    nc_x, nc_c = n_lat // DN_CHUNK, n_ctx // DN_CHUNK
    tok = lambda n, off: pl.BlockSpec((1, n, d), lambda i, j: (i, 0, off + j))
    cw = lambda off: pl.BlockSpec((DN_CONV, d), lambda i, j: (0, off + j))
    gspec = lambda n: pl.BlockSpec((1, 2, 2, 1, 1, n), lambda i, j: (i, 0, 0, j, 0, 0))
    chunk_scratch = lambda nc: [
        pltpu.VMEM((nc, d, 2 * d), BF16),
        pltpu.VMEM((nc, d, 2 * d), F32),
        pltpu.VMEM((nc, 2 * DN_CHUNK, d), BF16),
        pltpu.VMEM((nc, 2 * DN_CHUNK, d), F32),
        pltpu.VMEM((nc, SUBLANES, 2 * DN_CHUNK), F32),
    ]
    return pl.pallas_call(
        functools.partial(_dn_kernel, n_lat, n_ctx),
        grid=(b, hh),
        in_specs=[tok(n_lat, 0), tok(n_lat, hh), tok(n_lat, 2 * hh),
                  tok(n_ctx, 0), tok(n_ctx, hh), tok(n_ctx, 2 * hh),
                  cw(0), cw(hh), cw(2 * hh),
                  gspec(n_lat), gspec(n_ctx),
                  tok(n_lat, 0), _const_spec((1, d))],
        out_specs=tok(n_lat, 0),
        out_shape=jax.ShapeDtypeStruct((b, n_lat, hh * d), BF16),
        scratch_shapes=[pltpu.VMEM((n_lat + 2 * CONV_PAD, d), F32)]
        + [pltpu.VMEM((n_lat, d), F32)] * 3 + [pltpu.VMEM((n_ctx, d), F32)] * 3
        + chunk_scratch(nc_x) + chunk_scratch(nc_c)
        + [pltpu.VMEM((d, 2 * d), F32), pltpu.VMEM((n_lat, d), F32), pltpu.VMEM((n_lat, d), F32)],
        compiler_params=_cparams(("parallel", "parallel")),
        name="deltanet",
    )(qkv_x, qkv_x, qkv_x, qkv_c, qkv_c, qkv_c, conv_w, conv_w, conv_w, gbx, gbc, z, norm_g)


ATTN_ONES_ROWS = 32


def _rope(x, cos, sin):
    lane = lax.broadcasted_iota(jnp.int32, x.shape, 1)
    quarter = MLA_ROPE // 4
    rot = jnp.where((lane // quarter) % 2 == 0,
                    -pltpu.roll(x, LANES - quarter, axis=1), pltpu.roll(x, quarter, axis=1))
    return x * cos + rot * sin


def _rms(x, n):
    return lax.rsqrt(jnp.sum(x * x, axis=-1, keepdims=True) / n + EPS)


def _mla_proj_kernel(with_q, cq_ref, small_ref, gqa_ref, gkva_ref, wq_ref, wk_ref, wvt_ref, gq_ref, gk_ref,
                     cos_ref, sin_ref, *out_refs):
    hd = HEAD_DIM
    hh = N_HEADS
    cos, sin = cos_ref[...], sin_ref[...]
    small = small_ref[0]
    ckv = small[:, 0:MLA_KV_RANK]
    kpe = small[:, MLA_KV_RANK:MLA_KV_RANK + LANES]
    lane = lax.broadcasted_iota(jnp.int32, kpe.shape, 1)
    kpe = jnp.where(lane < MLA_ROPE, kpe, 0.0)
    kpe_ssq = jnp.sum(kpe * kpe, axis=-1, keepdims=True)
    ckvn = (ckv * _rms(ckv, MLA_KV_RANK) * gkva_ref[...]).astype(BF16)
    kf = _dot(ckvn, wk_ref[...])
    vt = _dot_nt(wvt_ref[...], ckvn)
    if with_q:
        q_ref, k_ref, v_ref = out_refs
        cq = cq_ref[0].astype(F32)
        cqn = (cq * _rms(cq, MLA_Q_RANK) * gqa_ref[...]).astype(BF16)
        qf = _dot(cqn, wq_ref[...])
        q_scale = MLA_QK_DIM ** -0.5 * math.log2(math.e) * ATTN_QK_BALANCE
    else:
        k_ref, v_ref = out_refs
    gq, gk = gq_ref[...], gk_ref[...]
    kpe_rot = _rope(kpe * gk[:, hd:2 * hd], cos, sin)
    for h in range(hh):
        if with_q:
            qn = qf[:, h * hd:(h + 1) * hd]
            qp = qf[:, (hh + h) * hd:(hh + h + 1) * hd]
            ssq = jnp.sum(qn * qn + qp * qp, axis=-1, keepdims=True)
            r = lax.rsqrt(ssq / MLA_QK_DIM + EPS)
            q_ref[0, h, :, 0:hd] = (qn * r * gq[:, 0:hd] * q_scale).astype(q_ref.dtype)
            qr = qf[:, (2 * hh + h) * hd:(2 * hh + h + 1) * hd]
            q_pe = qp * (gq[:, hd:2 * hd] * cos) + qr * (gq[:, 2 * hd:3 * hd] * sin)
            q_ref[0, h, :, hd:2 * hd] = (q_pe * (r * q_scale)).astype(q_ref.dtype)
        kn = kf[:, h * hd:(h + 1) * hd]
        r = lax.rsqrt((jnp.sum(kn * kn, axis=-1, keepdims=True) + kpe_ssq) / MLA_QK_DIM + EPS)
        rk = r * (1.0 / ATTN_QK_BALANCE)
        k_ref[0, h, :, 0:hd] = (kn * rk * gk[:, 0:hd]).astype(k_ref.dtype)
        k_ref[0, h, :, hd:2 * hd] = (kpe_rot * rk).astype(k_ref.dtype)
        v_ref[0, h, 0:hd, :] = vt[h * hd:(h + 1) * hd, :].astype(v_ref.dtype)
        v_ref[0, h, hd:hd + ATTN_ONES_ROWS, :] = jnp.ones((ATTN_ONES_ROWS, vt.shape[1]), v_ref.dtype)


def _mla_projections(cq, small, gqa, gkva, wq, wk, wvt, gq, gk, cos, sin, with_q, tl):
    b, n, _ = small.shape
    hh, hd = N_HEADS, HEAD_DIM
    row = lambda w: pl.BlockSpec((1, tl, w), lambda i, j: (i, j, 0))
    head = lambda w: pl.BlockSpec((1, hh, tl, w), lambda i, j: (i, 0, j, 0))
    tab = pl.BlockSpec((tl, LANES), lambda i, j: (j, 0))
    hv = hd + ATTN_ONES_ROWS
    out_specs = [head(2 * hd), pl.BlockSpec((1, hh, hv, tl), lambda i, j: (i, 0, 0, j))]
    out_shape = [jax.ShapeDtypeStruct((b, hh, n, 2 * hd), F8), jax.ShapeDtypeStruct((b, hh, hv, n), F8)]
    if with_q:
        out_specs = [head(2 * hd)] + out_specs
        out_shape = [jax.ShapeDtypeStruct((b, hh, n, 2 * hd), F8)] + out_shape
    return pl.pallas_call(
        functools.partial(_mla_proj_kernel, with_q),
        grid=(b, n // tl),
        in_specs=[row(cq.shape[-1]), row(small.shape[-1]), _const_spec(gqa.shape), _const_spec(gkva.shape),
                  _const_spec(wq.shape), _const_spec(wk.shape), _const_spec(wvt.shape), _const_spec(gq.shape),
                  _const_spec(gk.shape),
                  tab, tab],
        out_specs=out_specs,
        out_shape=out_shape,
        compiler_params=_cparams(("parallel", "parallel")),
        name="mla_projections_q" if with_q else "mla_projections_kv",
    )(cq, small, gqa, gkva, wq, wk, wvt, gq, gk, cos, sin)


ATTN_SUB_COLS = 256
ATTN_LOOKAHEAD = 4
ATTN_QK_BALANCE = 4.0
ATTN_P_SHIFT = 8.0


def _attn_kernel(q_ref, kc_ref, vct_ref, kx_ref, vxt_ref, o_ref):
    tq = q_ref.shape[2]
    n_sub = tq // ATTN_SUB_COLS

    def scores(i):
        q = q_ref[0, 0, i * ATTN_SUB_COLS:(i + 1) * ATTN_SUB_COLS, :]
        return _dot_nt(kc_ref[0, 0], q), _dot_nt(kx_ref[0, 0], q)

    pending = [scores(i) for i in range(min(ATTN_LOOKAHEAD, n_sub))]
    for i in range(n_sub):
        if i + ATTN_LOOKAHEAD < n_sub:
            pending.append(scores(i + ATTN_LOOKAHEAD))
        s_c, s_x = pending.pop(0)
        s_c, s_x = s_c.astype(BF16), s_x.astype(BF16)
        m = jnp.maximum(jnp.max(s_c, axis=0, keepdims=True), jnp.max(s_x, axis=0, keepdims=True)) - ATTN_P_SHIFT
        p_c = jnp.exp2(s_c - m).astype(F8)
        p_x = jnp.exp2(s_x - m).astype(F8)
        o_t = _dot(vct_ref[0, 0], p_c) + _dot(vxt_ref[0, 0], p_x)
        hd = o_ref.shape[-1]
        l = o_t[hd:hd + 1, :]
        o_ref[0, i * ATTN_SUB_COLS:(i + 1) * ATTN_SUB_COLS, :] = (o_t[0:hd, :] / l).T.astype(o_ref.dtype)


def _attention(q, k_c, vt_c, k_x, vt_x, tq):
    b, hh, n, dq = q.shape
    n_ctx = k_c.shape[2]
    hv = vt_x.shape[2]
    hd = hv - ATTN_ONES_ROWS
    keys = lambda n_: pl.BlockSpec((1, 1, n_, dq), lambda i, j, t: (i, j, 0, 0))
    vals = lambda n_: pl.BlockSpec((1, 1, hv, n_), lambda i, j, t: (i, j, 0, 0))
    return pl.pallas_call(
        _attn_kernel,
        grid=(b, hh, n // tq),
        in_specs=[pl.BlockSpec((1, 1, tq, dq), lambda i, j, t: (i, j, t, 0)),
                  keys(n_ctx), vals(n_ctx), keys(n), vals(n)],
        out_specs=pl.BlockSpec((1, tq, hd), lambda i, j, t: (i, t, j)),
        out_shape=jax.ShapeDtypeStruct((b, n, hh * hd), BF16),
        compiler_params=_cparams(("parallel", "parallel", "parallel")),
        name="attention",
    )(q, k_c, vt_c, k_x, vt_x)


def _merge_kernel(ya_ref, yb_ref, gate_ref, x_ref, g1_ref, sh2_ref, sc2_ref, n2g_ref,
                  wa_ref, wb_ref, wo_ref, rwt_ref, xmid_ref, h2_ref, aff_ref):
    d = x_ref.shape[-1]
    ga = _sigmoid(gate_ref[0, :, 0:d].astype(F32))
    gb = _sigmoid(gate_ref[0, :, d:2 * d].astype(F32))
    mix = ga * _dot(ya_ref[0], wa_ref[...]) + gb * _dot(yb_ref[0], wb_ref[...])
    mix = _dot(mix.astype(BF16), wo_ref[...])
    xm = x_ref[0] + g1_ref[0] * mix
    xmid_ref[0] = xm
    h2 = _modulated_norm(xm, n2g_ref[...], sh2_ref[0], sc2_ref[0])
    _to_tile_major(h2_ref, h2, lead=(0,))
    rw_hi, rw_lo = _hi_lo(rwt_ref[...])
    h_hi, h_lo = _hi_lo(h2)
    logits = _dot_nt(rw_hi, h_hi) + (_dot_nt(rw_hi, h_lo) + _dot_nt(rw_lo, h_hi))
    mx = jnp.max(logits, axis=0, keepdims=True)
    ex = jnp.exp(logits - mx)
    aff_ref[0] = ex / jnp.sum(ex, axis=0, keepdims=True)


def _merge(ya, yb, gate, x, g1, sh2, sc2, n2g, wa, wb, wo, rwt, tl):
    b, n, d = x.shape
    ne = rwt.shape[0]
    row = lambda w: pl.BlockSpec((1, tl, w), lambda i, j: (i, j, 0))
    vec = pl.BlockSpec((1, 1, d), lambda i, j: (i, 0, 0))
    return pl.pallas_call(
        _merge_kernel,
        grid=(b, n // tl),
        in_specs=[row(d), row(d), row(2 * d), row(d), vec, vec, vec, _const_spec((1, d)),
                  _const_spec(wa.shape), _const_spec(wb.shape), _const_spec(wo.shape), _const_spec(rwt.shape)],
        out_specs=[row(d), pl.BlockSpec((1, tl * SUBLANES, LANES), lambda i, j: (i, j, 0)),
                   pl.BlockSpec((1, ne, tl), lambda i, j: (i, 0, j))],
        out_shape=[jax.ShapeDtypeStruct((b, n, d), F32), jax.ShapeDtypeStruct((b, n * SUBLANES, LANES), F32),
                   jax.ShapeDtypeStruct((b, ne, n), F32)],
        compiler_params=_cparams(("parallel", "parallel")),
        name="merge_router",
    )(ya, yb, gate, x, g1, sh2, sc2, n2g, wa, wb, wo, rwt)


TOPK_RANK_LANES = 128


def _lane_cumsum(x_bf16, tri_bf16, out_ref, block_ref=None):
    ne, n = x_bf16.shape
    carry = jnp.zeros((ne, 1), F32)
    for j in range(n // LANES):
        blk = _dot(x_bf16[:, j * LANES:(j + 1) * LANES], tri_bf16) + carry
        out_ref[:, j * LANES:(j + 1) * LANES] = blk
        if block_ref is not None:
            for ex in range(ne):
                block_ref[ex, j:j + 1, :] = blk[ex:ex + 1, :]
        carry = blk[:, LANES - 1:LANES]


def _topk_kernel(cap, aff_ref, idx_ref, gate_ref, cum_ref, csb_ref, afb_ref):
    aff = aff_ref[0]
    ne, n = aff.shape
    nb = n // LANES
    bits = pltpu.bitcast(aff, jnp.int32)

    def search(i, t):
        cand = t | (jnp.int32(1) << (30 - i))
        cnt = jnp.sum(jnp.where(bits >= cand, 1.0, 0.0), axis=-1, keepdims=True)
        return jnp.where(cnt >= cap, cand, t)

    thr = lax.fori_loop(0, 31, search, jnp.zeros((ne, 1), jnp.int32))
    gt = bits > thr
    eq = bits == thr
    need = cap - jnp.sum(jnp.where(gt, 1.0, 0.0), axis=-1, keepdims=True)
    ri = lax.broadcasted_iota(jnp.int32, (LANES, LANES), 0)
    ci = lax.broadcasted_iota(jnp.int32, (LANES, LANES), 1)
    tri = jnp.where(ri <= ci, 1.0, 0.0).astype(BF16)
    _lane_cumsum(jnp.where(eq, 1.0, 0.0).astype(BF16), tri, cum_ref)
    sel = gt | (eq & (cum_ref[...] <= need))
    affsel = jnp.where(sel, aff, 0.0)
    for j in range(nb):
        for ex in range(ne):
            afb_ref[ex, j:j + 1, :] = affsel[ex:ex + 1, j * LANES:(j + 1) * LANES]
    _lane_cumsum(jnp.where(sel, 1.0, 0.0).astype(BF16), tri, cum_ref, csb_ref)

    group = min(TOPK_RANK_LANES, cap)
    blk_id = lax.broadcasted_iota(jnp.int32, (nb, group), 0).astype(F32)

    def per_expert(e, carry):
        idx_acc, gate_acc = carry
        cs = csb_ref[e]
        af = afb_ref[e]
        cs_hi = jnp.floor(cs * (1.0 / 16.0))
        cs_parts = (cs_hi.astype(BF16), (cs - 16.0 * cs_hi).astype(BF16))
        a1 = af.astype(BF16)
        a2 = (af - a1.astype(F32)).astype(BF16)
        a3 = (af - a1.astype(F32) - a2.astype(F32)).astype(BF16)
        block_end = cs[:, LANES - 1:LANES]
        ranks = [(lax.broadcasted_iota(jnp.int32, (1, group), 1) + r0).astype(F32) for r0 in range(0, cap, group)]
        jsel = [jnp.sum(jnp.where(block_end <= r, 1.0, 0.0), axis=0, keepdims=True) for r in ranks]
        onehot = [jnp.where(blk_id == j, 1.0, 0.0).astype(BF16) for j in jsel]
        rows_hi = [_dot_tn(cs_parts[0], oh) for oh in onehot]
        rows_lo = [_dot_tn(cs_parts[1], oh) for oh in onehot]
        rows_a = [_dot_tn(a1, oh) + (_dot_tn(a2, oh) + _dot_tn(a3, oh)) for oh in onehot]
        pos, gat = [], []
        for r, j, hi, lo, ra in zip(ranks, jsel, rows_hi, rows_lo, rows_a):
            cnt_row = 16.0 * hi + lo
            pos.append(LANES * j + jnp.sum(jnp.where(cnt_row <= r, 1.0, 0.0), axis=0, keepdims=True))
            gat.append(jnp.sum(jnp.where(cnt_row == r + 1.0, ra, 0.0), axis=0, keepdims=True))
        pos = jnp.concatenate(pos, axis=1).astype(jnp.int32)
        gat = jnp.concatenate(gat, axis=1)
        mine = lax.broadcasted_iota(jnp.int32, (ne, cap), 0) == e
        return jnp.where(mine, pos, idx_acc), jnp.where(mine, gat, gate_acc)

    idx, gate = lax.fori_loop(0, ne, per_expert,
                              (jnp.zeros((ne, cap), jnp.int32), jnp.zeros((ne, cap), F32)))
    idx_ref[0] = idx
    gate_ref[0] = gate


def _expert_topk(aff_t, cap):
    b, ne, n = aff_t.shape
    nb = n // LANES
    return pl.pallas_call(
        functools.partial(_topk_kernel, cap),
        grid=(b,),
        in_specs=[pl.BlockSpec((1, ne, n), lambda i: (i, 0, 0))],
        out_specs=[pl.BlockSpec((1, ne, cap), lambda i: (i, 0, 0))] * 2,
        out_shape=[jax.ShapeDtypeStruct((b, ne, cap), jnp.int32), jax.ShapeDtypeStruct((b, ne, cap), F32)],
        scratch_shapes=[pltpu.VMEM((ne, n), F32), pltpu.VMEM((ne, nb, LANES), F32), pltpu.VMEM((ne, nb, LANES), F32)],
        compiler_params=_cparams(("parallel",)),
        name="expert_topk",
    )(aff_t)


MOE_FF_TILE = 768
MOE_UNROLL = 8


def _moe_kernel(cap, idx_ref, gate_ref, h_ref, wg_ref, wu_ref, wd_ref, o_ref, xt_ref, x_ref, y_ref, yt_ref):
    e = pl.program_id(1)
    f = pl.program_id(2)
    nf = pl.num_programs(2)

    @pl.when((e == 0) & (f == 0))
    def _():
        o_ref[...] = jnp.zeros(o_ref.shape, o_ref.dtype)

    @pl.when(f == 0)
    def _():
        def gather(i, carry):
            for u in range(MOE_UNROLL):
                r = i * MOE_UNROLL + u
                xt_ref[_tile(r), :] = h_ref[0, _tile(idx_ref[0, 0, 0, r]), :]
            return carry
        lax.fori_loop(0, cap // MOE_UNROLL, gather, 0)
        x_ref[...] = _from_tile_major(xt_ref, cap).astype(x_ref.dtype)

    x = x_ref[...]
    hid = (_silu(_dot(x, wg_ref[0])) * _dot(x, wu_ref[0])).astype(BF16)
    part = _dot(hid, wd_ref[0])

    @pl.when(f == 0)
    def _():
        y_ref[...] = part

    @pl.when(f > 0)
    def _():
        y_ref[...] += part

    @pl.when(f == nf - 1)
    def _():
        _to_tile_major(yt_ref, y_ref[...])

        def scatter(i, carry):
            rows = [i * MOE_UNROLL + u for u in range(MOE_UNROLL)]
            toks = [idx_ref[0, 0, 0, r] for r in rows]
            new = [o_ref[0, _tile(t), :] + gate_ref[0, 0, 0, r] * yt_ref[_tile(r), :] for r, t in zip(rows, toks)]
            for t, val in zip(toks, new):
                o_ref[0, _tile(t), :] = val
            return carry
        lax.fori_loop(0, cap // MOE_UNROLL, scatter, 0)


def _moe(idx, gate, h2t, wg, wu, wd):
    b, rows, _ = h2t.shape
    ne, d, ffp = wg.shape
    cap = idx.shape[-1]
    tf = MOE_FF_TILE
    smem = lambda: pl.BlockSpec((1, 1, 1, cap), lambda i, j, k: (i, j, 0, 0), memory_space=pltpu.SMEM)
    act = lambda: pl.BlockSpec((1, rows, LANES), lambda i, j, k: (i, 0, 0), pipeline_mode=pl.Buffered(1))
    return pl.pallas_call(
        functools.partial(_moe_kernel, cap),
        grid=(b, ne, ffp // tf),
        in_specs=[smem(), smem(), act(),
                  pl.BlockSpec((1, d, tf), lambda i, j, k: (j, 0, k)),
                  pl.BlockSpec((1, d, tf), lambda i, j, k: (j, 0, k)),
                  pl.BlockSpec((1, tf, d), lambda i, j, k: (j, k, 0))],
        out_specs=act(),
        out_shape=jax.ShapeDtypeStruct((b, rows, LANES), F32),
        scratch_shapes=[pltpu.VMEM((cap * SUBLANES, LANES), F32), pltpu.VMEM((cap, d), BF16),
                        pltpu.VMEM((cap, d), F32), pltpu.VMEM((cap * SUBLANES, LANES), F32)],
        compiler_params=_cparams(("parallel", "arbitrary", "arbitrary")),
        name="expert_ffn",
    )(idx[:, :, None, :], gate[:, :, None, :], h2t, wg, wu, wd)


def _final_kernel(xm_ref, moe_ref, g_ref, o_ref):
    o_ref[0] = xm_ref[0] + g_ref[0] * _from_tile_major(moe_ref, xm_ref.shape[1], lead=(0,))


def _final(xmid, moe_t, g2, tl):
    b, n, d = xmid.shape
    row = pl.BlockSpec((1, tl, d), lambda i, j: (i, j, 0))
    return pl.pallas_call(
        _final_kernel,
        grid=(b, n // tl),
        in_specs=[row, pl.BlockSpec((1, tl * SUBLANES, LANES), lambda i, j: (i, j, 0)),
                  pl.BlockSpec((1, 1, d), lambda i, j: (i, 0, 0))],
        out_specs=row,
        out_shape=jax.ShapeDtypeStruct((b, n, d), F32),
        compiler_params=_cparams(("parallel", "parallel")),
        name="final_residual",
    )(xmid, moe_t, g2)


def _rope_tables(n_lat):
    rows = n_lat // GRID_W
    row = jnp.repeat(jnp.arange(rows), GRID_W).astype(F32)
    col = jnp.broadcast_to(jnp.arange(GRID_W), (rows, GRID_W)).reshape(-1).astype(F32)
    n_freq = MLA_ROPE // 4
    inv_freq = ROPE_BASE ** (-jnp.arange(n_freq, dtype=F32) / n_freq)
    ang_r = row[:, None] * inv_freq
    ang_c = col[:, None] * inv_freq
    ang = jnp.concatenate([ang_r, ang_r, ang_c, ang_c], axis=-1)
    zeros = jnp.zeros((n_lat, LANES - MLA_ROPE), F32)
    return jnp.concatenate([jnp.cos(ang), zeros], axis=-1), jnp.concatenate([jnp.sin(ang), zeros], axis=-1)


def _rotate_half(v, signed=True):
    x1, x2, x3, x4 = jnp.split(v, 4, axis=-1)
    sgn = -1.0 if signed else 1.0
    return jnp.concatenate([sgn * x2, x1, sgn * x4, x3], axis=-1)


def _pad_lanes(v, width):
    return jnp.concatenate([v, jnp.zeros(v.shape[:-1] + (width - v.shape[-1],), v.dtype)], axis=-1)


def kernel(x, c, ctx, c_ctx, ada_w, ada_b, norm1_g, norm2_g, w_in, conv_w, a_log, dt_bias, dn_norm_g, w_out_a, q_a_norm_g, w_uq, kv_a_norm_g, w_ukv, q_norm_g, k_norm_g, w_out_b, w_o, router_w, w_gate, w_up, w_down):
    assert ada_w.shape[0] == 1, "single-layer block"
    b, n_lat, d = x.shape
    n_ctx = ctx.shape[1]
    hh, hd = N_HEADS, HEAD_DIM
    qkv_dim = 3 * hh * hd
    nb = 2 * hh

    n_cond = -(-(b + 1) // SUBLANES) * SUBLANES
    cond = jnp.concatenate([c, c_ctx[None], jnp.zeros((n_cond - b - 1, d), F32)], axis=0)
    mod = _modulation(cond, ada_w[0], ada_b[0])
    mods = [mod[:, i * d:(i + 1) * d] for i in range(6)]
    lat = lambda m: m[:b, None, :]
    cvec = lambda m: jnp.broadcast_to(m[b][None, None, :], (b, 1, d))

    o_z = qkv_dim
    o_alpha = o_z + hh * hd
    o_beta = o_alpha + nb
    o_cq = o_beta + nb
    o_ckv = o_cq + MLA_Q_RANK
    o_kr = o_ckv + MLA_KV_RANK
    o_gate = o_kr + MLA_ROPE
    w = w_in[0]
    w_big = jnp.concatenate([w[:, :o_alpha], w[:, o_gate:], w[:, o_cq:o_ckv]], axis=1).astype(BF16)
    small_w = _pad_lanes(w[:, o_ckv:o_gate], MLA_KV_RANK + LANES).astype(BF16)
    w_ab_t = jnp.transpose(w[:, o_alpha:o_cq]).astype(BF16)
    gate_pad = jnp.zeros((nb, 1), F32)
    alog = jnp.concatenate([a_log[0].reshape(nb, 1), gate_pad], axis=0)
    dtb = jnp.concatenate([dt_bias[0].reshape(nb, 1), gate_pad], axis=0)
    widths = (qkv_dim, hh * hd, N_BRANCHES * d, MLA_Q_RANK)
    g1 = norm1_g[0][None]
    qkv_x, z_x, gt_x, cq_x, sm_x, dg_x = _in_projection(x, lat(mods[0]), lat(mods[1]), g1, w_big, small_w, w_ab_t,
                                                        alog, dtb, widths, min(n_lat, ROW_TILE))
    qkv_c, _, _, cq_c, sm_c, dg_c = _in_projection(ctx, cvec(mods[0]), cvec(mods[1]), g1, w_big, small_w, w_ab_t,
                                                   alog, dtb, widths, min(n_ctx, ROW_TILE))

    y_a = _deltanet(qkv_x, qkv_c, conv_w[0], dg_x, dg_c, z_x, dn_norm_g[0][None])

    wq = w_uq[0].reshape(MLA_Q_RANK, hh, MLA_QK_DIM)
    wq = jnp.concatenate([wq[:, :, :hd].reshape(MLA_Q_RANK, hh * hd),
                          _pad_lanes(wq[:, :, hd:], hd).reshape(MLA_Q_RANK, hh * hd),
                          _pad_lanes(_rotate_half(wq[:, :, hd:]), hd).reshape(MLA_Q_RANK, hh * hd)],
                         axis=1).astype(BF16)
    wkv = w_ukv[0].reshape(MLA_KV_RANK, hh, 2 * hd)
    wk = wkv[:, :, :hd].reshape(MLA_KV_RANK, hh * hd).astype(BF16)
    wvt = jnp.transpose(wkv[:, :, hd:].reshape(MLA_KV_RANK, hh * hd)).astype(BF16)
    gq = jnp.concatenate([_pad_lanes(q_norm_g[0], 2 * hd),
                          _pad_lanes(_rotate_half(q_norm_g[0][hd:], signed=False), hd)])[None]
    gk = _pad_lanes(k_norm_g[0], 2 * hd)[None]
    gqa, gkva = q_a_norm_g[0][None], kv_a_norm_g[0][None]
    cos, sin = _rope_tables(n_lat)
    cos_c = _pad_lanes(jnp.ones((n_ctx, MLA_ROPE), F32), LANES)
    sin_c = jnp.zeros((n_ctx, LANES), F32)
    q_x, k_x, v_x = _mla_projections(cq_x, sm_x, gqa, gkva, wq, wk, wvt, gq, gk, cos, sin, True, min(n_lat, ROW_TILE))
    k_c, v_c = _mla_projections(cq_c, sm_c, gqa, gkva, wq, wk, wvt, gq, gk, cos_c, sin_c, False,
                                min(n_ctx, ROW_TILE))
    y_b = _attention(q_x, k_c, v_c, k_x, v_x, min(n_lat, ATTN_Q_TILE))

    rwt = jnp.transpose(router_w[0])
    x_mid, h2, aff_t = _merge(y_a, y_b, gt_x, x, lat(mods[2]), lat(mods[3]), lat(mods[4]), norm2_g[0][None],
                              w_out_a[0].astype(BF16), w_out_b[0].astype(BF16), w_o[0].astype(BF16), rwt,
                              min(n_lat, ROW_TILE))

    cap = EC_CAPACITY * n_lat // N_EXPERTS
    idx, gate = _expert_topk(aff_t, cap)
    ff = w_gate.shape[-1]
    ffp = -(-ff // MOE_FF_TILE) * MOE_FF_TILE
    wg = _pad_lanes(w_gate[0], ffp).astype(BF16)
    wu = _pad_lanes(w_up[0], ffp).astype(BF16)
    wd = jnp.concatenate([w_down[0], jnp.zeros((N_EXPERTS, ffp - ff, d), F32)], axis=1).astype(BF16)
    moe = _moe(idx, gate, h2, wg, wu, wd)
    return _final(x_mid, moe, lat(mods[5]), min(n_lat, ROW_TILE))
```
